```python
import jax
import jax.numpy as jnp
from jax import lax
import numpy as np

D_MODEL = 2048
BATCH = 2
SEQ = 4096
DEPTH = 4

N_MIXERS = 3
NSA_HEADS = 16
NSA_KV_GROUPS = 4
HEAD_DIM = D_MODEL // NSA_HEADS
HEADS_PER_GROUP = NSA_HEADS // NSA_KV_GROUPS
ROPE_DIM = HEAD_DIM // 4
ROPE_THETA = 500000.0
CMP_BLOCK = 32
CMP_STRIDE = 16
CMP_HIDDEN = 256
SEL_BLOCK = 64
SEL_TOP = 16
WINDOW = 512
ATTN_QBLOCK = 128
SEL_QCHUNK = 64
CONV_WIDTH = 3
SGU_WIDTH = D_MODEL
SGU_CHUNK = 128
SGU_GROUPS = 8
SGU_GROUP_DIM = SGU_WIDTH // SGU_GROUPS
D_FF = 5632
PLE_DIM = 256
N_NORMS = 8
EPS = 1e-6

kernel_name = "hybrid_nsa_conv_sgu_macaron_trunk"


def rms_norm(x, g):
    xf = x.astype(jnp.float32)
    y = xf * lax.rsqrt(jnp.mean(xf * xf, axis=-1, keepdims=True) + EPS) * g.astype(jnp.float32)
    return y.astype(x.dtype)


def layer_norm(x, g, b):
    xf = x.astype(jnp.float32)
    mu = jnp.mean(xf, axis=-1, keepdims=True)
    var = jnp.mean(jnp.square(xf - mu), axis=-1, keepdims=True)
    y = (xf - mu) * lax.rsqrt(var + EPS) * g.astype(jnp.float32) + b.astype(jnp.float32)
    return y.astype(x.dtype)


def swiglu(x, wg, wu, wd):
    return (jax.nn.silu(x @ wg) * (x @ wu)) @ wd


def masked_softmax(s, mask):
    s = jnp.where(mask, s, -jnp.inf)
    m = jnp.max(s, axis=-1, keepdims=True)
    m = jnp.where(jnp.isfinite(m), m, 0.0)
    e = jnp.where(mask, jnp.exp(s - m), 0.0)
    return e / jnp.maximum(jnp.sum(e, axis=-1, keepdims=True), 1e-30)


def partial_rotary(x, pos):
    half = ROPE_DIM // 2
    inv_freq = jnp.power(jnp.float32(ROPE_THETA), -jnp.arange(0, ROPE_DIM, 2, dtype=jnp.float32) / ROPE_DIM)
    ang = pos.astype(jnp.float32)[:, None] * inv_freq[None, :]
    cos = jnp.cos(ang)[None, :, None, :]
    sin = jnp.sin(ang)[None, :, None, :]
    xr = x[..., :ROPE_DIM].astype(jnp.float32)
    x1, x2 = xr[..., :half], xr[..., half:]
    rot = jnp.concatenate([x1 * cos - x2 * sin, x2 * cos + x1 * sin], axis=-1)
    return jnp.concatenate([rot.astype(x.dtype), x[..., ROPE_DIM:]], axis=-1)


def compress_blocks(kv, pe, w1, b1, w2):
    b_, s_, g_, dh = kv.shape
    n_sub = CMP_BLOCK // CMP_STRIDE
    nc = s_ // CMP_STRIDE - n_sub + 1
    ch = kv.reshape(b_, s_ // CMP_STRIDE, CMP_STRIDE, g_, dh)
    blocks = jnp.concatenate([ch[:, i:i + nc] for i in range(n_sub)], axis=2)
    blocks = blocks + pe[None, None, :, None, :]
    flat = jnp.moveaxis(blocks, 3, 2).reshape(b_, nc, g_, CMP_BLOCK * dh)
    return jax.nn.silu(flat @ w1 + b1) @ w2


def nsa_mixer(h, w_in, w_out, phi_pe, phi_w1, phi_b1, phi_w2):
    b_, s_, _ = h.shape
    H, G, R, Dh = NSA_HEADS, NSA_KV_GROUPS, HEADS_PER_GROUP, HEAD_DIM
    dt = h.dtype
    scale = HEAD_DIM ** -0.5
    sizes = [H * Dh] + [G * Dh] * 6 + [3 * H]
    offs = []
    acc = 0
    for sz in sizes[:-1]:
        acc += sz
        offs.append(acc)
    q, k_c, v_c, k_s, v_s, k_w, v_w, gates = jnp.split(h @ w_in, offs, axis=-1)
    q = q.reshape(b_, s_, G, R, Dh)
    k_c, v_c, k_s, v_s, k_w, v_w = [t.reshape(b_, s_, G, Dh) for t in (k_c, v_c, k_s, v_s, k_w, v_w)]
    pos = jnp.arange(s_)

    kc = compress_blocks(k_c, phi_pe[0], phi_w1[0], phi_b1[0], phi_w2[0])
    vc = compress_blocks(v_c, phi_pe[1], phi_w1[1], phi_b1[1], phi_w2[1])
    nc = kc.shape[1]
    sc = jnp.einsum('btgrd,bcgd->bgrtc', q, kc).astype(jnp.float32) * scale
    c_start = jnp.arange(nc) * CMP_STRIDE
    c_end = c_start + CMP_BLOCK - 1
    p_cmp = masked_softmax(sc, c_end[None, :] <= pos[:, None])
    o_cmp = jnp.einsum('bgrtc,bcgd->btgrd', p_cmp.astype(dt), vc)

    n_sel = s_ // SEL_BLOCK
    j_start = jnp.arange(n_sel) * SEL_BLOCK
    j_end = j_start + SEL_BLOCK - 1
    overlap = jnp.clip(jnp.minimum(c_end[:, None], j_end[None, :]) - jnp.maximum(c_start[:, None], j_start[None, :]) + 1, 0)
    overlap = overlap.astype(jnp.float32) / CMP_STRIDE
    imp = jnp.einsum('bgrtc,cj->bgtj', p_cmp, overlap)
    cur = pos // SEL_BLOCK
    blk = jnp.arange(n_sel)
    imp = jnp.where(blk[None, :] > cur[:, None], -jnp.inf, imp)
    forced = (blk[None, :] == 0) | (blk[None, :] == cur[:, None]) | (blk[None, :] == cur[:, None] - 1)
    imp = jnp.where(forced, jnp.inf, imp)
    n_top = min(SEL_TOP, n_sel)
    _, idx = lax.top_k(imp, n_top)

    q_r = partial_rotary(q.reshape(b_, s_, H, Dh), pos).reshape(b_, s_, G, R, Dh)
    k_s = partial_rotary(k_s, pos)
    k_w = partial_rotary(k_w, pos)

    ks_blk = jnp.transpose(k_s.reshape(b_, n_sel, SEL_BLOCK, G, Dh), (0, 3, 1, 2, 4))
    vs_blk = jnp.transpose(v_s.reshape(b_, n_sel, SEL_BLOCK, G, Dh), (0, 3, 1, 2, 4))
    nq = s_ // SEL_QCHUNK
    q_chunks = jnp.swapaxes(q_r.reshape(b_, nq, SEL_QCHUNK, G, R, Dh), 0, 1)
    idx_chunks = jnp.swapaxes(jnp.transpose(idx, (0, 2, 1, 3)).reshape(b_, nq, SEL_QCHUNK, G, n_top), 0, 1)
    pos_chunks = pos.reshape(nq, SEL_QCHUNK)
    bi = jnp.arange(b_)[:, None, None, None]
    gi = jnp.arange(G)[None, None, :, None]

    def sel_chunk(args):
        qc, ic, tc = args
        kg = ks_blk[bi, gi, ic]
        vg = vs_blk[bi, gi, ic]
        s = jnp.einsum('btgrd,btgnjd->btgrnj', qc, kg).astype(jnp.float32) * scale
        kpos = ic[..., None] * SEL_BLOCK + jnp.arange(SEL_BLOCK)
        m = (kpos <= tc[None, :, None, None, None]).reshape(b_, SEL_QCHUNK, G, 1, n_top * SEL_BLOCK)
        pr = masked_softmax(s.reshape(b_, SEL_QCHUNK, G, R, n_top * SEL_BLOCK), m)
        return jnp.einsum('btgrk,btgkd->btgrd', pr.astype(dt), vg.reshape(b_, SEL_QCHUNK, G, n_top * SEL_BLOCK, Dh))

    o_sel = lax.map(sel_chunk, (q_chunks, idx_chunks, pos_chunks))
    o_sel = jnp.swapaxes(o_sel, 0, 1).reshape(b_, s_, G, R, Dh)

    nb = s_ // ATTN_QBLOCK
    n_prev = WINDOW // ATTN_QBLOCK
    band = (n_prev + 1) * ATTN_QBLOCK
    pad = ((0, 0), (n_prev, 0), (0, 0), (0, 0), (0, 0))
    kw_p = jnp.pad(k_w.reshape(b_, nb, ATTN_QBLOCK, G, Dh), pad)
    vw_p = jnp.pad(v_w.reshape(b_, nb, ATTN_QBLOCK, G, Dh), pad)
    k_band = jnp.concatenate([kw_p[:, i:i + nb] for i in range(n_prev + 1)], axis=2)
    v_band = jnp.concatenate([vw_p[:, i:i + nb] for i in range(n_prev + 1)], axis=2)
    qw = q_r.reshape(b_, nb, ATTN_QBLOCK, G, R, Dh)
    sw = jnp.einsum('bntgrd,bnsgd->bngrts', qw, k_band).astype(jnp.float32) * scale
    tpos = jnp.arange(nb)[:, None] * ATTN_QBLOCK + jnp.arange(ATTN_QBLOCK)[None, :]
    spos = (jnp.arange(nb)[:, None] - n_prev) * ATTN_QBLOCK + jnp.arange(band)[None, :]
    diff = tpos[:, :, None] - spos[:, None, :]
    mw = (spos[:, None, :] >= 0) & (diff >= 0) & (diff < WINDOW)
    pw = masked_softmax(sw, mw[None, :, None, None])
    o_win = jnp.einsum('bngrts,bnsgd->bntgrd', pw.astype(dt), v_band).reshape(b_, s_, G, R, Dh)

    g = jax.nn.sigmoid(gates.astype(jnp.float32)).reshape(b_, s_, 3, G, R, 1)
    o = (g[:, :, 0] * o_cmp.astype(jnp.float32) + g[:, :, 1] * o_sel.astype(jnp.float32)
         + g[:, :, 2] * o_win.astype(jnp.float32)).astype(dt)
    return o.reshape(b_, s_, H * Dh) @ w_out


def conv_mixer(h, w_in, conv_w, w_out):
    b_gate, c_gate, u = jnp.split(h @ w_in, 3, axis=-1)
    z = c_gate * u
    z = lax.conv_general_dilated(z, conv_w[:, None, :], window_strides=(1,), padding=[(CONV_WIDTH - 1, 0)],
                                 dimension_numbers=('NWC', 'WIO', 'NWC'), feature_group_count=D_MODEL)
    return (b_gate * z) @ w_out


def sgu_mixer(h, w_in, ln_g, ln_b, w_s, b_s, w_out):
    b_, s_, _ = h.shape
    z = jax.nn.gelu(h @ w_in)
    u, v = jnp.split(z, 2, axis=-1)
    v = layer_norm(v, ln_g, ln_b)
    vc = v.reshape(b_, s_ // SGU_CHUNK, SGU_CHUNK, SGU_GROUPS, SGU_GROUP_DIM)
    causal = jnp.tril(jnp.ones((SGU_CHUNK, SGU_CHUNK), dtype=bool))
    ws = jnp.where(causal, w_s, 0)
    sv = jnp.einsum('gts,bcsgd->bctgd', ws, vc) + jnp.swapaxes(b_s, 0, 1)[None, None, :, :, None]
    return (u * sv.reshape(b_, s_, SGU_WIDTH)) @ w_out


def _normal(key, shape, scale):
    return jax.random.normal(key, shape, jnp.float32) * scale


def setup_inputs(seed: int = 0) -> dict:
    key = jax.random.key(seed)
    ks = jax.random.split(key, 26)
    D = D_MODEL
    n_a = len(range(0, DEPTH, N_MIXERS))
    n_b = len(range(1, DEPTH, N_MIXERS))
    n_c = len(range(2, DEPTH, N_MIXERS))
    nsa_in = NSA_HEADS * HEAD_DIM + 6 * NSA_KV_GROUPS * HEAD_DIM + 3 * NSA_HEADS
    return {
        "x": _normal(ks[0], (BATCH, SEQ, D), 1.0),
        "p": _normal(ks[1], (DEPTH, BATCH, SEQ, PLE_DIM), 1.0),
        "norm_g": 1.0 + _normal(ks[2], (DEPTH, N_NORMS, D), 0.02),
        "ffn1_wg": _normal(ks[3], (DEPTH, D, D_FF), D ** -0.5),
        "ffn1_wu": _normal(ks[4], (DEPTH, D, D_FF), D ** -0.5),
        "ffn1_wd": _normal(ks[5], (DEPTH, D_FF, D), D_FF ** -0.5),
        "ffn2_wg": _normal(ks[6], (DEPTH, D, D_FF), D ** -0.5),
        "ffn2_wu": _normal(ks[7], (DEPTH, D, D_FF), D ** -0.5),
        "ffn2_wd": _normal(ks[8], (DEPTH, D_FF, D), D_FF ** -0.5),
        "ple_wg": _normal(ks[9], (DEPTH, D, D), D ** -0.5),
        "ple_wp": _normal(ks[10], (DEPTH, PLE_DIM, D), PLE_DIM ** -0.5),
        "nsa_w_in": _normal(ks[11], (n_a, D, nsa_in), D ** -0.5),
        "nsa_w_out": _normal(ks[12], (n_a, NSA_HEADS * HEAD_DIM, D), (NSA_HEADS * HEAD_DIM) ** -0.5),
        "nsa_phi_pe": _normal(ks[13], (n_a, 2, CMP_BLOCK, HEAD_DIM), 0.02),
        "nsa_phi_w1": _normal(ks[14], (n_a, 2, CMP_BLOCK * HEAD_DIM, CMP_HIDDEN), (CMP_BLOCK * HEAD_DIM) ** -0.5),
        "nsa_phi_b1": _normal(ks[15], (n_a, 2, CMP_HIDDEN), 0.02),
        "nsa_phi_w2": _normal(ks[16], (n_a, 2, CMP_HIDDEN, HEAD_DIM), CMP_HIDDEN ** -0.5),
        "conv_w_in": _normal(ks[17], (n_b, D, 3 * D), D ** -0.5),
        "conv_w": _normal(ks[18], (n_b, CONV_WIDTH, D), CONV_WIDTH ** -0.5),
        "conv_w_out": _normal(ks[19], (n_b, D, D), D ** -0.5),
        "sgu_w_in": _normal(ks[20], (n_c, D, 2 * SGU_WIDTH), D ** -0.5),
        "sgu_ln_g": 1.0 + _normal(ks[21], (n_c, SGU_WIDTH), 0.02),
        "sgu_ln_b": _normal(ks[22], (n_c, SGU_WIDTH), 0.02),
        "sgu_w_s": _normal(ks[23], (n_c, SGU_GROUPS, SGU_CHUNK, SGU_CHUNK), SGU_CHUNK ** -0.5),
        "sgu_b_s": 1.0 + _normal(ks[24], (n_c, SGU_GROUPS, SGU_CHUNK), 0.02),
        "sgu_w_out": _normal(ks[25], (n_c, SGU_WIDTH, D), SGU_WIDTH ** -0.5),
    }


def reference(x, p, norm_g, ffn1_wg, ffn1_wu, ffn1_wd, ffn2_wg, ffn2_wu, ffn2_wd, ple_wg, ple_wp,
              nsa_w_in, nsa_w_out, nsa_phi_pe, nsa_phi_w1, nsa_phi_b1, nsa_phi_w2,
              conv_w_in, conv_w, conv_w_out,
              sgu_w_in, sgu_ln_g, sgu_ln_b, sgu_w_s, sgu_b_s, sgu_w_out):
    h = x
    for i in range(DEPTH):
        g = norm_g[i]
        j = i // N_MIXERS
        h = h + 0.5 * rms_norm(swiglu(rms_norm(h, g[0]), ffn1_wg[i], ffn1_wu[i], ffn1_wd[i]), g[1])
        hn = rms_norm(h, g[2])
        kind = i % N_MIXERS
        if kind == 0:
            m = nsa_mixer(hn, nsa_w_in[j], nsa_w_out[j], nsa_phi_pe[j], nsa_phi_w1[j], nsa_phi_b1[j], nsa_phi_w2[j])
        elif kind == 1:
            m = conv_mixer(hn, conv_w_in[j], conv_w[j], conv_w_out[j])
        else:
            m = sgu_mixer(hn, sgu_w_in[j], sgu_ln_g[j], sgu_ln_b[j], sgu_w_s[j], sgu_b_s[j], sgu_w_out[j])
        h = h + rms_norm(m, g[3])
        h = h + 0.5 * rms_norm(swiglu(rms_norm(h, g[4]), ffn2_wg[i], ffn2_wu[i], ffn2_wd[i]), g[5])
        gate = jax.nn.sigmoid(rms_norm(h, g[6]) @ ple_wg[i])
        h = h + rms_norm(gate * (p[i] @ ple_wp[i]), g[7])
    return h
```

```python
import functools

import jax
import jax.numpy as jnp
from jax import lax
from jax.experimental import pallas as pl
from jax.experimental.pallas import tpu as pltpu

F32 = jnp.float32
BF16 = jnp.bfloat16

EPS = 1e-6
LANES = 128
D_MODEL = 2048
D_FF = 5632
PLE_DIM = 256
N_NORMS = 8
N_MIXERS = 3
NSA_HEADS = 16
NSA_GROUPS = 4
HEADS_PER_GROUP = NSA_HEADS // NSA_GROUPS
HEAD_DIM = D_MODEL // NSA_HEADS
ROPE_DIM = HEAD_DIM // 4
ROPE_THETA = 500000.0
CMP_BLOCK = 32
CMP_STRIDE = 16
CMP_HIDDEN = 256
SEL_BLOCK = 64
SEL_TOP = 16
WINDOW = 512
QKV_WIDTH = NSA_HEADS * HEAD_DIM + 6 * NSA_GROUPS * HEAD_DIM
N_GATES = 3 * NSA_HEADS
SGU_CHUNK = 128
SGU_GROUPS = 8
SGU_GROUP_DIM = D_MODEL // SGU_GROUPS
CONV_WIDTH = 3
CONV_HALO = 16

MASK_VALUE = -1e30
VMEM_LIMIT = 56 * 1024 * 1024


def _params(semantics):
    return pltpu.CompilerParams(dimension_semantics=semantics, vmem_limit_bytes=VMEM_LIMIT)


def _rms(x, g):
    return x * lax.rsqrt(jnp.mean(x * x, axis=-1, keepdims=True) + EPS) * g


def _dot(a, b):
    return jnp.dot(a, b, preferred_element_type=F32)


def _dot_nt(a, b):
    return lax.dot_general(a, b, (((1,), (1,)), ((), ())), preferred_element_type=F32)


def _norm_row_spec(layer, k, n_grid_axes):
    idx = layer * N_NORMS + k
    if n_grid_axes == 1:
        return pl.BlockSpec((None, 1, D_MODEL), lambda i: (idx, 0, 0))
    return pl.BlockSpec((None, 1, D_MODEL), lambda i, j: (idx, 0, 0))


def _ffn_body(h_ref, gpre_ref, gpost_ref, wg_ref, wu_ref, wd_ref, o_ref, xn_ref):
    f = pl.program_id(1)

    @pl.when(f == 0)
    def _():
        xn_ref[...] = _rms(h_ref[...], gpre_ref[...]).astype(BF16)
        o_ref[...] = jnp.zeros_like(o_ref)

    x = xn_ref[...]
    gate = _dot(x, wg_ref[...].astype(BF16))
    up = _dot(x, wu_ref[...].astype(BF16))
    act = (gate * jax.nn.sigmoid(gate) * up).astype(BF16)
    o_ref[...] += _dot(act, wd_ref[...].astype(BF16))

    @pl.when(f == pl.num_programs(1) - 1)
    def _():
        o_ref[...] = h_ref[...] + 0.5 * _rms(o_ref[...], gpost_ref[...])


def _ffn(h, norm_rows, layer, k_pre, k_post, wg, wu, wd, tm, tf):
    t = h.shape[0]
    return pl.pallas_call(
        _ffn_body,
        grid=(t // tm, D_FF // tf),
        in_specs=[
            pl.BlockSpec((tm, D_MODEL), lambda i, f: (i, 0), pipeline_mode=pl.Buffered(1)),
            _norm_row_spec(layer, k_pre, 2),
            _norm_row_spec(layer, k_post, 2),
            pl.BlockSpec((None, D_MODEL, tf), lambda i, f: (layer, 0, f)),
            pl.BlockSpec((None, D_MODEL, tf), lambda i, f: (layer, 0, f)),
            pl.BlockSpec((None, tf, D_MODEL), lambda i, f: (layer, f, 0)),
        ],
        out_specs=pl.BlockSpec((tm, D_MODEL), lambda i, f: (i, 0)),
        out_shape=jax.ShapeDtypeStruct((t, D_MODEL), F32),
        scratch_shapes=[pltpu.VMEM((tm, D_MODEL), BF16)],
        compiler_params=_params(("parallel", "arbitrary")),
        name="ffn",
    )(h, norm_rows, norm_rows, wg, wu, wd)


def _outproj_body(a_ref, w_ref, h_ref, g_ref, o_ref, *, tn):
    j = pl.program_id(1)
    col = pl.multiple_of(j * tn, tn)
    o_ref[:, pl.ds(col, tn)] = _dot(a_ref[...], w_ref[...].astype(BF16))

    @pl.when(j == pl.num_programs(1) - 1)
    def _():
        o_ref[...] = h_ref[...] + _rms(o_ref[...], g_ref[...])


def _outproj(a, w, w_layer, h, norm_rows, layer, k_norm, tm, tn):
    t, kdim = a.shape
    return pl.pallas_call(
        functools.partial(_outproj_body, tn=tn),
        grid=(t // tm, D_MODEL // tn),
        in_specs=[
            pl.BlockSpec((tm, kdim), lambda i, j: (i, 0)),
            pl.BlockSpec((None, kdim, tn), lambda i, j: (w_layer, 0, j)),
            pl.BlockSpec((tm, D_MODEL), lambda i, j: (i, 0)),
            _norm_row_spec(layer, k_norm, 2),
        ],
        out_specs=pl.BlockSpec((tm, D_MODEL), lambda i, j: (i, 0)),
        out_shape=jax.ShapeDtypeStruct((t, D_MODEL), F32),
        compiler_params=_params(("parallel", "arbitrary")),
        name="outproj",
    )(a, w, h, norm_rows)


def _ple_body(h_ref, p_ref, gpre_ref, gpost_ref, wg_ref, wp_ref, o_ref, xn_ref, pb_ref, *, tn):
    j = pl.program_id(1)

    @pl.when(j == 0)
    def _():
        xn_ref[...] = _rms(h_ref[...], gpre_ref[...]).astype(BF16)
        pb_ref[...] = p_ref[...].astype(BF16)

    gate = jax.nn.sigmoid(_dot(xn_ref[...], wg_ref[...].astype(BF16)))
    emb = _dot(pb_ref[...], wp_ref[...].astype(BF16))
    col = pl.multiple_of(j * tn, tn)
    o_ref[:, pl.ds(col, tn)] = gate * emb

    @pl.when(j == pl.num_programs(1) - 1)
    def _():
        o_ref[...] = h_ref[...] + _rms(o_ref[...], gpost_ref[...])


def _ple(h, p, norm_rows, layer, wg, wp, tm, tn):
    t = h.shape[0]
    return pl.pallas_call(
        functools.partial(_ple_body, tn=tn),
        grid=(t // tm, D_MODEL // tn),
        in_specs=[
            pl.BlockSpec((tm, D_MODEL), lambda i, j: (i, 0)),
            pl.BlockSpec((None, tm, PLE_DIM), lambda i, j: (layer, i, 0)),
            _norm_row_spec(layer, 6, 2),
            _norm_row_spec(layer, 7, 2),
            pl.BlockSpec((None, D_MODEL, tn), lambda i, j: (layer, 0, j)),
            pl.BlockSpec((None, PLE_DIM, tn), lambda i, j: (layer, 0, j)),
        ],
        out_specs=pl.BlockSpec((tm, D_MODEL), lambda i, j: (i, 0)),
        out_shape=jax.ShapeDtypeStruct((t, D_MODEL), F32),
        scratch_shapes=[pltpu.VMEM((tm, D_MODEL), BF16), pltpu.VMEM((tm, PLE_DIM), BF16)],
        compiler_params=_params(("parallel", "arbitrary")),
        name="ple",
    )(h, p, norm_rows, norm_rows, wg, wp)


def _nsa_inproj_body(h_ref, g_ref, w_ref, wgate_ref, qkv_ref, gates_ref, xn_ref, *, tn):
    j = pl.program_id(1)

    @pl.when(j == 0)
    def _():
        xn = _rms(h_ref[...], g_ref[...]).astype(BF16)
        xn_ref[...] = xn
        gates_ref[...] = jax.nn.sigmoid(_dot(xn, wgate_ref[...].astype(BF16)))

    res = _dot(xn_ref[...], w_ref[...].astype(BF16))
    for c in range(tn // LANES):
        qkv_ref[c] = res[:, c * LANES:(c + 1) * LANES].astype(BF16)


def _nsa_inproj(h, norm_rows, layer, w_in, w_layer, w_gate, tm, tn):
    t = h.shape[0]
    return pl.pallas_call(
        functools.partial(_nsa_inproj_body, tn=tn),
        grid=(t // tm, QKV_WIDTH // tn),
        in_specs=[
            pl.BlockSpec((tm, D_MODEL), lambda i, j: (i, 0)),
            _norm_row_spec(layer, 2, 2),
            pl.BlockSpec((None, D_MODEL, tn), lambda i, j: (w_layer, 0, j)),
            pl.BlockSpec((D_MODEL, LANES), lambda i, j: (0, 0)),
        ],
        out_specs=[
            pl.BlockSpec((tn // LANES, tm, LANES), lambda i, j: (j, i, 0)),
            pl.BlockSpec((tm, LANES), lambda i, j: (i, 0)),
        ],
        out_shape=[
            jax.ShapeDtypeStruct((QKV_WIDTH // LANES, t, LANES), BF16),
            jax.ShapeDtypeStruct((t, LANES), F32),
        ],
        scratch_shapes=[pltpu.VMEM((tm, D_MODEL), BF16)],
        compiler_params=_params(("parallel", "arbitrary")),
        name="nsa_inproj",
    )(h, norm_rows, w_in, w_gate)


def _rope_body(x_ref, cos_ref, sin_ref, o_ref):
    x = x_ref[...].astype(F32)
    lane = lax.broadcasted_iota(jnp.int32, x.shape, 1)
    half = ROPE_DIM // 2
    partner = jnp.where(lane < half, pltpu.roll(x, LANES - half, axis=1), pltpu.roll(x, half, axis=1))
    o_ref[...] = (x * cos_ref[...] + partner * sin_ref[...]).astype(BF16)


def _rope_src_slab(s):
    kq = NSA_HEADS
    return jnp.where(s < kq, s, jnp.where(s < kq + NSA_GROUPS, s + 8, s + 12))


def _rope(qkv, cos_t, sin_t, seq, tm):
    t = qkv.shape[1]
    n_slabs = NSA_HEADS + 2 * NSA_GROUPS
    per_seq = seq // tm
    return pl.pallas_call(
        _rope_body,
        grid=(n_slabs, t // tm),
        in_specs=[
            pl.BlockSpec((None, tm, LANES), lambda s, i: (_rope_src_slab(s), i, 0)),
            pl.BlockSpec((tm, LANES), lambda s, i: (i % per_seq, 0)),
            pl.BlockSpec((tm, LANES), lambda s, i: (i % per_seq, 0)),
        ],
        out_specs=pl.BlockSpec((None, tm, LANES), lambda s, i: (s, i, 0)),
        out_shape=jax.ShapeDtypeStruct((n_slabs, t, LANES), BF16),
        compiler_params=_params(("parallel", "parallel")),
        name="rope",
    )(qkv, cos_t, sin_t)


def _rope_tables(seq):
    half = ROPE_DIM // 2
    inv_freq = jnp.power(jnp.float32(ROPE_THETA), -jnp.arange(0, ROPE_DIM, 2, dtype=F32) / ROPE_DIM)
    ang = jnp.arange(seq, dtype=F32)[:, None] * inv_freq[None, :]
    cos, sin = jnp.cos(ang), jnp.sin(ang)
    cos_t = jnp.concatenate([cos, cos, jnp.ones((seq, LANES - ROPE_DIM), F32)], axis=1)
    sin_t = jnp.concatenate([-sin, sin, jnp.zeros((seq, LANES - ROPE_DIM), F32)], axis=1)
    del half
    return cos_t, sin_t


def _compress_body(x_ref, pe_ref, w1_ref, b1_ref, w2_ref, o_ref):
    half = CMP_STRIDE * HEAD_DIM
    x = x_ref[...]
    w1 = w1_ref[...].astype(BF16)
    first = _dot(x, w1[:half])
    second = _dot(x, w1[half:])
    n_chunks = x.shape[0]
    second = pltpu.roll(second, n_chunks - 1, axis=0)
    pe = jnp.broadcast_to(pe_ref[...], (8, CMP_BLOCK * HEAD_DIM)).astype(BF16)
    const = _dot(pe, w1)[0:1]
    hid = first + second + const + b1_ref[...]
    act = (hid * jax.nn.sigmoid(hid)).astype(BF16)
    o_ref[...] = _dot(act, w2_ref[...].astype(BF16)).astype(BF16)


def _compress(x, pe, w1, b1, w2, layer_j, batch):
    n_chunks = x.shape[2] // batch
    return pl.pallas_call(
        _compress_body,
        grid=(2, NSA_GROUPS, batch),
        in_specs=[
            pl.BlockSpec((None, None, n_chunks, CMP_STRIDE * HEAD_DIM), lambda a, g, b: (a, g, b, 0)),
            pl.BlockSpec((None, None, 1, CMP_BLOCK * HEAD_DIM), lambda a, g, b: (layer_j, a, 0, 0)),
            pl.BlockSpec((None, None, CMP_BLOCK * HEAD_DIM, CMP_HIDDEN), lambda a, g, b: (layer_j, a, 0, 0)),
            pl.BlockSpec((None, None, 1, CMP_HIDDEN), lambda a, g, b: (layer_j, a, 0, 0)),
            pl.BlockSpec((None, None, CMP_HIDDEN, HEAD_DIM), lambda a, g, b: (layer_j, a, 0, 0)),
        ],
        out_specs=pl.BlockSpec((None, None, None, n_chunks, HEAD_DIM), lambda a, g, b: (a, g, b, 0, 0)),
        out_shape=jax.ShapeDtypeStruct((2, NSA_GROUPS, batch, n_chunks, HEAD_DIM), BF16),
        compiler_params=_params(("parallel", "parallel", "parallel")),
        name="compress",
    )(x, pe, w1, b1, w2)


def _gate_column(gates, col):
    lane = lax.broadcasted_iota(jnp.int32, gates.shape, 1)
    return jnp.sum(jnp.where(lane == col, gates, 0.0), axis=-1, keepdims=True)


def _split3(x):
    hi = x.astype(BF16)
    r1 = x - hi.astype(F32)
    mid = r1.astype(BF16)
    lo = (r1 - mid.astype(F32)).astype(BF16)
    return hi, mid, lo


def _cmp_attn_body(q_ref, kc_ref, vc_ref, gates_ref, o_ref, sel_ref, *, tq, n_cmp):
    g = pl.program_id(1)
    qi = pl.program_id(2)
    rows = HEADS_PER_GROUP * tq
    scale = HEAD_DIM ** -0.5
    q = q_ref[...].reshape(rows, HEAD_DIM)
    s = _dot_nt(q, kc_ref[...]) * scale

    tpos = qi * tq + lax.broadcasted_iota(jnp.int32, (tq, n_cmp), 0)
    cend = lax.broadcasted_iota(jnp.int32, (tq, n_cmp), 1) * CMP_STRIDE + (CMP_BLOCK - 1)
    bias = jnp.where(cend <= tpos, 0.0, MASK_VALUE)
    okf = jnp.where(cend <= tpos, 1.0, 0.0)
    bias = jnp.concatenate([bias] * HEADS_PER_GROUP, axis=0)
    okf = jnp.concatenate([okf] * HEADS_PER_GROUP, axis=0)
    s = s + bias
    m = jnp.max(s, axis=-1, keepdims=True)
    e = jnp.exp(s - m) * okf
    p = e / jnp.maximum(jnp.sum(e, axis=-1, keepdims=True), 1e-30)
    o = _dot(p.astype(BF16), vc_ref[...])

    gates = gates_ref[...]
    for r in range(HEADS_PER_GROUP):
        gcol = _gate_column(gates, g * HEADS_PER_GROUP + r)
        o_ref[:, r * HEAD_DIM:(r + 1) * HEAD_DIM] = gcol * o[r * tq:(r + 1) * tq]

    psum = p[0:tq]
    for r in range(1, HEADS_PER_GROUP):
        psum = psum + p[r * tq:(r + 1) * tq]
    crow = lax.broadcasted_iota(jnp.int32, (n_cmp, LANES), 0)
    jcol = lax.broadcasted_iota(jnp.int32, (n_cmp, LANES), 1)
    c_lo, c_hi = crow * CMP_STRIDE, crow * CMP_STRIDE + (CMP_BLOCK - 1)
    j_lo, j_hi = jcol * SEL_BLOCK, jcol * SEL_BLOCK + (SEL_BLOCK - 1)
    ov = jnp.maximum(jnp.minimum(c_hi, j_hi) - jnp.maximum(c_lo, j_lo) + 1, 0).astype(F32) / CMP_STRIDE
    ov = ov.astype(BF16)
    hi, mid, lo = _split3(psum)
    imp = _dot(hi, ov) + _dot(mid, ov) + _dot(lo, ov)

    blk = lax.broadcasted_iota(jnp.int32, (tq, LANES), 1)
    cur = (qi * tq + lax.broadcasted_iota(jnp.int32, (tq, LANES), 0)) // SEL_BLOCK
    forced = (blk == 0) | (blk == cur) | (blk == cur - 1)
    key = jnp.where(blk > cur, -1.0, imp)
    key = jnp.where(forced, 1e30, key)
    sel = jnp.zeros((tq, LANES), F32)
    for _ in range(SEL_TOP):
        mx = jnp.max(key, axis=-1, keepdims=True)
        first = jnp.min(jnp.where(key == mx, blk, LANES), axis=-1, keepdims=True)
        pick = blk == first
        sel = jnp.where(pick, 1.0, sel)
        key = jnp.where(pick, -2.0, key)
    sel_ref[...] = sel.astype(BF16)


def _cmp_attn(qkv, kvc, gates, batch, seq, tq):
    t = qkv.shape[1]
    nq = seq // tq
    n_cmp = kvc.shape[3]
    return pl.pallas_call(
        functools.partial(_cmp_attn_body, tq=tq, n_cmp=n_cmp),
        grid=(batch, NSA_GROUPS, nq),
        in_specs=[
            pl.BlockSpec((HEADS_PER_GROUP, tq, HEAD_DIM), lambda b, g, i: (g, b * nq + i, 0)),
            pl.BlockSpec((None, None, None, n_cmp, HEAD_DIM), lambda b, g, i: (0, g, b, 0, 0)),
            pl.BlockSpec((None, None, None, n_cmp, HEAD_DIM), lambda b, g, i: (1, g, b, 0, 0)),
            pl.BlockSpec((tq, LANES), lambda b, g, i: (b * nq + i, 0)),
        ],
        out_specs=[
            pl.BlockSpec((tq, HEADS_PER_GROUP * HEAD_DIM), lambda b, g, i: (b * nq + i, g)),
            pl.BlockSpec((None, tq, LANES), lambda b, g, i: (g, b * nq + i, 0)),
        ],
        out_shape=[
            jax.ShapeDtypeStruct((t, D_MODEL), F32),
            jax.ShapeDtypeStruct((NSA_GROUPS, t, LANES), BF16),
        ],
        compiler_params=_params(("parallel", "parallel", "parallel")),
        name="cmp_attn",
    )(qkv, kvc, kvc, gates)


def _flash_body(*refs, mode, tq, tk, branch, out_dtype):
    if mode == "sel":
        q_ref, k_ref, v_ref, sel_ref, gates_ref, oin_ref, o_ref, m_ref, l_ref, acc_ref = refs
    else:
        q_ref, k_ref, v_ref, gates_ref, oin_ref, o_ref, m_ref, l_ref, acc_ref = refs
    g = pl.program_id(1)
    qi = pl.program_id(2)
    rows = HEADS_PER_GROUP * tq
    scale = HEAD_DIM ** -0.5
    t0 = qi * tq
    q = q_ref[...].reshape(rows, HEAD_DIM)
    tpos = t0 + lax.broadcasted_iota(jnp.int32, (tq, tk), 0)

    m_ref[...] = jnp.full(m_ref.shape, MASK_VALUE, F32)
    l_ref[...] = jnp.zeros(l_ref.shape, F32)
    acc_ref[...] = jnp.zeros(acc_ref.shape, F32)

    hi_tile = (t0 + tq - 1) // tk + 1
    if mode == "sel":
        lo_tile = 0
        selm = sel_ref[...]
    else:
        lo_tile = jnp.maximum(t0 - (WINDOW - 1), 0) // tk

    def body(ki, carry):
        k0 = pl.multiple_of(ki * tk, tk)
        k = k_ref[pl.ds(k0, tk), :]
        v = v_ref[pl.ds(k0, tk), :]
        s = _dot_nt(q, k) * scale
        kpos = k0 + lax.broadcasted_iota(jnp.int32, (tq, tk), 1)
        if mode == "sel":
            erow = lax.broadcasted_iota(jnp.int32, (LANES, tk), 0)
            ecol = k0 + lax.broadcasted_iota(jnp.int32, (LANES, tk), 1)
            expand = jnp.where(ecol // SEL_BLOCK == erow, 1.0, 0.0).astype(BF16)
            chosen = _dot(selm, expand)
            valid = (chosen > 0.5) & (kpos <= tpos)
        else:
            valid = (kpos <= tpos) & (kpos > tpos - WINDOW)
        bias = jnp.where(valid, 0.0, MASK_VALUE)
        s = s + jnp.concatenate([bias] * HEADS_PER_GROUP, axis=0)
        m_prev = m_ref[...]
        m_new = jnp.maximum(m_prev, jnp.max(s, axis=-1, keepdims=True))
        alpha = jnp.exp(m_prev - m_new)
        p = jnp.exp(s - m_new)
        l_ref[...] = alpha * l_ref[...] + jnp.sum(p, axis=-1, keepdims=True)
        acc_ref[...] = alpha * acc_ref[...] + _dot(p.astype(BF16), v)
        m_ref[...] = m_new
        return carry

    lax.fori_loop(lo_tile, hi_tile, body, 0)

    o = acc_ref[...] / l_ref[...]
    gates = gates_ref[...]
    for r in range(HEADS_PER_GROUP):
        gcol = _gate_column(gates, branch * NSA_HEADS + g * HEADS_PER_GROUP + r)
        cols = slice(r * HEAD_DIM, (r + 1) * HEAD_DIM)
        o_ref[:, cols] = (oin_ref[:, cols] + gcol * o[r * tq:(r + 1) * tq]).astype(out_dtype)


def _flash(mode, q_arr, k_arr, k_slab0, v_arr, v_slab0, selm, gates, oin, batch, seq, tq, tk, branch,
           out_dtype):
    t = q_arr.shape[1]
    nq = seq // tq
    rows = HEADS_PER_GROUP * tq
    in_specs = [
        pl.BlockSpec((HEADS_PER_GROUP, tq, HEAD_DIM), lambda b, g, i: (g, b * nq + i, 0)),
        pl.BlockSpec((None, seq, HEAD_DIM), lambda b, g, i: (k_slab0 + g, b, 0)),
        pl.BlockSpec((None, seq, HEAD_DIM), lambda b, g, i: (v_slab0 + g, b, 0)),
    ]
    args = [q_arr, k_arr, v_arr]
    if mode == "sel":
        in_specs.append(pl.BlockSpec((None, tq, LANES), lambda b, g, i: (g, b * nq + i, 0)))
        args.append(selm)
    in_specs += [
        pl.BlockSpec((tq, LANES), lambda b, g, i: (b * nq + i, 0)),
        pl.BlockSpec((tq, HEADS_PER_GROUP * HEAD_DIM), lambda b, g, i: (b * nq + i, g)),
    ]
    args += [gates, oin]
    return pl.pallas_call(
        functools.partial(_flash_body, mode=mode, tq=tq, tk=tk, branch=branch, out_dtype=out_dtype),
        grid=(batch, NSA_GROUPS, nq),
        in_specs=in_specs,
        out_specs=pl.BlockSpec((tq, HEADS_PER_GROUP * HEAD_DIM), lambda b, g, i: (b * nq + i, g)),
        out_shape=jax.ShapeDtypeStruct((t, D_MODEL), out_dtype),
        scratch_shapes=[
            pltpu.VMEM((rows, 1), F32),
            pltpu.VMEM((rows, 1), F32),
            pltpu.VMEM((rows, HEAD_DIM), F32),
        ],
        compiler_params=_params(("parallel", "parallel", "parallel")),
        name="attn_" + mode,
    )(*args)


def _nsa_mixer(h, norm_rows, layer, layer_j, batch, seq, w_in, w_out, phi_pe, phi_w1, phi_b1, phi_w2):
    t = h.shape[0]
    w_gate = jnp.pad(w_in[layer_j, :, QKV_WIDTH:], ((0, 0), (0, LANES - N_GATES)))
    qkv, gates = _nsa_inproj(h, norm_rows, layer, w_in, layer_j, w_gate, tm=1024, tn=512)

    kq = NSA_HEADS
    x_cmp = qkv[kq:kq + 2 * NSA_GROUPS].reshape(2, NSA_GROUPS, t // CMP_STRIDE, CMP_STRIDE * HEAD_DIM)
    pe = phi_pe.reshape(phi_pe.shape[0], 2, 1, CMP_BLOCK * HEAD_DIM)
    b1 = phi_b1.reshape(phi_b1.shape[0], 2, 1, CMP_HIDDEN)
    kvc = _compress(x_cmp, pe, phi_w1, b1, phi_w2, layer_j, batch)

    tq = min(256, seq)
    o1, selm = _cmp_attn(qkv, kvc, gates, batch, seq, tq)

    cos_t, sin_t = _rope_tables(seq)
    roped = _rope(qkv, cos_t, sin_t, seq, tm=min(1024, seq))

    tk = min(256, seq)
    o2 = _flash("sel", roped, roped, kq, qkv, kq + 3 * NSA_GROUPS, selm, gates, o1, batch, seq, tq, tk,
                branch=1, out_dtype=F32)
    o3 = _flash("win", roped, roped, kq + NSA_GROUPS, qkv, kq + 5 * NSA_GROUPS, None, gates, o2, batch, seq,
                tq, tk, branch=2, out_dtype=BF16)
    return _outproj(o3, w_out, layer_j, h, norm_rows, layer, 3, tm=512, tn=512)


def _conv_inproj_body(h_ref, halo_ref, g_ref, wb_ref, wc_ref, wu_ref, cw_ref, o_ref, xn_ref, *, tiles_per_seq):
    i = pl.program_id(0)
    j = pl.program_id(1)

    @pl.when(j == 0)
    def _():
        keep = jnp.where(i % tiles_per_seq == 0, 0.0, 1.0)
        xn_ref[0:CONV_HALO] = (_rms(halo_ref[...], g_ref[...]) * keep).astype(BF16)
        xn_ref[CONV_HALO:] = _rms(h_ref[...], g_ref[...]).astype(BF16)

    x = xn_ref[...]
    bg = _dot(x, wb_ref[...].astype(BF16))
    z = _dot(x, wc_ref[...].astype(BF16)) * _dot(x, wu_ref[...].astype(BF16))
    cw = cw_ref[...]
    conv = cw[2:3] * z + cw[1:2] * pltpu.roll(z, 1, axis=0) + cw[0:1] * pltpu.roll(z, 2, axis=0)
    o_ref[...] = (bg * conv)[CONV_HALO:].astype(BF16)


def _conv_inproj(h, norm_rows, layer, w_in, conv_w, layer_j, seq, tm, tn):
    t = h.shape[0]
    nj = D_MODEL // tn
    halo_blocks = tm // CONV_HALO
    return pl.pallas_call(
        functools.partial(_conv_inproj_body, tiles_per_seq=seq // tm),
        grid=(t // tm, nj),
        in_specs=[
            pl.BlockSpec((tm, D_MODEL), lambda i, j: (i, 0)),
            pl.BlockSpec((CONV_HALO, D_MODEL), lambda i, j: (jnp.maximum(i * halo_blocks - 1, 0), 0)),
            _norm_row_spec(layer, 2, 2),
            pl.BlockSpec((None, D_MODEL, tn), lambda i, j: (layer_j, 0, j)),
            pl.BlockSpec((None, D_MODEL, tn), lambda i, j: (layer_j, 0, nj + j)),
            pl.BlockSpec((None, D_MODEL, tn), lambda i, j: (layer_j, 0, 2 * nj + j)),
            pl.BlockSpec((None, CONV_WIDTH, tn), lambda i, j: (layer_j, 0, j)),
        ],
        out_specs=pl.BlockSpec((tm, tn), lambda i, j: (i, j)),
        out_shape=jax.ShapeDtypeStruct((t, D_MODEL), BF16),
        scratch_shapes=[pltpu.VMEM((tm + CONV_HALO, D_MODEL), BF16)],
        compiler_params=_params(("parallel", "arbitrary")),
        name="conv_inproj",
    )(h, h, norm_rows, w_in, w_in, w_in, conv_w)


def _sgu_inproj_body(h_ref, g_ref, w_ref, lng_ref, lnb_ref, ws_ref, bs_ref, o_ref, xn_ref, z_ref, *, tm, tn):
    j = pl.program_id(1)

    @pl.when(j == 0)
    def _():
        xn_ref[...] = _rms(h_ref[...], g_ref[...]).astype(BF16)

    col = pl.multiple_of(j * tn, tn)
    z_ref[:, pl.ds(col, tn)] = jax.nn.gelu(_dot(xn_ref[...], w_ref[...].astype(BF16)))

    @pl.when(j == pl.num_programs(1) - 1)
    def _():
        v = z_ref[:, D_MODEL:]
        mu = jnp.mean(v, axis=-1, keepdims=True)
        var = jnp.mean(jnp.square(v - mu), axis=-1, keepdims=True)
        vn = ((v - mu) * lax.rsqrt(var + EPS) * lng_ref[...] + lnb_ref[...]).astype(BF16)
        row = lax.broadcasted_iota(jnp.int32, (SGU_CHUNK, SGU_CHUNK), 0)
        colm = lax.broadcasted_iota(jnp.int32, (SGU_CHUNK, SGU_CHUNK), 1)
        bs = bs_ref[...]
        for grp in range(SGU_GROUPS):
            ws = jnp.where(colm <= row, ws_ref[grp], 0.0).astype(BF16)
            cols = slice(grp * SGU_GROUP_DIM, (grp + 1) * SGU_GROUP_DIM)
            for c in range(tm // SGU_CHUNK):
                rws = slice(c * SGU_CHUNK, (c + 1) * SGU_CHUNK)
                sv = _dot(ws, vn[rws, cols]) + bs[:, grp:grp + 1]
                o_ref[rws, cols] = (z_ref[rws, cols] * sv).astype(BF16)


def _sgu_inproj(h, norm_rows, layer, w_in, ln_g, ln_b, w_s, b_s_t, layer_j, tm, tn):
    t = h.shape[0]
    return pl.pallas_call(
        functools.partial(_sgu_inproj_body, tm=tm, tn=tn),
        grid=(t // tm, 2 * D_MODEL // tn),
        in_specs=[
            pl.BlockSpec((tm, D_MODEL), lambda i, j: (i, 0)),
            _norm_row_spec(layer, 2, 2),
            pl.BlockSpec((None, D_MODEL, tn), lambda i, j: (layer_j, 0, j)),
            pl.BlockSpec((None, 1, D_MODEL), lambda i, j: (layer_j, 0, 0)),
            pl.BlockSpec((None, 1, D_MODEL), lambda i, j: (layer_j, 0, 0)),
            pl.BlockSpec((None, SGU_GROUPS, SGU_CHUNK, SGU_CHUNK), lambda i, j: (layer_j, 0, 0, 0)),
            pl.BlockSpec((None, SGU_CHUNK, SGU_GROUPS), lambda i, j: (layer_j, 0, 0)),
        ],
        out_specs=pl.BlockSpec((tm, D_MODEL), lambda i, j: (i, 0)),
        out_shape=jax.ShapeDtypeStruct((t, D_MODEL), BF16),
        scratch_shapes=[pltpu.VMEM((tm, D_MODEL), BF16), pltpu.VMEM((tm, 2 * D_MODEL), F32)],
        compiler_params=_params(("parallel", "arbitrary")),
        name="sgu_inproj",
    )(h, norm_rows, w_in, ln_g, ln_b, w_s, b_s_t)


def kernel(x, p, norm_g, ffn1_wg, ffn1_wu, ffn1_wd, ffn2_wg, ffn2_wu, ffn2_wd, ple_wg, ple_wp, nsa_w_in, nsa_w_out, nsa_phi_pe, nsa_phi_w1, nsa_phi_b1, nsa_phi_w2, conv_w_in, conv_w, conv_w_out, sgu_w_in, sgu_ln_g, sgu_ln_b, sgu_w_s, sgu_b_s, sgu_w_out):
    batch, seq, d = x.shape
    depth = p.shape[0]
    t = batch * seq
    assert d == D_MODEL and seq % 256 == 0 and seq >= WINDOW
    h = x.reshape(t, d)
    p2 = p.reshape(depth, t, PLE_DIM)
    norm_rows = norm_g.reshape(depth * N_NORMS, 1, d)
    sgu_ln_g3 = sgu_ln_g.reshape(-1, 1, d)
    sgu_ln_b3 = sgu_ln_b.reshape(-1, 1, d)
    sgu_b_s_t = jnp.swapaxes(sgu_b_s, 1, 2)
    tm_big = min(1024, t)

    for layer in range(depth):
        layer_j = layer // N_MIXERS
        h = _ffn(h, norm_rows, layer, 0, 1, ffn1_wg, ffn1_wu, ffn1_wd, tm=tm_big, tf=256)
        kind = layer % N_MIXERS
        if kind == 0:
            h = _nsa_mixer(h, norm_rows, layer, layer_j, batch, seq, nsa_w_in, nsa_w_out, nsa_phi_pe,
                           nsa_phi_w1, nsa_phi_b1, nsa_phi_w2)
        elif kind == 1:
            a = _conv_inproj(h, norm_rows, layer, conv_w_in, conv_w, layer_j, seq, tm=min(512, seq), tn=512)
            h = _outproj(a, conv_w_out, layer_j, h, norm_rows, layer, 3, tm=512, tn=512)
        else:
            a = _sgu_inproj(h, norm_rows, layer, sgu_w_in, sgu_ln_g3, sgu_ln_b3, sgu_w_s, sgu_b_s_t, layer_j,
                            tm=512, tn=512)
            h = _outproj(a, sgu_w_out, layer_j, h, norm_rows, layer, 3, tm=512, tn=512)
        h = _ffn(h, norm_rows, layer, 4, 5, ffn2_wg, ffn2_wu, ffn2_wd, tm=tm_big, tf=256)
        h = _ple(h, p2, norm_rows, layer, ple_wg, ple_wp, tm=512, tn=512)
    return h.reshape(batch, seq, d)
```

```python
import functools
import math

import jax
import jax.numpy as jnp
from jax import lax
from jax.experimental import pallas as pl
from jax.experimental.pallas import tpu as pltpu

F32 = jnp.float32
BF16 = jnp.bfloat16

EPS = 1e-6
LANES = 128
D_MODEL = 2048
D_FF = 5632
PLE_DIM = 256
N_NORMS = 8
N_MIXERS = 3
NSA_HEADS = 16
NSA_GROUPS = 4
HEADS_PER_GROUP = NSA_HEADS // NSA_GROUPS
HEAD_DIM = D_MODEL // NSA_HEADS
GROUP_WIDTH = HEADS_PER_GROUP * HEAD_DIM
KV_WIDTH = NSA_GROUPS * HEAD_DIM
ROPE_DIM = HEAD_DIM // 4
ROPE_HALF = ROPE_DIM // 2
ROPE_THETA = 500000.0
CMP_BLOCK = 32
CMP_STRIDE = 16
CMP_HIDDEN = 256
SEL_BLOCK = 64
SEL_TOP = 16
WINDOW = 512
QKV_WIDTH = NSA_HEADS * HEAD_DIM + 6 * KV_WIDTH
N_GATES = 3 * NSA_HEADS
SGU_CHUNK = 128
SGU_GROUPS = 8
SGU_GROUP_DIM = D_MODEL // SGU_GROUPS
CONV_WIDTH = 3
CONV_HALO = 16

MASK_VALUE = -1e30
SCORE_SCALE = HEAD_DIM ** -0.5
EXP2_SCALE = SCORE_SCALE * math.log2(math.e)
VMEM_LIMIT = 56 * 1024 * 1024


def _params(semantics):
    return pltpu.CompilerParams(dimension_semantics=semantics, vmem_limit_bytes=VMEM_LIMIT)


def _rms(x, g):
    return x * lax.rsqrt(jnp.mean(x * x, axis=-1, keepdims=True) + EPS) * g


def _dot(a, b):
    return jnp.dot(a, b, preferred_element_type=F32)


def _dot_nt(a, b):
    return lax.dot_general(a, b, (((1,), (1,)), ((), ())), preferred_element_type=F32)


def _norm_row_spec(layer, k):
    idx = layer * N_NORMS + k
    return pl.BlockSpec((None, 1, D_MODEL), lambda i, j: (idx, 0, 0))


def _ffn_body(h_ref, gpre_ref, gpost_ref, wg_ref, wu_ref, wd_ref, o_ref, xn_ref):
    f = pl.program_id(1)

    @pl.when(f == 0)
    def _():
        xn_ref[...] = _rms(h_ref[...], gpre_ref[...]).astype(BF16)
        o_ref[...] = jnp.zeros_like(o_ref)

    x = xn_ref[...]
    gate = _dot(x, wg_ref[...].astype(BF16))
    up = _dot(x, wu_ref[...].astype(BF16))
    act = (gate * jax.nn.sigmoid(gate) * up).astype(BF16)
    o_ref[...] += _dot(act, wd_ref[...].astype(BF16))

    @pl.when(f == pl.num_programs(1) - 1)
    def _():
        o_ref[...] = h_ref[...] + 0.5 * _rms(o_ref[...], gpost_ref[...])


def _ffn(h, norm_rows, layer, k_pre, k_post, wg, wu, wd, tm, tf):
    t = h.shape[0]
    return pl.pallas_call(
        _ffn_body,
        grid=(t // tm, D_FF // tf),
        in_specs=[
            pl.BlockSpec((tm, D_MODEL), lambda i, f: (i, 0), pipeline_mode=pl.Buffered(1)),
            _norm_row_spec(layer, k_pre),
            _norm_row_spec(layer, k_post),
            pl.BlockSpec((None, D_MODEL, tf), lambda i, f: (layer, 0, f)),
            pl.BlockSpec((None, D_MODEL, tf), lambda i, f: (layer, 0, f)),
            pl.BlockSpec((None, tf, D_MODEL), lambda i, f: (layer, f, 0)),
        ],
        out_specs=pl.BlockSpec((tm, D_MODEL), lambda i, f: (i, 0)),
        out_shape=jax.ShapeDtypeStruct((t, D_MODEL), F32),
        scratch_shapes=[pltpu.VMEM((tm, D_MODEL), BF16)],
        compiler_params=_params(("parallel", "arbitrary")),
        name="ffn",
    )(h, norm_rows, norm_rows, wg, wu, wd)


def _outproj_body(a_ref, w_ref, h_ref, g_ref, o_ref, *, tn):
    j = pl.program_id(1)
    col = pl.multiple_of(j * tn, tn)
    o_ref[:, pl.ds(col, tn)] = _dot(a_ref[...], w_ref[...].astype(BF16))

    @pl.when(j == pl.num_programs(1) - 1)
    def _():
        o_ref[...] = h_ref[...] + _rms(o_ref[...], g_ref[...])


def _outproj(a, w, w_layer, h, norm_rows, layer, k_norm, tm, tn):
    t, kdim = a.shape
    return pl.pallas_call(
        functools.partial(_outproj_body, tn=tn),
        grid=(t // tm, D_MODEL // tn),
        in_specs=[
            pl.BlockSpec((tm, kdim), lambda i, j: (i, 0)),
            pl.BlockSpec((None, kdim, tn), lambda i, j: (w_layer, 0, j)),
            pl.BlockSpec((tm, D_MODEL), lambda i, j: (i, 0)),
            _norm_row_spec(layer, k_norm),
        ],
        out_specs=pl.BlockSpec((tm, D_MODEL), lambda i, j: (i, 0)),
        out_shape=jax.ShapeDtypeStruct((t, D_MODEL), F32),
        compiler_params=_params(("parallel", "arbitrary")),
        name="outproj",
    )(a, w, h, norm_rows)


def _ple_body(h_ref, p_ref, gpre_ref, gpost_ref, wg_ref, wp_ref, o_ref, xn_ref, pb_ref, *, tn):
    j = pl.program_id(1)

    @pl.when(j == 0)
    def _():
        xn_ref[...] = _rms(h_ref[...], gpre_ref[...]).astype(BF16)
        pb_ref[...] = p_ref[...].astype(BF16)

    gate = jax.nn.sigmoid(_dot(xn_ref[...], wg_ref[...].astype(BF16)))
    emb = _dot(pb_ref[...], wp_ref[...].astype(BF16))
    col = pl.multiple_of(j * tn, tn)
    o_ref[:, pl.ds(col, tn)] = gate * emb

    @pl.when(j == pl.num_programs(1) - 1)
    def _():
        o_ref[...] = h_ref[...] + _rms(o_ref[...], gpost_ref[...])


def _ple(h, p, norm_rows, layer, wg, wp, tm, tn):
    t = h.shape[0]
    return pl.pallas_call(
        functools.partial(_ple_body, tn=tn),
        grid=(t // tm, D_MODEL // tn),
        in_specs=[
            pl.BlockSpec((tm, D_MODEL), lambda i, j: (i, 0)),
            pl.BlockSpec((None, tm, PLE_DIM), lambda i, j: (layer, i, 0)),
            _norm_row_spec(layer, 6),
            _norm_row_spec(layer, 7),
            pl.BlockSpec((None, D_MODEL, tn), lambda i, j: (layer, 0, j)),
            pl.BlockSpec((None, PLE_DIM, tn), lambda i, j: (layer, 0, j)),
        ],
        out_specs=pl.BlockSpec((tm, D_MODEL), lambda i, j: (i, 0)),
        out_shape=jax.ShapeDtypeStruct((t, D_MODEL), F32),
        scratch_shapes=[pltpu.VMEM((tm, D_MODEL), BF16), pltpu.VMEM((tm, PLE_DIM), BF16)],
        compiler_params=_params(("parallel", "arbitrary")),
        name="ple",
    )(h, p, norm_rows, norm_rows, wg, wp)


def _nsa_inproj_feat_body(h_ref, g_ref, w_ref, wg_ref, feat_ref, gates_ref, xn_ref):
    j = pl.program_id(1)

    @pl.when(j == 0)
    def _():
        xn = _rms(h_ref[...], g_ref[...]).astype(BF16)
        xn_ref[...] = xn
        gates_ref[...] = jax.nn.sigmoid(_dot_nt(wg_ref[...].astype(BF16), xn))

    feat_ref[...] = _dot_nt(w_ref[...].astype(BF16), xn_ref[...]).astype(BF16)


def _nsa_inproj_feat(h, norm_rows, layer, w_feat_t, w_gate_t, tm, tn):
    t = h.shape[0]
    n_feat = w_feat_t.shape[0]
    return pl.pallas_call(
        _nsa_inproj_feat_body,
        grid=(t // tm, n_feat // tn),
        in_specs=[
            pl.BlockSpec((tm, D_MODEL), lambda i, j: (i, 0), pipeline_mode=pl.Buffered(1)),
            _norm_row_spec(layer, 2),
            pl.BlockSpec((tn, D_MODEL), lambda i, j: (j, 0)),
            pl.BlockSpec((LANES, D_MODEL), lambda i, j: (0, 0)),
        ],
        out_specs=[
            pl.BlockSpec((tn, tm), lambda i, j: (j, i)),
            pl.BlockSpec((LANES, tm), lambda i, j: (0, i)),
            pl.BlockSpec((tm, D_MODEL), lambda i, j: (i, 0)),
        ],
        out_shape=[
            jax.ShapeDtypeStruct((n_feat, t), BF16),
            jax.ShapeDtypeStruct((LANES, t), F32),
            jax.ShapeDtypeStruct((t, D_MODEL), BF16),
        ],
        compiler_params=_params(("parallel", "arbitrary")),
        name="nsa_inproj_feat",
    )(h, norm_rows, w_feat_t, w_gate_t)


def _rope_token_major(x, cos, sin):
    lane = lax.broadcasted_iota(jnp.int32, x.shape, 1)
    partner = jnp.where(lane < ROPE_HALF, pltpu.roll(x, LANES - ROPE_HALF, axis=1),
                        pltpu.roll(x, ROPE_HALF, axis=1))
    return x * cos + partner * sin


def _nsa_inproj_tok_body(xn_ref, w_ref, cos_ref, sin_ref, o_ref):
    j = pl.program_id(1)
    res = _dot(xn_ref[...], w_ref[...].astype(BF16))
    n_slabs = o_ref.shape[0]

    @pl.when(j < 2)
    def _():
        for c in range(n_slabs):
            o_ref[c] = res[:, c * LANES:(c + 1) * LANES].astype(BF16)

    @pl.when(j >= 2)
    def _():
        cos, sin = cos_ref[...], sin_ref[...]
        for c in range(n_slabs):
            o_ref[c] = _rope_token_major(res[:, c * LANES:(c + 1) * LANES], cos, sin).astype(BF16)


def _nsa_inproj_tok(xn, w_in, w_layer, cos_tok, sin_tok, seq, tm):
    t = xn.shape[0]
    per_seq = seq // tm
    q_blocks = NSA_HEADS * HEAD_DIM // KV_WIDTH
    return pl.pallas_call(
        _nsa_inproj_tok_body,
        grid=(t // tm, 4),
        in_specs=[
            pl.BlockSpec((tm, D_MODEL), lambda i, j: (i, 0)),
            pl.BlockSpec((None, D_MODEL, KV_WIDTH), lambda i, j: (w_layer, 0, q_blocks + j + j // 3)),
            pl.BlockSpec((tm, LANES), lambda i, j: (i % per_seq, 0)),
            pl.BlockSpec((tm, LANES), lambda i, j: (i % per_seq, 0)),
        ],
        out_specs=pl.BlockSpec((NSA_GROUPS, tm, LANES), lambda i, j: (j, i, 0)),
        out_shape=jax.ShapeDtypeStruct((4 * NSA_GROUPS, t, LANES), BF16),
        compiler_params=_params(("parallel", "arbitrary")),
        name="nsa_inproj_tok",
    )(xn, w_in, cos_tok, sin_tok)


def _rope_tables(seq):
    inv_freq = jnp.power(jnp.float32(ROPE_THETA), -jnp.arange(0, ROPE_DIM, 2, dtype=F32) / ROPE_DIM)
    ang = jnp.arange(seq, dtype=F32)[:, None] * inv_freq[None, :]
    cos, sin = jnp.cos(ang), jnp.sin(ang)
    cos_tok = jnp.concatenate([cos, cos, jnp.ones((seq, LANES - ROPE_DIM), F32)], axis=1)
    sin_tok = jnp.concatenate([-sin, sin, jnp.zeros((seq, LANES - ROPE_DIM), F32)], axis=1)
    return cos_tok, sin_tok, cos.T, sin.T


def _compress_body(x_ref, pe_ref, w1_ref, b1_ref, w2_ref, o_ref, *, feature_major):
    half = CMP_STRIDE * HEAD_DIM
    x = x_ref[...]
    w1 = w1_ref[...].astype(BF16)
    first = _dot(x, w1[:half])
    second = _dot(x, w1[half:])
    n_chunks = x.shape[0]
    second = pltpu.roll(second, n_chunks - 1, axis=0)
    pe = jnp.broadcast_to(pe_ref[...], (8, CMP_BLOCK * HEAD_DIM)).astype(BF16)
    const = _dot(pe, w1)[0:1]
    hid = first + second + const + b1_ref[...]
    act = (hid * jax.nn.sigmoid(hid)).astype(BF16)
    if feature_major:
        o_ref[...] = _dot_nt(w2_ref[...].astype(BF16), act).astype(BF16)
    else:
        o_ref[...] = _dot(act, w2_ref[...].astype(BF16)).astype(BF16)


def _compress(x, which, pe, w1, b1, w2, layer_j, batch, feature_major):
    n_chunks = x.shape[2] // batch
    if feature_major:
        w2_spec = pl.BlockSpec((HEAD_DIM, CMP_HIDDEN), lambda g, b: (0, 0))
        out_block, out_dims = (None, None, HEAD_DIM, n_chunks), (NSA_GROUPS, batch, HEAD_DIM, n_chunks)
    else:
        w2_spec = pl.BlockSpec((None, None, CMP_HIDDEN, HEAD_DIM), lambda g, b: (layer_j, which, 0, 0))
        out_block, out_dims = (None, None, n_chunks, HEAD_DIM), (NSA_GROUPS, batch, n_chunks, HEAD_DIM)
    return pl.pallas_call(
        functools.partial(_compress_body, feature_major=feature_major),
        grid=(NSA_GROUPS, batch),
        in_specs=[
            pl.BlockSpec((None, None, n_chunks, CMP_STRIDE * HEAD_DIM), lambda g, b: (which, g, b, 0)),
            pl.BlockSpec((None, None, 1, CMP_BLOCK * HEAD_DIM), lambda g, b: (layer_j, which, 0, 0)),
            pl.BlockSpec((None, None, CMP_BLOCK * HEAD_DIM, CMP_HIDDEN), lambda g, b: (layer_j, which, 0, 0)),
            pl.BlockSpec((None, None, 1, CMP_HIDDEN), lambda g, b: (layer_j, which, 0, 0)),
            w2_spec,
        ],
        out_specs=pl.BlockSpec(out_block, lambda g, b: (g, b, 0, 0)),
        out_shape=jax.ShapeDtypeStruct(out_dims, BF16),
        compiler_params=_params(("parallel", "parallel")),
        name="compress",
    )(x, pe, w1, b1, w2)


def _lane_tile(x, n):
    return jnp.concatenate([x] * n, axis=1)


def _load_q_t(q_ref):
    return jnp.concatenate([q_ref[r * HEAD_DIM:(r + 1) * HEAD_DIM, :] for r in range(HEADS_PER_GROUP)], axis=1)


def _rope_feature_major(q, cos, sin):
    x1 = q[0:ROPE_HALF].astype(F32)
    x2 = q[ROPE_HALF:ROPE_DIM].astype(F32)
    r1 = (x1 * cos - x2 * sin).astype(BF16)
    r2 = (x2 * cos + x1 * sin).astype(BF16)
    return jnp.concatenate([r1, r2, q[ROPE_DIM:]], axis=0)


def _split3(x):
    hi = x.astype(BF16)
    r1 = x - hi.astype(F32)
    mid = r1.astype(BF16)
    lo = (r1 - mid.astype(F32)).astype(BF16)
    return hi, mid, lo


def _gate_row(gates_ref, branch, g, r):
    return gates_ref[pl.ds(branch * NSA_HEADS + g * HEADS_PER_GROUP + r, 1), :]


def _cmp_attn_body(q_ref, kc_ref, vc_ref, gates_ref, o_ref, sel_ref, *, tq, n_cmp):
    g = pl.program_id(1)
    qi = pl.program_id(2)
    t0 = qi * tq
    q = _load_q_t(q_ref)
    s = _dot(kc_ref[...], q)

    tpos = t0 + lax.broadcasted_iota(jnp.int32, (n_cmp, tq), 1)
    cend = lax.broadcasted_iota(jnp.int32, (n_cmp, tq), 0) * CMP_STRIDE + (CMP_BLOCK - 1)
    visible = cend <= tpos
    bias = _lane_tile(jnp.where(visible, 0.0, MASK_VALUE), HEADS_PER_GROUP)
    okf = _lane_tile(jnp.where(visible, 1.0, 0.0), HEADS_PER_GROUP)
    s = s + bias
    m = jnp.max(s, axis=0, keepdims=True)
    e = jnp.exp2((s - m) * EXP2_SCALE) * okf
    inv = 1.0 / jnp.maximum(jnp.sum(e, axis=0, keepdims=True), 1e-30)
    p = e * inv
    o = _dot(vc_ref[...], p.astype(BF16))

    for r in range(HEADS_PER_GROUP):
        o_ref[r * HEAD_DIM:(r + 1) * HEAD_DIM, :] = _gate_row(gates_ref, 0, g, r) * o[:, r * tq:(r + 1) * tq]

    psum = p[:, 0:tq]
    for r in range(1, HEADS_PER_GROUP):
        psum = psum + p[:, r * tq:(r + 1) * tq]
    n_sel_rows = LANES // 2
    jrow = lax.broadcasted_iota(jnp.int32, (n_sel_rows, n_cmp), 0)
    ccol = lax.broadcasted_iota(jnp.int32, (n_sel_rows, n_cmp), 1)
    c_lo, c_hi = ccol * CMP_STRIDE, ccol * CMP_STRIDE + (CMP_BLOCK - 1)
    j_lo, j_hi = jrow * SEL_BLOCK, jrow * SEL_BLOCK + (SEL_BLOCK - 1)
    ov = jnp.maximum(jnp.minimum(c_hi, j_hi) - jnp.maximum(c_lo, j_lo) + 1, 0).astype(F32) / CMP_STRIDE
    ov = ov.astype(BF16)
    hi, mid, lo = _split3(psum)
    imp = _dot(ov, hi) + _dot(ov, mid) + _dot(ov, lo)

    blk = lax.broadcasted_iota(jnp.int32, (n_sel_rows, tq), 0)
    cur = (t0 + lax.broadcasted_iota(jnp.int32, (n_sel_rows, tq), 1)) // SEL_BLOCK
    forced = (blk == 0) | (blk == cur) | (blk == cur - 1)
    key = jnp.where(blk > cur, -1.0, imp)
    key = jnp.where(forced, 1e30, key)
    sel = jnp.zeros((n_sel_rows, tq), F32)
    for _ in range(SEL_TOP):
        mx = jnp.max(key, axis=0, keepdims=True)
        first = jnp.min(jnp.where(key == mx, blk, LANES), axis=0, keepdims=True)
        pick = blk == first
        sel = jnp.where(pick, 1.0, sel)
        key = jnp.where(pick, -2.0, key)
    sel_ref[...] = jnp.concatenate([sel, jnp.zeros_like(sel)], axis=0).astype(BF16)


def _cmp_attn(feat_t, kc, vc_t, gates_t, batch, seq, tq):
    t = feat_t.shape[1]
    nq = seq // tq
    n_cmp = kc.shape[2]
    assert seq // SEL_BLOCK <= LANES // 2
    return pl.pallas_call(
        functools.partial(_cmp_attn_body, tq=tq, n_cmp=n_cmp),
        grid=(batch, NSA_GROUPS, nq),
        in_specs=[
            pl.BlockSpec((GROUP_WIDTH, tq), lambda b, g, i: (g, b * nq + i)),
            pl.BlockSpec((None, None, n_cmp, HEAD_DIM), lambda b, g, i: (g, b, 0, 0)),
            pl.BlockSpec((None, None, HEAD_DIM, n_cmp), lambda b, g, i: (g, b, 0, 0)),
            pl.BlockSpec((LANES, tq), lambda b, g, i: (0, b * nq + i)),
        ],
        out_specs=[
            pl.BlockSpec((GROUP_WIDTH, tq), lambda b, g, i: (g, b * nq + i)),
            pl.BlockSpec((None, LANES, tq), lambda b, g, i: (g, 0, b * nq + i)),
        ],
        out_shape=[
            jax.ShapeDtypeStruct((D_MODEL, t), F32),
            jax.ShapeDtypeStruct((NSA_GROUPS, LANES, t), BF16),
        ],
        compiler_params=_params(("parallel", "parallel", "parallel")),
        name="cmp_attn",
    )(feat_t, kc, vc_t, gates_t)


def _sel_attn_body(q_ref, k_ref, v_ref, sel_ref, cos_ref, sin_ref, gates_ref, oin_ref, o_ref,
                   m_ref, l_ref, acc_ref, *, tq, tk):
    g = pl.program_id(1)
    qi = pl.program_id(2)
    t0 = qi * tq
    q = _rope_feature_major(_load_q_t(q_ref), _lane_tile(cos_ref[...], HEADS_PER_GROUP),
                            _lane_tile(sin_ref[...], HEADS_PER_GROUP))
    selm = sel_ref[...]
    tpos = t0 + lax.broadcasted_iota(jnp.int32, (tk, tq), 1)

    m_ref[...] = jnp.full(m_ref.shape, MASK_VALUE, F32)
    l_ref[...] = jnp.zeros(l_ref.shape, F32)
    acc_ref[...] = jnp.zeros(acc_ref.shape, F32)

    def body(ki, carry):
        k0 = pl.multiple_of(ki * tk, tk)
        s = _dot(k_ref[pl.ds(k0, tk), :], q)
        kpos = k0 + lax.broadcasted_iota(jnp.int32, (tk, tq), 0)
        erow = k0 + lax.broadcasted_iota(jnp.int32, (tk, LANES), 0)
        ecol = lax.broadcasted_iota(jnp.int32, (tk, LANES), 1)
        expand = jnp.where(erow // SEL_BLOCK == ecol, 1.0, 0.0).astype(BF16)
        chosen = _dot(expand, selm)
        valid = (chosen > 0.5) & (kpos <= tpos)
        s = s + _lane_tile(jnp.where(valid, 0.0, MASK_VALUE), HEADS_PER_GROUP)
        m_prev = m_ref[...]
        m_new = jnp.maximum(m_prev, jnp.max(s, axis=0, keepdims=True))
        alpha = jnp.exp2((m_prev - m_new) * EXP2_SCALE)
        p = jnp.exp2((s - m_new) * EXP2_SCALE)
        l_ref[...] = alpha * l_ref[...] + jnp.sum(p, axis=0, keepdims=True)
        acc_ref[...] = alpha * acc_ref[...] + _dot(v_ref[:, pl.ds(k0, tk)], p.astype(BF16))
        m_ref[...] = m_new
        return carry

    lax.fori_loop(0, (t0 + tq - 1) // tk + 1, body, 0)

    o = acc_ref[...] * (1.0 / l_ref[...])
    for r in range(HEADS_PER_GROUP):
        rows = slice(r * HEAD_DIM, (r + 1) * HEAD_DIM)
        o_ref[rows, :] = oin_ref[rows, :] + _gate_row(gates_ref, 1, g, r) * o[:, r * tq:(r + 1) * tq]


def _sel_attn(feat_t, tok, selm, cos_t, sin_t, gates_t, oin, batch, seq, tq, tk):
    t = feat_t.shape[1]
    nq = seq // tq
    lanes = HEADS_PER_GROUP * tq
    v_row0 = NSA_HEADS
    return pl.pallas_call(
        functools.partial(_sel_attn_body, tq=tq, tk=tk),
        grid=(batch, NSA_GROUPS, nq),
        in_specs=[
            pl.BlockSpec((GROUP_WIDTH, tq), lambda b, g, i: (g, b * nq + i)),
            pl.BlockSpec((None, seq, HEAD_DIM), lambda b, g, i: (2 * NSA_GROUPS + g, b, 0)),
            pl.BlockSpec((HEAD_DIM, seq), lambda b, g, i: (v_row0 + g, b)),
            pl.BlockSpec((None, LANES, tq), lambda b, g, i: (g, 0, b * nq + i)),
            pl.BlockSpec((ROPE_HALF, tq), lambda b, g, i: (0, i)),
            pl.BlockSpec((ROPE_HALF, tq), lambda b, g, i: (0, i)),
            pl.BlockSpec((LANES, tq), lambda b, g, i: (0, b * nq + i)),
            pl.BlockSpec((GROUP_WIDTH, tq), lambda b, g, i: (g, b * nq + i)),
        ],
        out_specs=pl.BlockSpec((GROUP_WIDTH, tq), lambda b, g, i: (g, b * nq + i)),
        out_shape=jax.ShapeDtypeStruct((D_MODEL, t), F32),
        scratch_shapes=[
            pltpu.VMEM((1, lanes), F32),
            pltpu.VMEM((1, lanes), F32),
            pltpu.VMEM((HEAD_DIM, lanes), F32),
        ],
        compiler_params=_params(("parallel", "parallel", "parallel")),
        name="attn_sel",
    )(feat_t, tok, feat_t, selm, cos_t, sin_t, gates_t, oin)


def _win_attn_body(q_ref, k_ref, v_ref, cos_ref, sin_ref, gates_ref, oin_ref, o_ref, *, tq):
    g = pl.program_id(1)
    qi = pl.program_id(2)
    t0 = qi * tq
    band = WINDOW + tq
    q = _rope_feature_major(_load_q_t(q_ref), _lane_tile(cos_ref[...], HEADS_PER_GROUP),
                            _lane_tile(sin_ref[...], HEADS_PER_GROUP))
    start = pl.multiple_of(jnp.maximum(t0 - WINDOW, 0), tq)
    s = _dot(k_ref[pl.ds(start, band), :], q)
    kpos = start + lax.broadcasted_iota(jnp.int32, (band, tq), 0)
    tpos = t0 + lax.broadcasted_iota(jnp.int32, (band, tq), 1)
    valid = (kpos <= tpos) & (kpos > tpos - WINDOW)
    s = s + _lane_tile(jnp.where(valid, 0.0, MASK_VALUE), HEADS_PER_GROUP)
    m = jnp.max(s, axis=0, keepdims=True)
    p = jnp.exp2((s - m) * EXP2_SCALE)
    inv = 1.0 / jnp.sum(p, axis=0, keepdims=True)
    o = _dot(v_ref[:, pl.ds(start, band)], p.astype(BF16)) * inv
    for r in range(HEADS_PER_GROUP):
        rows = slice(r * HEAD_DIM, (r + 1) * HEAD_DIM)
        total = oin_ref[rows, :] + _gate_row(gates_ref, 2, g, r) * o[:, r * tq:(r + 1) * tq]
        o_ref[:, rows] = total.T.astype(BF16)


def _win_attn(feat_t, tok, cos_t, sin_t, gates_t, oin, batch, seq, tq):
    t = feat_t.shape[1]
    nq = seq // tq
    v_row0 = NSA_HEADS + NSA_GROUPS
    assert seq >= WINDOW + tq
    return pl.pallas_call(
        functools.partial(_win_attn_body, tq=tq),
        grid=(batch, NSA_GROUPS, nq),
        in_specs=[
            pl.BlockSpec((GROUP_WIDTH, tq), lambda b, g, i: (g, b * nq + i)),
            pl.BlockSpec((None, seq, HEAD_DIM), lambda b, g, i: (3 * NSA_GROUPS + g, b, 0)),
            pl.BlockSpec((HEAD_DIM, seq), lambda b, g, i: (v_row0 + g, b)),
            pl.BlockSpec((ROPE_HALF, tq), lambda b, g, i: (0, i)),
            pl.BlockSpec((ROPE_HALF, tq), lambda b, g, i: (0, i)),
            pl.BlockSpec((LANES, tq), lambda b, g, i: (0, b * nq + i)),
            pl.BlockSpec((GROUP_WIDTH, tq), lambda b, g, i: (g, b * nq + i)),
        ],
        out_specs=pl.BlockSpec((tq, GROUP_WIDTH), lambda b, g, i: (b * nq + i, g)),
        out_shape=jax.ShapeDtypeStruct((t, D_MODEL), BF16),
        compiler_params=_params(("parallel", "parallel", "parallel")),
        name="attn_win",
    )(feat_t, tok, feat_t, cos_t, sin_t, gates_t, oin)


def _nsa_mixer(h, norm_rows, layer, layer_j, batch, seq, w_in, w_out, phi_pe, phi_w1, phi_b1, phi_w2):
    t = h.shape[0]
    w = w_in[layer_j]
    q_end = NSA_HEADS * HEAD_DIM
    vs0, vw0 = q_end + 3 * KV_WIDTH, q_end + 5 * KV_WIDTH
    w_feat_t = jnp.concatenate([w[:, :q_end], w[:, vs0:vs0 + KV_WIDTH], w[:, vw0:vw0 + KV_WIDTH]], axis=1).T
    w_gate_t = jnp.pad(w[:, QKV_WIDTH:], ((0, 0), (0, LANES - N_GATES))).T
    cos_tok, sin_tok, cos_t, sin_t = _rope_tables(seq)

    feat_t, gates_t, xn = _nsa_inproj_feat(h, norm_rows, layer, w_feat_t, w_gate_t, tm=1024, tn=512)
    tok = _nsa_inproj_tok(xn, w_in, layer_j, cos_tok, sin_tok, seq, tm=min(1024, seq))

    x_cmp = tok[0:2 * NSA_GROUPS].reshape(2, NSA_GROUPS, t // CMP_STRIDE, CMP_STRIDE * HEAD_DIM)
    pe = phi_pe.reshape(phi_pe.shape[0], 2, 1, CMP_BLOCK * HEAD_DIM)
    b1 = phi_b1.reshape(phi_b1.shape[0], 2, 1, CMP_HIDDEN)
    kc = _compress(x_cmp, 0, pe, phi_w1, b1, phi_w2, layer_j, batch, feature_major=False)
    vc_t = _compress(x_cmp, 1, pe, phi_w1, b1, phi_w2[layer_j, 1].T, layer_j, batch, feature_major=True)

    tq = 256
    o1, selm = _cmp_attn(feat_t, kc, vc_t, gates_t, batch, seq, tq)
    o2 = _sel_attn(feat_t, tok, selm, cos_t, sin_t, gates_t, o1, batch, seq, tq, tk=512)
    o3 = _win_attn(feat_t, tok, cos_t, sin_t, gates_t, o2, batch, seq, tq)
    return _outproj(o3, w_out, layer_j, h, norm_rows, layer, 3, tm=512, tn=512)


def _conv_inproj_body(h_ref, halo_ref, g_ref, wb_ref, wc_ref, wu_ref, cw_ref, o_ref, xn_ref, *, tiles_per_seq):
    i = pl.program_id(0)
    j = pl.program_id(1)

    @pl.when(j == 0)
    def _():
        keep = jnp.where(i % tiles_per_seq == 0, 0.0, 1.0)
        xn_ref[0:CONV_HALO] = (_rms(halo_ref[...], g_ref[...]) * keep).astype(BF16)
        xn_ref[CONV_HALO:] = _rms(h_ref[...], g_ref[...]).astype(BF16)

    x = xn_ref[...]
    bg = _dot(x, wb_ref[...].astype(BF16))
    z = _dot(x, wc_ref[...].astype(BF16)) * _dot(x, wu_ref[...].astype(BF16))
    cw = cw_ref[...]
    conv = cw[2:3] * z + cw[1:2] * pltpu.roll(z, 1, axis=0) + cw[0:1] * pltpu.roll(z, 2, axis=0)
    o_ref[...] = (bg * conv)[CONV_HALO:].astype(BF16)


def _conv_inproj(h, norm_rows, layer, w_in, conv_w, layer_j, seq, tm, tn):
    t = h.shape[0]
    nj = D_MODEL // tn
    halo_blocks = tm // CONV_HALO
    return pl.pallas_call(
        functools.partial(_conv_inproj_body, tiles_per_seq=seq // tm),
        grid=(t // tm, nj),
        in_specs=[
            pl.BlockSpec((tm, D_MODEL), lambda i, j: (i, 0)),
            pl.BlockSpec((CONV_HALO, D_MODEL), lambda i, j: (jnp.maximum(i * halo_blocks - 1, 0), 0)),
            _norm_row_spec(layer, 2),
            pl.BlockSpec((None, D_MODEL, tn), lambda i, j: (layer_j, 0, j)),
            pl.BlockSpec((None, D_MODEL, tn), lambda i, j: (layer_j, 0, nj + j)),
            pl.BlockSpec((None, D_MODEL, tn), lambda i, j: (layer_j, 0, 2 * nj + j)),
            pl.BlockSpec((None, CONV_WIDTH, tn), lambda i, j: (layer_j, 0, j)),
        ],
        out_specs=pl.BlockSpec((tm, tn), lambda i, j: (i, j)),
        out_shape=jax.ShapeDtypeStruct((t, D_MODEL), BF16),
        scratch_shapes=[pltpu.VMEM((tm + CONV_HALO, D_MODEL), BF16)],
        compiler_params=_params(("parallel", "arbitrary")),
        name="conv_inproj",
    )(h, h, norm_rows, w_in, w_in, w_in, conv_w)


def _sgu_inproj_body(h_ref, g_ref, w_ref, lng_ref, lnb_ref, ws_ref, bs_ref, o_ref, xn_ref, z_ref, *, tm, tn):
    j = pl.program_id(1)

    @pl.when(j == 0)
    def _():
        xn_ref[...] = _rms(h_ref[...], g_ref[...]).astype(BF16)

    col = pl.multiple_of(j * tn, tn)
    z_ref[:, pl.ds(col, tn)] = jax.nn.gelu(_dot(xn_ref[...], w_ref[...].astype(BF16)))

    @pl.when(j == pl.num_programs(1) - 1)
    def _():
        v = z_ref[:, D_MODEL:]
        mu = jnp.mean(v, axis=-1, keepdims=True)
        var = jnp.mean(jnp.square(v - mu), axis=-1, keepdims=True)
        vn = ((v - mu) * lax.rsqrt(var + EPS) * lng_ref[...] + lnb_ref[...]).astype(BF16)
        row = lax.broadcasted_iota(jnp.int32, (SGU_CHUNK, SGU_CHUNK), 0)
        colm = lax.broadcasted_iota(jnp.int32, (SGU_CHUNK, SGU_CHUNK), 1)
        bs = bs_ref[...]
        for grp in range(SGU_GROUPS):
            ws = jnp.where(colm <= row, ws_ref[grp], 0.0).astype(BF16)
            cols = slice(grp * SGU_GROUP_DIM, (grp + 1) * SGU_GROUP_DIM)
            for c in range(tm // SGU_CHUNK):
                rws = slice(c * SGU_CHUNK, (c + 1) * SGU_CHUNK)
                sv = _dot(ws, vn[rws, cols]) + bs[:, grp:grp + 1]
                o_ref[rws, cols] = (z_ref[rws, cols] * sv).astype(BF16)


def _sgu_inproj(h, norm_rows, layer, w_in, ln_g, ln_b, w_s, b_s_t, layer_j, tm, tn):
    t = h.shape[0]
    return pl.pallas_call(
        functools.partial(_sgu_inproj_body, tm=tm, tn=tn),
        grid=(t // tm, 2 * D_MODEL // tn),
        in_specs=[
            pl.BlockSpec((tm, D_MODEL), lambda i, j: (i, 0)),
            _norm_row_spec(layer, 2),
            pl.BlockSpec((None, D_MODEL, tn), lambda i, j: (layer_j, 0, j)),
            pl.BlockSpec((None, 1, D_MODEL), lambda i, j: (layer_j, 0, 0)),
            pl.BlockSpec((None, 1, D_MODEL), lambda i, j: (layer_j, 0, 0)),
            pl.BlockSpec((None, SGU_GROUPS, SGU_CHUNK, SGU_CHUNK), lambda i, j: (layer_j, 0, 0, 0)),
            pl.BlockSpec((None, SGU_CHUNK, SGU_GROUPS), lambda i, j: (layer_j, 0, 0)),
        ],
        out_specs=pl.BlockSpec((tm, D_MODEL), lambda i, j: (i, 0)),
        out_shape=jax.ShapeDtypeStruct((t, D_MODEL), BF16),
        scratch_shapes=[pltpu.VMEM((tm, D_MODEL), BF16), pltpu.VMEM((tm, 2 * D_MODEL), F32)],
        compiler_params=_params(("parallel", "arbitrary")),
        name="sgu_inproj",
    )(h, norm_rows, w_in, ln_g, ln_b, w_s, b_s_t)


def kernel(x, p, norm_g, ffn1_wg, ffn1_wu, ffn1_wd, ffn2_wg, ffn2_wu, ffn2_wd, ple_wg, ple_wp, nsa_w_in, nsa_w_out, nsa_phi_pe, nsa_phi_w1, nsa_phi_b1, nsa_phi_w2, conv_w_in, conv_w, conv_w_out, sgu_w_in, sgu_ln_g, sgu_ln_b, sgu_w_s, sgu_b_s, sgu_w_out):
    batch, seq, d = x.shape
    depth = p.shape[0]
    t = batch * seq
    assert d == D_MODEL and seq % 512 == 0
    h = x.reshape(t, d)
    p2 = p.reshape(depth, t, PLE_DIM)
    norm_rows = norm_g.reshape(depth * N_NORMS, 1, d)
    sgu_ln_g3 = sgu_ln_g.reshape(-1, 1, d)
    sgu_ln_b3 = sgu_ln_b.reshape(-1, 1, d)
    sgu_b_s_t = jnp.swapaxes(sgu_b_s, 1, 2)
    tm_big = min(1024, t)

    for layer in range(depth):
        layer_j = layer // N_MIXERS
        h = _ffn(h, norm_rows, layer, 0, 1, ffn1_wg, ffn1_wu, ffn1_wd, tm=tm_big, tf=256)
        kind = layer % N_MIXERS
        if kind == 0:
            h = _nsa_mixer(h, norm_rows, layer, layer_j, batch, seq, nsa_w_in, nsa_w_out, nsa_phi_pe,
                           nsa_phi_w1, nsa_phi_b1, nsa_phi_w2)
        elif kind == 1:
            a = _conv_inproj(h, norm_rows, layer, conv_w_in, conv_w, layer_j, seq, tm=min(512, seq), tn=512)
            h = _outproj(a, conv_w_out, layer_j, h, norm_rows, layer, 3, tm=512, tn=512)
        else:
            a = _sgu_inproj(h, norm_rows, layer, sgu_w_in, sgu_ln_g3, sgu_ln_b3, sgu_w_s, sgu_b_s_t, layer_j,
                            tm=512, tn=512)
            h = _outproj(a, sgu_w_out, layer_j, h, norm_rows, layer, 3, tm=512, tn=512)
        h = _ffn(h, norm_rows, layer, 4, 5, ffn2_wg, ffn2_wu, ffn2_wd, tm=tm_big, tf=256)
        h = _ple(h, p2, norm_rows, layer, ple_wg, ple_wp, tm=512, tn=512)
    return h.reshape(batch, seq, d)
```

```python
import functools
import math

import jax
import jax.numpy as jnp
from jax import lax
from jax.experimental import pallas as pl
from jax.experimental.pallas import tpu as pltpu

F32 = jnp.float32
BF16 = jnp.bfloat16

EPS = 1e-6
LANES = 128
D_MODEL = 2048
D_FF = 5632
PLE_DIM = 256
N_NORMS = 8
N_MIXERS = 3
NSA_HEADS = 16
NSA_GROUPS = 4
HEADS_PER_GROUP = NSA_HEADS // NSA_GROUPS
HEAD_DIM = D_MODEL // NSA_HEADS
GROUP_WIDTH = HEADS_PER_GROUP * HEAD_DIM
KV_WIDTH = NSA_GROUPS * HEAD_DIM
ROPE_DIM = HEAD_DIM // 4
ROPE_HALF = ROPE_DIM // 2
ROPE_THETA = 500000.0
CMP_BLOCK = 32
CMP_STRIDE = 16
CMP_HIDDEN = 256
SEL_BLOCK = 64
SEL_TOP = 16
WINDOW = 512
QKV_WIDTH = NSA_HEADS * HEAD_DIM + 6 * KV_WIDTH
N_GATES = 3 * NSA_HEADS
SGU_CHUNK = 128
SGU_GROUPS = 8
SGU_GROUP_DIM = D_MODEL // SGU_GROUPS
CONV_WIDTH = 3
CONV_HALO = 16

MASK_VALUE = -1e30
BLOCK_MASK_BIG = 2.0 ** 100
SCORE_SCALE = HEAD_DIM ** -0.5
EXP2_SCALE = SCORE_SCALE * math.log2(math.e)
VMEM_LIMIT = 56 * 1024 * 1024


def _params(semantics):
    return pltpu.CompilerParams(dimension_semantics=semantics, vmem_limit_bytes=VMEM_LIMIT)


def _rms(x, g):
    return x * lax.rsqrt(jnp.mean(x * x, axis=-1, keepdims=True) + EPS) * g


def _dot(a, b):
    return jnp.dot(a, b, preferred_element_type=F32)


def _dot_nt(a, b):
    return lax.dot_general(a, b, (((1,), (1,)), ((), ())), preferred_element_type=F32)


def _norm_row_spec(layer, k):
    idx = layer * N_NORMS + k
    return pl.BlockSpec((None, 1, D_MODEL), lambda i, j: (idx, 0, 0))


def _ffn_body(h_ref, gpre_ref, gpost_ref, wg_ref, wu_ref, wd_ref, o_ref, xn_ref):
    f = pl.program_id(1)

    @pl.when(f == 0)
    def _():
        xn_ref[...] = _rms(h_ref[...], gpre_ref[...]).astype(BF16)
        o_ref[...] = jnp.zeros_like(o_ref)

    x = xn_ref[...]
    gate = _dot(x, wg_ref[...].astype(BF16))
    up = _dot(x, wu_ref[...].astype(BF16))
    act = (gate * jax.nn.sigmoid(gate) * up).astype(BF16)
    o_ref[...] += _dot(act, wd_ref[...].astype(BF16))

    @pl.when(f == pl.num_programs(1) - 1)
    def _():
        o_ref[...] = h_ref[...] + 0.5 * _rms(o_ref[...], gpost_ref[...])


def _ffn(h, norm_rows, layer, k_pre, k_post, wg, wu, wd, tm, tf):
    t = h.shape[0]
    return pl.pallas_call(
        _ffn_body,
        grid=(t // tm, D_FF // tf),
        in_specs=[
            pl.BlockSpec((tm, D_MODEL), lambda i, f: (i, 0), pipeline_mode=pl.Buffered(1)),
            _norm_row_spec(layer, k_pre),
            _norm_row_spec(layer, k_post),
            pl.BlockSpec((None, D_MODEL, tf), lambda i, f: (layer, 0, f)),
            pl.BlockSpec((None, D_MODEL, tf), lambda i, f: (layer, 0, f)),
            pl.BlockSpec((None, tf, D_MODEL), lambda i, f: (layer, f, 0)),
        ],
        out_specs=pl.BlockSpec((tm, D_MODEL), lambda i, f: (i, 0)),
        out_shape=jax.ShapeDtypeStruct((t, D_MODEL), F32),
        scratch_shapes=[pltpu.VMEM((tm, D_MODEL), BF16)],
        compiler_params=_params(("parallel", "arbitrary")),
        name="ffn",
    )(h, norm_rows, norm_rows, wg, wu, wd)


def _outproj_body(a_ref, w_ref, h_ref, g_ref, o_ref, *, tn):
    j = pl.program_id(1)
    col = pl.multiple_of(j * tn, tn)
    o_ref[:, pl.ds(col, tn)] = _dot(a_ref[...], w_ref[...].astype(BF16))

    @pl.when(j == pl.num_programs(1) - 1)
    def _():
        o_ref[...] = h_ref[...] + _rms(o_ref[...], g_ref[...])


def _outproj(a, w, w_layer, h, norm_rows, layer, k_norm, tm, tn):
    t, kdim = a.shape
    return pl.pallas_call(
        functools.partial(_outproj_body, tn=tn),
        grid=(t // tm, D_MODEL // tn),
        in_specs=[
            pl.BlockSpec((tm, kdim), lambda i, j: (i, 0)),
            pl.BlockSpec((None, kdim, tn), lambda i, j: (w_layer, 0, j)),
            pl.BlockSpec((tm, D_MODEL), lambda i, j: (i, 0), pipeline_mode=pl.Buffered(1)),
            _norm_row_spec(layer, k_norm),
        ],
        out_specs=pl.BlockSpec((tm, D_MODEL), lambda i, j: (i, 0)),
        out_shape=jax.ShapeDtypeStruct((t, D_MODEL), F32),
        compiler_params=_params(("parallel", "arbitrary")),
        name="outproj",
    )(a, w, h, norm_rows)


def _ple_body(h_ref, p_ref, gpre_ref, gpost_ref, wg_ref, wp_ref, o_ref, xn_ref, pb_ref, *, tn):
    j = pl.program_id(1)

    @pl.when(j == 0)
    def _():
        xn_ref[...] = _rms(h_ref[...], gpre_ref[...]).astype(BF16)
        pb_ref[...] = p_ref[...].astype(BF16)

    gate = jax.nn.sigmoid(_dot(xn_ref[...], wg_ref[...].astype(BF16)))
    emb = _dot(pb_ref[...], wp_ref[...].astype(BF16))
    col = pl.multiple_of(j * tn, tn)
    o_ref[:, pl.ds(col, tn)] = gate * emb

    @pl.when(j == pl.num_programs(1) - 1)
    def _():
        o_ref[...] = h_ref[...] + _rms(o_ref[...], gpost_ref[...])


def _ple(h, p, norm_rows, layer, wg, wp, tm, tn):
    t = h.shape[0]
    return pl.pallas_call(
        functools.partial(_ple_body, tn=tn),
        grid=(t // tm, D_MODEL // tn),
        in_specs=[
            pl.BlockSpec((tm, D_MODEL), lambda i, j: (i, 0), pipeline_mode=pl.Buffered(1)),
            pl.BlockSpec((None, tm, PLE_DIM), lambda i, j: (layer, i, 0)),
            _norm_row_spec(layer, 6),
            _norm_row_spec(layer, 7),
            pl.BlockSpec((None, D_MODEL, tn), lambda i, j: (layer, 0, j)),
            pl.BlockSpec((None, PLE_DIM, tn), lambda i, j: (layer, 0, j)),
        ],
        out_specs=pl.BlockSpec((tm, D_MODEL), lambda i, j: (i, 0)),
        out_shape=jax.ShapeDtypeStruct((t, D_MODEL), F32),
        scratch_shapes=[pltpu.VMEM((tm, D_MODEL), BF16), pltpu.VMEM((tm, PLE_DIM), BF16)],
        compiler_params=_params(("parallel", "arbitrary")),
        name="ple",
    )(h, p, norm_rows, norm_rows, wg, wp)


def _nsa_inproj_feat_body(h_ref, g_ref, w_ref, wg_ref, feat_ref, gates_ref, xn_ref, *, n_q_tiles):
    j = pl.program_id(1)

    @pl.when(j == 0)
    def _():
        xn = _rms(h_ref[...], g_ref[...]).astype(BF16)
        xn_ref[...] = xn
        gates_ref[...] = jax.nn.sigmoid(_dot_nt(wg_ref[...].astype(BF16), xn))

    row_scale = jnp.where(j < n_q_tiles, EXP2_SCALE, 1.0)
    feat_ref[...] = (_dot_nt(w_ref[...].astype(BF16), xn_ref[...]) * row_scale).astype(BF16)


def _nsa_inproj_feat(h, norm_rows, layer, w_feat_t, w_gate_t, tm, tn):
    t = h.shape[0]
    n_feat = w_feat_t.shape[0]
    return pl.pallas_call(
        functools.partial(_nsa_inproj_feat_body, n_q_tiles=NSA_HEADS * HEAD_DIM // tn),
        grid=(t // tm, n_feat // tn),
        in_specs=[
            pl.BlockSpec((tm, D_MODEL), lambda i, j: (i, 0), pipeline_mode=pl.Buffered(1)),
            _norm_row_spec(layer, 2),
            pl.BlockSpec((tn, D_MODEL), lambda i, j: (j, 0)),
            pl.BlockSpec((LANES, D_MODEL), lambda i, j: (0, 0)),
        ],
        out_specs=[
            pl.BlockSpec((tn, tm), lambda i, j: (j, i)),
            pl.BlockSpec((LANES, tm), lambda i, j: (0, i)),
            pl.BlockSpec((tm, D_MODEL), lambda i, j: (i, 0)),
        ],
        out_shape=[
            jax.ShapeDtypeStruct((n_feat, t), BF16),
            jax.ShapeDtypeStruct((LANES, t), F32),
            jax.ShapeDtypeStruct((t, D_MODEL), BF16),
        ],
        compiler_params=_params(("parallel", "arbitrary")),
        name="nsa_inproj_feat",
    )(h, norm_rows, w_feat_t, w_gate_t)


def _rope_token_major(x, cos, sin):
    lane = lax.broadcasted_iota(jnp.int32, x.shape, 1)
    partner = jnp.where(lane < ROPE_HALF, pltpu.roll(x, LANES - ROPE_HALF, axis=1),
                        pltpu.roll(x, ROPE_HALF, axis=1))
    return x * cos + partner * sin


def _nsa_inproj_tok_body(xn_ref, w_ref, cos_ref, sin_ref, o_ref):
    j = pl.program_id(1)
    res = _dot(xn_ref[...], w_ref[...].astype(BF16))
    n_slabs = o_ref.shape[0]

    @pl.when(j < 2)
    def _():
        for c in range(n_slabs):
            o_ref[c] = res[:, c * LANES:(c + 1) * LANES].astype(BF16)

    @pl.when(j >= 2)
    def _():
        cos, sin = cos_ref[...], sin_ref[...]
        for c in range(n_slabs):
            o_ref[c] = _rope_token_major(res[:, c * LANES:(c + 1) * LANES], cos, sin).astype(BF16)


def _nsa_inproj_tok(xn, w_in, w_layer, cos_tok, sin_tok, seq, tm):
    t = xn.shape[0]
    per_seq = seq // tm
    q_blocks = NSA_HEADS * HEAD_DIM // KV_WIDTH
    return pl.pallas_call(
        _nsa_inproj_tok_body,
        grid=(t // tm, 4),
        in_specs=[
            pl.BlockSpec((tm, D_MODEL), lambda i, j: (i, 0)),
            pl.BlockSpec((None, D_MODEL, KV_WIDTH), lambda i, j: (w_layer, 0, q_blocks + j + j // 3)),
            pl.BlockSpec((tm, LANES), lambda i, j: (i % per_seq, 0)),
            pl.BlockSpec((tm, LANES), lambda i, j: (i % per_seq, 0)),
        ],
        out_specs=pl.BlockSpec((NSA_GROUPS, tm, LANES), lambda i, j: (j, i, 0)),
        out_shape=jax.ShapeDtypeStruct((4 * NSA_GROUPS, t, LANES), BF16),
        compiler_params=_params(("parallel", "arbitrary")),
        name="nsa_inproj_tok",
    )(xn, w_in, cos_tok, sin_tok)


def _rope_tables(seq):
    inv_freq = jnp.power(jnp.float32(ROPE_THETA), -jnp.arange(0, ROPE_DIM, 2, dtype=F32) / ROPE_DIM)
    ang = jnp.arange(seq, dtype=F32)[:, None] * inv_freq[None, :]
    cos, sin = jnp.cos(ang), jnp.sin(ang)
    cos_tok = jnp.concatenate([cos, cos, jnp.ones((seq, LANES - ROPE_DIM), F32)], axis=1)
    sin_tok = jnp.concatenate([-sin, sin, jnp.zeros((seq, LANES - ROPE_DIM), F32)], axis=1)
    return cos_tok, sin_tok, cos.T, sin.T


def _compress_body(x_ref, pe_ref, w1_ref, b1_ref, w2_ref, o_ref, *, feature_major):
    half = CMP_STRIDE * HEAD_DIM
    x = x_ref[...]
    w1 = w1_ref[...].astype(BF16)
    first = _dot(x, w1[:half])
    second = _dot(x, w1[half:])
    n_chunks = x.shape[0]
    second = pltpu.roll(second, n_chunks - 1, axis=0)
    pe = jnp.broadcast_to(pe_ref[...], (8, CMP_BLOCK * HEAD_DIM)).astype(BF16)
    const = _dot(pe, w1)[0:1]
    hid = first + second + const + b1_ref[...]
    act = (hid * jax.nn.sigmoid(hid)).astype(BF16)
    if feature_major:
        o_ref[...] = _dot_nt(w2_ref[...].astype(BF16), act).astype(BF16)
    else:
        o_ref[...] = _dot(act, w2_ref[...].astype(BF16)).astype(BF16)


def _compress(x, which, pe, w1, b1, w2, layer_j, batch, feature_major):
    n_chunks = x.shape[2] // batch
    if feature_major:
        w2_spec = pl.BlockSpec((HEAD_DIM, CMP_HIDDEN), lambda g, b: (0, 0))
        out_block, out_dims = (None, None, HEAD_DIM, n_chunks), (NSA_GROUPS, batch, HEAD_DIM, n_chunks)
    else:
        w2_spec = pl.BlockSpec((None, None, CMP_HIDDEN, HEAD_DIM), lambda g, b: (layer_j, which, 0, 0))
        out_block, out_dims = (None, None, n_chunks, HEAD_DIM), (NSA_GROUPS, batch, n_chunks, HEAD_DIM)
    return pl.pallas_call(
        functools.partial(_compress_body, feature_major=feature_major),
        grid=(NSA_GROUPS, batch),
        in_specs=[
            pl.BlockSpec((None, None, n_chunks, CMP_STRIDE * HEAD_DIM), lambda g, b: (which, g, b, 0)),
            pl.BlockSpec((None, None, 1, CMP_BLOCK * HEAD_DIM), lambda g, b: (layer_j, which, 0, 0)),
            pl.BlockSpec((None, None, CMP_BLOCK * HEAD_DIM, CMP_HIDDEN), lambda g, b: (layer_j, which, 0, 0)),
            pl.BlockSpec((None, None, 1, CMP_HIDDEN), lambda g, b: (layer_j, which, 0, 0)),
            w2_spec,
        ],
        out_specs=pl.BlockSpec(out_block, lambda g, b: (g, b, 0, 0)),
        out_shape=jax.ShapeDtypeStruct(out_dims, BF16),
        compiler_params=_params(("parallel", "parallel")),
        name="compress",
    )(x, pe, w1, b1, w2)


def _lane_tile(x, n):
    return jnp.concatenate([x] * n, axis=1)


def _load_q_t(q_ref):
    return jnp.concatenate([q_ref[r * HEAD_DIM:(r + 1) * HEAD_DIM, :] for r in range(HEADS_PER_GROUP)], axis=1)


def _rope_feature_major(q, cos, sin):
    x1 = q[0:ROPE_HALF].astype(F32)
    x2 = q[ROPE_HALF:ROPE_DIM].astype(F32)
    r1 = (x1 * cos - x2 * sin).astype(BF16)
    r2 = (x2 * cos + x1 * sin).astype(BF16)
    return jnp.concatenate([r1, r2, q[ROPE_DIM:]], axis=0)


def _split3(x):
    hi = x.astype(BF16)
    r1 = x - hi.astype(F32)
    mid = r1.astype(BF16)
    lo = (r1 - mid.astype(F32)).astype(BF16)
    return hi, mid, lo


def _gate_row(gates_ref, branch, g, r):
    return gates_ref[pl.ds(branch * NSA_HEADS + g * HEADS_PER_GROUP + r, 1), :]


def _cmp_attn_body(q_ref, kc_ref, vc_ref, gates_ref, o_ref, sel_ref, *, tq, n_cmp):
    g = pl.program_id(1)
    qi = pl.program_id(2)
    t0 = qi * tq
    q = _load_q_t(q_ref)
    s = _dot(kc_ref[...], q)

    tpos = t0 + lax.broadcasted_iota(jnp.int32, (n_cmp, tq), 1)
    cend = lax.broadcasted_iota(jnp.int32, (n_cmp, tq), 0) * CMP_STRIDE + (CMP_BLOCK - 1)
    visible = cend <= tpos
    bias = _lane_tile(jnp.where(visible, 0.0, MASK_VALUE), HEADS_PER_GROUP)
    okf = _lane_tile(jnp.where(visible, 1.0, 0.0), HEADS_PER_GROUP)
    s = s + bias
    m = jnp.max(s, axis=0, keepdims=True)
    e = jnp.exp2(s - m) * okf
    inv = 1.0 / jnp.maximum(jnp.sum(e, axis=0, keepdims=True), 1e-30)
    p = e * inv
    o = _dot(vc_ref[...], p.astype(BF16))

    for r in range(HEADS_PER_GROUP):
        o_ref[r * HEAD_DIM:(r + 1) * HEAD_DIM, :] = _gate_row(gates_ref, 0, g, r) * o[:, r * tq:(r + 1) * tq]

    psum = p[:, 0:tq]
    for r in range(1, HEADS_PER_GROUP):
        psum = psum + p[:, r * tq:(r + 1) * tq]
    n_sel_rows = LANES // 2
    jrow = lax.broadcasted_iota(jnp.int32, (n_sel_rows, n_cmp), 0)
    ccol = lax.broadcasted_iota(jnp.int32, (n_sel_rows, n_cmp), 1)
    c_lo, c_hi = ccol * CMP_STRIDE, ccol * CMP_STRIDE + (CMP_BLOCK - 1)
    j_lo, j_hi = jrow * SEL_BLOCK, jrow * SEL_BLOCK + (SEL_BLOCK - 1)
    ov = jnp.maximum(jnp.minimum(c_hi, j_hi) - jnp.maximum(c_lo, j_lo) + 1, 0).astype(F32) / CMP_STRIDE
    ov = ov.astype(BF16)
    hi, mid, lo = _split3(psum)
    imp = _dot(ov, hi) + _dot(ov, mid) + _dot(ov, lo)

    blk = lax.broadcasted_iota(jnp.int32, (n_sel_rows, tq), 0)
    cur = (t0 + lax.broadcasted_iota(jnp.int32, (n_sel_rows, tq), 1)) // SEL_BLOCK
    forced = (blk == 0) | (blk == cur) | (blk == cur - 1)
    key = jnp.where(blk > cur, -1.0, imp)
    key = jnp.where(forced, 1e30, key)
    sel = jnp.zeros((n_sel_rows, tq), F32)
    for _ in range(SEL_TOP):
        mx = jnp.max(key, axis=0, keepdims=True)
        first = jnp.min(jnp.where(key == mx, blk, LANES), axis=0, keepdims=True)
        pick = blk == first
        sel = jnp.where(pick, 1.0, sel)
        key = jnp.where(pick, -2.0, key)
    sel = jnp.where(blk > cur, 0.0, sel)
    sel_ref[...] = jnp.concatenate([sel, jnp.zeros_like(sel)], axis=0).astype(BF16)


def _cmp_attn(feat_t, kc, vc_t, gates_t, batch, seq, tq):
    t = feat_t.shape[1]
    nq = seq // tq
    n_cmp = kc.shape[2]
    assert seq // SEL_BLOCK <= LANES // 2
    return pl.pallas_call(
        functools.partial(_cmp_attn_body, tq=tq, n_cmp=n_cmp),
        grid=(batch, NSA_GROUPS, nq),
        in_specs=[
            pl.BlockSpec((GROUP_WIDTH, tq), lambda b, g, i: (g, b * nq + i)),
            pl.BlockSpec((None, None, n_cmp, HEAD_DIM), lambda b, g, i: (g, b, 0, 0)),
            pl.BlockSpec((None, None, HEAD_DIM, n_cmp), lambda b, g, i: (g, b, 0, 0)),
            pl.BlockSpec((LANES, tq), lambda b, g, i: (0, b * nq + i)),
        ],
        out_specs=[
            pl.BlockSpec((GROUP_WIDTH, tq), lambda b, g, i: (g, b * nq + i)),
            pl.BlockSpec((None, LANES, tq), lambda b, g, i: (g, 0, b * nq + i)),
        ],
        out_shape=[
            jax.ShapeDtypeStruct((D_MODEL, t), F32),
            jax.ShapeDtypeStruct((NSA_GROUPS, LANES, t), BF16),
        ],
        compiler_params=_params(("parallel", "parallel", "parallel")),
        name="cmp_attn",
    )(feat_t, kc, vc_t, gates_t)


def _sel_attn_body(q_ref, k_ref, e_ref, v_ref, sel_ref, cos_ref, sin_ref, gates_ref, oin_ref, o_ref,
                   m_ref, l_ref, acc_ref, sa_ref, sb_ref, *, tq, tk):
    g = pl.program_id(1)
    qi = pl.program_id(2)
    t0 = qi * tq
    q = _rope_feature_major(_load_q_t(q_ref), _lane_tile(cos_ref[...], HEADS_PER_GROUP),
                            _lane_tile(sin_ref[...], HEADS_PER_GROUP))
    unselected = _lane_tile(sel_ref[...] - 1.0, HEADS_PER_GROUP)
    q_aug = jnp.concatenate([q, unselected.astype(BF16)], axis=0)

    m_ref[...] = jnp.full(m_ref.shape, MASK_VALUE, F32)
    l_ref[...] = jnp.zeros(l_ref.shape, F32)
    acc_ref[...] = jnp.zeros(acc_ref.shape, F32)

    def scores(ki):
        k0 = pl.multiple_of(ki * tk, tk)
        k_aug = jnp.concatenate([k_ref[pl.ds(k0, tk), :], e_ref[pl.ds(k0, tk), :]], axis=1)
        return _dot(k_aug, q_aug)

    def accumulate(s, ki, causal):
        k0 = pl.multiple_of(ki * tk, tk)
        if causal:
            kpos = k0 + lax.broadcasted_iota(jnp.int32, (tk, tq), 0)
            tpos = t0 + lax.broadcasted_iota(jnp.int32, (tk, tq), 1)
            s = s + _lane_tile(jnp.where(kpos <= tpos, 0.0, MASK_VALUE), HEADS_PER_GROUP)
        m_prev = m_ref[...]
        m_new = jnp.maximum(m_prev, jnp.max(s, axis=0, keepdims=True))
        alpha = jnp.exp2(m_prev - m_new)
        p = jnp.exp2(s - m_new)
        l_ref[...] = alpha * l_ref[...] + jnp.sum(p, axis=0, keepdims=True)
        acc_ref[...] = alpha * acc_ref[...] + _dot(v_ref[:, pl.ds(k0, tk)], p.astype(BF16))
        m_ref[...] = m_new

    n_before = t0 // tk
    sa_ref[...] = scores(0)

    def body(j, carry):
        sb_ref[...] = scores(2 * j + 1)
        accumulate(sa_ref[...], 2 * j, causal=False)
        sa_ref[...] = scores(2 * j + 2)
        accumulate(sb_ref[...], 2 * j + 1, causal=False)
        return carry

    lax.fori_loop(0, n_before // 2, body, 0)

    @pl.when(n_before % 2 == 1)
    def _():
        sb_ref[...] = scores(n_before)
        accumulate(sa_ref[...], n_before - 1, causal=False)
        accumulate(sb_ref[...], n_before, causal=True)

    @pl.when(n_before % 2 == 0)
    def _():
        accumulate(sa_ref[...], n_before, causal=True)

    o = acc_ref[...] * (1.0 / l_ref[...])
    for r in range(HEADS_PER_GROUP):
        rows = slice(r * HEAD_DIM, (r + 1) * HEAD_DIM)
        o_ref[rows, :] = oin_ref[rows, :] + _gate_row(gates_ref, 1, g, r) * o[:, r * tq:(r + 1) * tq]


def _sel_attn(feat_t, tok, selm, cos_t, sin_t, gates_t, oin, batch, seq, tq, tk):
    t = feat_t.shape[1]
    nq = seq // tq
    lanes = HEADS_PER_GROUP * tq
    v_row0 = NSA_HEADS
    assert tk % tq == 0 and seq % tk == 0
    key_block = jnp.arange(seq, dtype=jnp.int32)[:, None] // SEL_BLOCK
    block_of_key = jnp.where(key_block == jnp.arange(LANES, dtype=jnp.int32)[None, :], BLOCK_MASK_BIG, 0.0)
    return pl.pallas_call(
        functools.partial(_sel_attn_body, tq=tq, tk=tk),
        grid=(batch, NSA_GROUPS, nq),
        in_specs=[
            pl.BlockSpec((GROUP_WIDTH, tq), lambda b, g, i: (g, b * nq + i)),
            pl.BlockSpec((None, seq, HEAD_DIM), lambda b, g, i: (2 * NSA_GROUPS + g, b, 0)),
            pl.BlockSpec((seq, LANES), lambda b, g, i: (0, 0)),
            pl.BlockSpec((HEAD_DIM, seq), lambda b, g, i: (v_row0 + g, b)),
            pl.BlockSpec((None, LANES, tq), lambda b, g, i: (g, 0, b * nq + i)),
            pl.BlockSpec((ROPE_HALF, tq), lambda b, g, i: (0, i)),
            pl.BlockSpec((ROPE_HALF, tq), lambda b, g, i: (0, i)),
            pl.BlockSpec((LANES, tq), lambda b, g, i: (0, b * nq + i)),
            pl.BlockSpec((GROUP_WIDTH, tq), lambda b, g, i: (g, b * nq + i)),
        ],
        out_specs=pl.BlockSpec((GROUP_WIDTH, tq), lambda b, g, i: (g, b * nq + i)),
        out_shape=jax.ShapeDtypeStruct((D_MODEL, t), F32),
        scratch_shapes=[
            pltpu.VMEM((1, lanes), F32),
            pltpu.VMEM((1, lanes), F32),
            pltpu.VMEM((HEAD_DIM, lanes), F32),
            pltpu.VMEM((tk, lanes), F32),
            pltpu.VMEM((tk, lanes), F32),
        ],
        compiler_params=_params(("parallel", "parallel", "parallel")),
        name="attn_sel",
    )(feat_t, tok, block_of_key.astype(BF16), feat_t, selm, cos_t, sin_t, gates_t, oin)


def _win_attn_body(q_ref, k_ref, v_ref, cos_ref, sin_ref, gates_ref, oin_ref, o_ref, *, tq):
    g = pl.program_id(1)
    qi = pl.program_id(2)
    t0 = qi * tq
    band = WINDOW + tq
    q = _rope_feature_major(_load_q_t(q_ref), _lane_tile(cos_ref[...], HEADS_PER_GROUP),
                            _lane_tile(sin_ref[...], HEADS_PER_GROUP))
    start = pl.multiple_of(jnp.maximum(t0 - WINDOW, 0), tq)
    s = _dot(k_ref[pl.ds(start, band), :], q)
    kpos = start + lax.broadcasted_iota(jnp.int32, (band, tq), 0)
    tpos = t0 + lax.broadcasted_iota(jnp.int32, (band, tq), 1)
    valid = (kpos <= tpos) & (kpos > tpos - WINDOW)
    s = s + _lane_tile(jnp.where(valid, 0.0, MASK_VALUE), HEADS_PER_GROUP)
    m = jnp.max(s, axis=0, keepdims=True)
    p = jnp.exp2(s - m)
    inv = 1.0 / jnp.sum(p, axis=0, keepdims=True)
    o = _dot(v_ref[:, pl.ds(start, band)], p.astype(BF16)) * inv
    for r in range(HEADS_PER_GROUP):
        rows = slice(r * HEAD_DIM, (r + 1) * HEAD_DIM)
        total = oin_ref[rows, :] + _gate_row(gates_ref, 2, g, r) * o[:, r * tq:(r + 1) * tq]
        o_ref[:, rows] = total.T.astype(BF16)


def _win_attn(feat_t, tok, cos_t, sin_t, gates_t, oin, batch, seq, tq):
    t = feat_t.shape[1]
    nq = seq // tq
    v_row0 = NSA_HEADS + NSA_GROUPS
    assert seq >= WINDOW + tq
    return pl.pallas_call(
        functools.partial(_win_attn_body, tq=tq),
        grid=(batch, NSA_GROUPS, nq),
        in_specs=[
            pl.BlockSpec((GROUP_WIDTH, tq), lambda b, g, i: (g, b * nq + i)),
            pl.BlockSpec((None, seq, HEAD_DIM), lambda b, g, i: (3 * NSA_GROUPS + g, b, 0)),
            pl.BlockSpec((HEAD_DIM, seq), lambda b, g, i: (v_row0 + g, b)),
            pl.BlockSpec((ROPE_HALF, tq), lambda b, g, i: (0, i)),
            pl.BlockSpec((ROPE_HALF, tq), lambda b, g, i: (0, i)),
            pl.BlockSpec((LANES, tq), lambda b, g, i: (0, b * nq + i)),
            pl.BlockSpec((GROUP_WIDTH, tq), lambda b, g, i: (g, b * nq + i)),
        ],
        out_specs=pl.BlockSpec((tq, GROUP_WIDTH), lambda b, g, i: (b * nq + i, g)),
        out_shape=jax.ShapeDtypeStruct((t, D_MODEL), BF16),
        compiler_params=_params(("parallel", "parallel", "parallel")),
        name="attn_win",
    )(feat_t, tok, feat_t, cos_t, sin_t, gates_t, oin)


def _nsa_mixer(h, norm_rows, layer, layer_j, batch, seq, w_in, w_out, phi_pe, phi_w1, phi_b1, phi_w2):
    t = h.shape[0]
    w = w_in[layer_j]
    q_end = NSA_HEADS * HEAD_DIM
    vs0, vw0 = q_end + 3 * KV_WIDTH, q_end + 5 * KV_WIDTH
    w_feat_t = jnp.concatenate([w[:, :q_end], w[:, vs0:vs0 + KV_WIDTH], w[:, vw0:vw0 + KV_WIDTH]], axis=1).T
    w_gate_t = jnp.pad(w[:, QKV_WIDTH:], ((0, 0), (0, LANES - N_GATES))).T
    cos_tok, sin_tok, cos_t, sin_t = _rope_tables(seq)

    feat_t, gates_t, xn = _nsa_inproj_feat(h, norm_rows, layer, w_feat_t, w_gate_t, tm=1024, tn=512)
    tok = _nsa_inproj_tok(xn, w_in, layer_j, cos_tok, sin_tok, seq, tm=min(1024, seq))

    x_cmp = tok[0:2 * NSA_GROUPS].reshape(2, NSA_GROUPS, t // CMP_STRIDE, CMP_STRIDE * HEAD_DIM)
    pe = phi_pe.reshape(phi_pe.shape[0], 2, 1, CMP_BLOCK * HEAD_DIM)
    b1 = phi_b1.reshape(phi_b1.shape[0], 2, 1, CMP_HIDDEN)
    kc = _compress(x_cmp, 0, pe, phi_w1, b1, phi_w2, layer_j, batch, feature_major=False)
    vc_t = _compress(x_cmp, 1, pe, phi_w1, b1, phi_w2[layer_j, 1].T, layer_j, batch, feature_major=True)

    tq = 256
    o1, selm = _cmp_attn(feat_t, kc, vc_t, gates_t, batch, seq, tq)
    o2 = _sel_attn(feat_t, tok, selm, cos_t, sin_t, gates_t, o1, batch, seq, tq, tk=512)
    o3 = _win_attn(feat_t, tok, cos_t, sin_t, gates_t, o2, batch, seq, tq)
    return _outproj(o3, w_out, layer_j, h, norm_rows, layer, 3, tm=min(1024, t), tn=512)


def _conv_inproj_body(h_ref, halo_ref, g_ref, wb_ref, wc_ref, wu_ref, cw_ref, o_ref, xn_ref, *, tiles_per_seq):
    i = pl.program_id(0)
    j = pl.program_id(1)

    @pl.when(j == 0)
    def _():
        keep = jnp.where(i % tiles_per_seq == 0, 0.0, 1.0)
        xn_ref[0:CONV_HALO] = (_rms(halo_ref[...], g_ref[...]) * keep).astype(BF16)
        xn_ref[CONV_HALO:] = _rms(h_ref[...], g_ref[...]).astype(BF16)

    x = xn_ref[...]
    bg = _dot(x, wb_ref[...].astype(BF16))
    z = _dot(x, wc_ref[...].astype(BF16)) * _dot(x, wu_ref[...].astype(BF16))
    cw = cw_ref[...]
    conv = cw[2:3] * z + cw[1:2] * pltpu.roll(z, 1, axis=0) + cw[0:1] * pltpu.roll(z, 2, axis=0)
    o_ref[...] = (bg * conv)[CONV_HALO:].astype(BF16)


def _conv_inproj(h, norm_rows, layer, w_in, conv_w, layer_j, seq, tm, tn):
    t = h.shape[0]
    nj = D_MODEL // tn
    halo_blocks = tm // CONV_HALO
    return pl.pallas_call(
        functools.partial(_conv_inproj_body, tiles_per_seq=seq // tm),
        grid=(t // tm, nj),
        in_specs=[
            pl.BlockSpec((tm, D_MODEL), lambda i, j: (i, 0), pipeline_mode=pl.Buffered(1)),
            pl.BlockSpec((CONV_HALO, D_MODEL), lambda i, j: (jnp.maximum(i * halo_blocks - 1, 0), 0)),
            _norm_row_spec(layer, 2),
            pl.BlockSpec((None, D_MODEL, tn), lambda i, j: (layer_j, 0, j)),
            pl.BlockSpec((None, D_MODEL, tn), lambda i, j: (layer_j, 0, nj + j)),
            pl.BlockSpec((None, D_MODEL, tn), lambda i, j: (layer_j, 0, 2 * nj + j)),
            pl.BlockSpec((None, CONV_WIDTH, tn), lambda i, j: (layer_j, 0, j)),
        ],
        out_specs=pl.BlockSpec((tm, tn), lambda i, j: (i, j)),
        out_shape=jax.ShapeDtypeStruct((t, D_MODEL), BF16),
        scratch_shapes=[pltpu.VMEM((tm + CONV_HALO, D_MODEL), BF16)],
        compiler_params=_params(("parallel", "arbitrary")),
        name="conv_inproj",
    )(h, h, norm_rows, w_in, w_in, w_in, conv_w)


def _sgu_inproj_body(h_ref, g_ref, w_ref, lng_ref, lnb_ref, ws_ref, bs_ref, o_ref, xn_ref, z_ref, *, tm, tn):
    j = pl.program_id(1)

    @pl.when(j == 0)
    def _():
        xn_ref[...] = _rms(h_ref[...], g_ref[...]).astype(BF16)

    col = pl.multiple_of(j * tn, tn)
    z_ref[:, pl.ds(col, tn)] = jax.nn.gelu(_dot(xn_ref[...], w_ref[...].astype(BF16)))

    @pl.when(j == pl.num_programs(1) - 1)
    def _():
        v = z_ref[:, D_MODEL:]
        mu = jnp.mean(v, axis=-1, keepdims=True)
        var = jnp.mean(jnp.square(v - mu), axis=-1, keepdims=True)
        vn = ((v - mu) * lax.rsqrt(var + EPS) * lng_ref[...] + lnb_ref[...]).astype(BF16)
        row = lax.broadcasted_iota(jnp.int32, (SGU_CHUNK, SGU_CHUNK), 0)
        colm = lax.broadcasted_iota(jnp.int32, (SGU_CHUNK, SGU_CHUNK), 1)
        bs = bs_ref[...]
        for grp in range(SGU_GROUPS):
            ws = jnp.where(colm <= row, ws_ref[grp], 0.0).astype(BF16)
            cols = slice(grp * SGU_GROUP_DIM, (grp + 1) * SGU_GROUP_DIM)
            for c in range(tm // SGU_CHUNK):
                rws = slice(c * SGU_CHUNK, (c + 1) * SGU_CHUNK)
                sv = _dot(ws, vn[rws, cols]) + bs[:, grp:grp + 1]
                o_ref[rws, cols] = (z_ref[rws, cols] * sv).astype(BF16)


def _sgu_inproj(h, norm_rows, layer, w_in, ln_g, ln_b, w_s, b_s_t, layer_j, tm, tn):
    t = h.shape[0]
    return pl.pallas_call(
        functools.partial(_sgu_inproj_body, tm=tm, tn=tn),
        grid=(t // tm, 2 * D_MODEL // tn),
        in_specs=[
            pl.BlockSpec((tm, D_MODEL), lambda i, j: (i, 0)),
            _norm_row_spec(layer, 2),
            pl.BlockSpec((None, D_MODEL, tn), lambda i, j: (layer_j, 0, j)),
            pl.BlockSpec((None, 1, D_MODEL), lambda i, j: (layer_j, 0, 0)),
            pl.BlockSpec((None, 1, D_MODEL), lambda i, j: (layer_j, 0, 0)),
            pl.BlockSpec((None, SGU_GROUPS, SGU_CHUNK, SGU_CHUNK), lambda i, j: (layer_j, 0, 0, 0)),
            pl.BlockSpec((None, SGU_CHUNK, SGU_GROUPS), lambda i, j: (layer_j, 0, 0)),
        ],
        out_specs=pl.BlockSpec((tm, D_MODEL), lambda i, j: (i, 0)),
        out_shape=jax.ShapeDtypeStruct((t, D_MODEL), BF16),
        scratch_shapes=[pltpu.VMEM((tm, D_MODEL), BF16), pltpu.VMEM((tm, 2 * D_MODEL), F32)],
        compiler_params=_params(("parallel", "arbitrary")),
        name="sgu_inproj",
    )(h, norm_rows, w_in, ln_g, ln_b, w_s, b_s_t)


def kernel(x, p, norm_g, ffn1_wg, ffn1_wu, ffn1_wd, ffn2_wg, ffn2_wu, ffn2_wd, ple_wg, ple_wp, nsa_w_in, nsa_w_out, nsa_phi_pe, nsa_phi_w1, nsa_phi_b1, nsa_phi_w2, conv_w_in, conv_w, conv_w_out, sgu_w_in, sgu_ln_g, sgu_ln_b, sgu_w_s, sgu_b_s, sgu_w_out):
    batch, seq, d = x.shape
    depth = p.shape[0]
    t = batch * seq
    assert d == D_MODEL and seq % 512 == 0
    h = x.reshape(t, d)
    p2 = p.reshape(depth, t, PLE_DIM)
    norm_rows = norm_g.reshape(depth * N_NORMS, 1, d)
    sgu_ln_g3 = sgu_ln_g.reshape(-1, 1, d)
    sgu_ln_b3 = sgu_ln_b.reshape(-1, 1, d)
    sgu_b_s_t = jnp.swapaxes(sgu_b_s, 1, 2)
    tm_big = min(1024, t)

    for layer in range(depth):
        layer_j = layer // N_MIXERS
        h = _ffn(h, norm_rows, layer, 0, 1, ffn1_wg, ffn1_wu, ffn1_wd, tm=tm_big, tf=256)
        kind = layer % N_MIXERS
        if kind == 0:
            h = _nsa_mixer(h, norm_rows, layer, layer_j, batch, seq, nsa_w_in, nsa_w_out, nsa_phi_pe,
                           nsa_phi_w1, nsa_phi_b1, nsa_phi_w2)
        elif kind == 1:
            a = _conv_inproj(h, norm_rows, layer, conv_w_in, conv_w, layer_j, seq, tm=min(1024, seq), tn=512)
            h = _outproj(a, conv_w_out, layer_j, h, norm_rows, layer, 3, tm=tm_big, tn=512)
        else:
            a = _sgu_inproj(h, norm_rows, layer, sgu_w_in, sgu_ln_g3, sgu_ln_b3, sgu_w_s, sgu_b_s_t, layer_j,
                            tm=512, tn=512)
            h = _outproj(a, sgu_w_out, layer_j, h, norm_rows, layer, 3, tm=tm_big, tn=512)
        h = _ffn(h, norm_rows, layer, 4, 5, ffn2_wg, ffn2_wu, ffn2_wd, tm=tm_big, tf=256)
        h = _ple(h, p2, norm_rows, layer, ple_wg, ple_wp, tm=tm_big, tn=512)
    return h.reshape(batch, seq, d)
```

```python
import functools
import math

import jax
import jax.numpy as jnp
from jax import lax
from jax.experimental import pallas as pl
from jax.experimental.pallas import tpu as pltpu

F32 = jnp.float32
BF16 = jnp.bfloat16

EPS = 1e-6
LANES = 128
D_MODEL = 2048
D_FF = 5632
PLE_DIM = 256
N_NORMS = 8
N_MIXERS = 3
NSA_HEADS = 16
NSA_GROUPS = 4
HEADS_PER_GROUP = NSA_HEADS // NSA_GROUPS
HEAD_DIM = D_MODEL // NSA_HEADS
GROUP_WIDTH = HEADS_PER_GROUP * HEAD_DIM
KV_WIDTH = NSA_GROUPS * HEAD_DIM
ROPE_DIM = HEAD_DIM // 4
ROPE_HALF = ROPE_DIM // 2
ROPE_THETA = 500000.0
CMP_BLOCK = 32
CMP_STRIDE = 16
CMP_HIDDEN = 256
SEL_BLOCK = 64
SEL_TOP = 16
WINDOW = 512
QKV_WIDTH = NSA_HEADS * HEAD_DIM + 6 * KV_WIDTH
N_GATES = 3 * NSA_HEADS
SGU_CHUNK = 128
SGU_GROUPS = 8
SGU_GROUP_DIM = D_MODEL // SGU_GROUPS
CONV_WIDTH = 3
CONV_HALO = 16

MASK_VALUE = -1e30
BLOCK_MASK_BIG = 2.0 ** 100
SCORE_SCALE = HEAD_DIM ** -0.5
EXP2_SCALE = SCORE_SCALE * math.log2(math.e)
VMEM_LIMIT = 60 * 1024 * 1024


def _params(semantics):
    return pltpu.CompilerParams(dimension_semantics=semantics, vmem_limit_bytes=VMEM_LIMIT)


def _rms(x, g):
    return x * lax.rsqrt(jnp.mean(x * x, axis=-1, keepdims=True) + EPS) * g


def _dot(a, b):
    return jnp.dot(a, b, preferred_element_type=F32)


def _dot_nt(a, b):
    return lax.dot_general(a, b, (((1,), (1,)), ((), ())), preferred_element_type=F32)


def _norm_row_spec(layer, k):
    idx = layer * N_NORMS + k
    return pl.BlockSpec((None, 1, D_MODEL), lambda i, j: (idx, 0, 0))


def _ffn_body(h_ref, gpre_ref, gpost_ref, wg_ref, wu_ref, wd_ref, o_ref, xn_ref):
    f = pl.program_id(1)

    @pl.when(f == 0)
    def _():
        xn_ref[...] = _rms(h_ref[...], gpre_ref[...]).astype(BF16)
        o_ref[...] = jnp.zeros_like(o_ref)

    x = xn_ref[...]
    gate = _dot(x, wg_ref[...].astype(BF16))
    up = _dot(x, wu_ref[...].astype(BF16))
    act = (gate * jax.nn.sigmoid(gate) * up).astype(BF16)
    o_ref[...] += _dot(act, wd_ref[...].astype(BF16))

    @pl.when(f == pl.num_programs(1) - 1)
    def _():
        o_ref[...] = h_ref[...] + 0.5 * _rms(o_ref[...], gpost_ref[...])


def _ffn(h, norm_rows, layer, k_pre, k_post, wg, wu, wd, tm, tf):
    t = h.shape[0]
    return pl.pallas_call(
        _ffn_body,
        grid=(t // tm, D_FF // tf),
        in_specs=[
            pl.BlockSpec((tm, D_MODEL), lambda i, f: (i, 0)),
            _norm_row_spec(layer, k_pre),
            _norm_row_spec(layer, k_post),
            pl.BlockSpec((None, D_MODEL, tf), lambda i, f: (layer, 0, f)),
            pl.BlockSpec((None, D_MODEL, tf), lambda i, f: (layer, 0, f)),
            pl.BlockSpec((None, tf, D_MODEL), lambda i, f: (layer, f, 0)),
        ],
        out_specs=pl.BlockSpec((tm, D_MODEL), lambda i, f: (i, 0)),
        out_shape=jax.ShapeDtypeStruct((t, D_MODEL), F32),
        scratch_shapes=[pltpu.VMEM((tm, D_MODEL), BF16)],
        compiler_params=_params(("parallel", "arbitrary")),
        name="ffn",
    )(h, norm_rows, norm_rows, wg, wu, wd)


def _outproj_body(a_ref, w_ref, h_ref, g_ref, o_ref, *, tn):
    j = pl.program_id(1)
    col = pl.multiple_of(j * tn, tn)
    o_ref[:, pl.ds(col, tn)] = _dot(a_ref[...], w_ref[...].astype(BF16))

    @pl.when(j == pl.num_programs(1) - 1)
    def _():
        o_ref[...] = h_ref[...] + _rms(o_ref[...], g_ref[...])


def _outproj(a, w, w_layer, h, norm_rows, layer, k_norm, tm, tn):
    t, kdim = a.shape
    return pl.pallas_call(
        functools.partial(_outproj_body, tn=tn),
        grid=(t // tm, D_MODEL // tn),
        in_specs=[
            pl.BlockSpec((tm, kdim), lambda i, j: (i, 0)),
            pl.BlockSpec((None, kdim, tn), lambda i, j: (w_layer, 0, j)),
            pl.BlockSpec((tm, D_MODEL), lambda i, j: (i, 0)),
            _norm_row_spec(layer, k_norm),
        ],
        out_specs=pl.BlockSpec((tm, D_MODEL), lambda i, j: (i, 0)),
        out_shape=jax.ShapeDtypeStruct((t, D_MODEL), F32),
        compiler_params=_params(("parallel", "arbitrary")),
        name="outproj",
    )(a, w, h, norm_rows)


def _ple_body(h_ref, p_ref, gpre_ref, gpost_ref, wg_ref, wp_ref, o_ref, xn_ref, pb_ref, *, tn):
    j = pl.program_id(1)

    @pl.when(j == 0)
    def _():
        xn_ref[...] = _rms(h_ref[...], gpre_ref[...]).astype(BF16)
        pb_ref[...] = p_ref[...].astype(BF16)

    gate = jax.nn.sigmoid(_dot(xn_ref[...], wg_ref[...].astype(BF16)))
    emb = _dot(pb_ref[...], wp_ref[...].astype(BF16))
    col = pl.multiple_of(j * tn, tn)
    o_ref[:, pl.ds(col, tn)] = gate * emb

    @pl.when(j == pl.num_programs(1) - 1)
    def _():
        o_ref[...] = h_ref[...] + _rms(o_ref[...], gpost_ref[...])


def _ple(h, p, norm_rows, layer, wg, wp, tm, tn):
    t = h.shape[0]
    return pl.pallas_call(
        functools.partial(_ple_body, tn=tn),
        grid=(t // tm, D_MODEL // tn),
        in_specs=[
            pl.BlockSpec((tm, D_MODEL), lambda i, j: (i, 0)),
            pl.BlockSpec((None, tm, PLE_DIM), lambda i, j: (layer, i, 0)),
            _norm_row_spec(layer, 6),
            _norm_row_spec(layer, 7),
            pl.BlockSpec((None, D_MODEL, tn), lambda i, j: (layer, 0, j)),
            pl.BlockSpec((None, PLE_DIM, tn), lambda i, j: (layer, 0, j)),
        ],
        out_specs=pl.BlockSpec((tm, D_MODEL), lambda i, j: (i, 0)),
        out_shape=jax.ShapeDtypeStruct((t, D_MODEL), F32),
        scratch_shapes=[pltpu.VMEM((tm, D_MODEL), BF16), pltpu.VMEM((tm, PLE_DIM), BF16)],
        compiler_params=_params(("parallel", "arbitrary")),
        name="ple",
    )(h, p, norm_rows, norm_rows, wg, wp)


def _nsa_inproj_feat_body(h_ref, g_ref, w_ref, wg_ref, feat_ref, gates_ref, xn_ref, *, n_q_tiles):
    j = pl.program_id(1)

    @pl.when(j == 0)
    def _():
        xn = _rms(h_ref[...], g_ref[...]).astype(BF16)
        xn_ref[...] = xn
        gates_ref[...] = jax.nn.sigmoid(_dot(xn, wg_ref[...].astype(BF16))).T

    row_scale = jnp.where(j < n_q_tiles, EXP2_SCALE, 1.0)
    feat_ref[...] = (_dot(xn_ref[...], w_ref[...].astype(BF16)) * row_scale).T.astype(BF16)


def _nsa_inproj_feat(h, norm_rows, layer, w_in, w_layer, w_gate, tm):
    t = h.shape[0]
    tn = KV_WIDTH
    n_q_tiles = NSA_HEADS * HEAD_DIM // tn
    n_feat = NSA_HEADS * HEAD_DIM + 2 * KV_WIDTH
    return pl.pallas_call(
        functools.partial(_nsa_inproj_feat_body, n_q_tiles=n_q_tiles),
        grid=(t // tm, n_feat // tn),
        in_specs=[
            pl.BlockSpec((tm, D_MODEL), lambda i, j: (i, 0)),
            _norm_row_spec(layer, 2),
            pl.BlockSpec((None, D_MODEL, tn),
                         lambda i, j: (w_layer, 0, j + 3 * (j // n_q_tiles) + j // (n_q_tiles + 1))),
            pl.BlockSpec((D_MODEL, LANES), lambda i, j: (0, 0)),
        ],
        out_specs=[
            pl.BlockSpec((tn, tm), lambda i, j: (j, i)),
            pl.BlockSpec((LANES, tm), lambda i, j: (0, i)),
            pl.BlockSpec((tm, D_MODEL), lambda i, j: (i, 0)),
        ],
        out_shape=[
            jax.ShapeDtypeStruct((n_feat, t), BF16),
            jax.ShapeDtypeStruct((LANES, t), F32),
            jax.ShapeDtypeStruct((t, D_MODEL), BF16),
        ],
        compiler_params=_params(("parallel", "arbitrary")),
        name="nsa_inproj_feat",
    )(h, norm_rows, w_in, w_gate)


def _rope_token_major(x, cos, sin):
    lane = lax.broadcasted_iota(jnp.int32, x.shape, 1)
    partner = jnp.where(lane < ROPE_HALF, pltpu.roll(x, LANES - ROPE_HALF, axis=1),
                        pltpu.roll(x, ROPE_HALF, axis=1))
    return x * cos + partner * sin


def _nsa_inproj_tok_body(xn_ref, wc_ref, wk_ref, cos_ref, sin_ref, oc_ref, ok_ref):
    xn = xn_ref[...]
    res_c = _dot(xn, wc_ref[...].astype(BF16))
    res_k = _dot(xn, wk_ref[...].astype(BF16))
    cos, sin = cos_ref[...], sin_ref[...]
    for c in range(NSA_GROUPS):
        cols = slice(c * LANES, (c + 1) * LANES)
        oc_ref[c] = res_c[:, cols]
        ok_ref[c] = _rope_token_major(res_k[:, cols], cos, sin).astype(BF16)


def _nsa_inproj_tok(xn, w_in, w_layer, cos_tok, sin_tok, seq, tm):
    t = xn.shape[0]
    per_seq = seq // tm
    q_blocks = NSA_HEADS * HEAD_DIM // KV_WIDTH
    slab_spec = pl.BlockSpec((NSA_GROUPS, tm, LANES), lambda i, j: (j, i, 0))
    return pl.pallas_call(
        _nsa_inproj_tok_body,
        grid=(t // tm, 2),
        in_specs=[
            pl.BlockSpec((tm, D_MODEL), lambda i, j: (i, 0)),
            pl.BlockSpec((None, D_MODEL, KV_WIDTH), lambda i, j: (w_layer, 0, q_blocks + j)),
            pl.BlockSpec((None, D_MODEL, KV_WIDTH), lambda i, j: (w_layer, 0, q_blocks + 2 + 2 * j)),
            pl.BlockSpec((tm, LANES), lambda i, j: (i % per_seq, 0)),
            pl.BlockSpec((tm, LANES), lambda i, j: (i % per_seq, 0)),
        ],
        out_specs=[slab_spec, slab_spec],
        out_shape=[
            jax.ShapeDtypeStruct((2 * NSA_GROUPS, t, LANES), F32),
            jax.ShapeDtypeStruct((2 * NSA_GROUPS, t, LANES), BF16),
        ],
        compiler_params=_params(("parallel", "arbitrary")),
        name="nsa_inproj_tok",
    )(xn, w_in, w_in, cos_tok, sin_tok)


def _rope_tables(seq):
    inv_freq = jnp.power(jnp.float32(ROPE_THETA), -jnp.arange(0, ROPE_DIM, 2, dtype=F32) / ROPE_DIM)
    ang = jnp.arange(seq, dtype=F32)[:, None] * inv_freq[None, :]
    cos, sin = jnp.cos(ang), jnp.sin(ang)
    cos_tok = jnp.concatenate([cos, cos, jnp.ones((seq, LANES - ROPE_DIM), F32)], axis=1)
    sin_tok = jnp.concatenate([-sin, sin, jnp.zeros((seq, LANES - ROPE_DIM), F32)], axis=1)
    return cos_tok, sin_tok, cos.T, sin.T


def _compress_body(x_ref, pe_ref, w1_ref, b1_ref, w2_ref, o_ref, *, feature_major):
    n_chunks = x_ref.shape[0] // CMP_STRIDE
    w1 = w1_ref[...].astype(BF16)
    both = jnp.zeros((n_chunks, 2 * CMP_HIDDEN), F32)
    for l in range(CMP_STRIDE):
        x_l = x_ref[pl.ds(l, n_chunks, stride=CMP_STRIDE), :].astype(BF16)
        lo, hi = l * HEAD_DIM, (CMP_STRIDE + l) * HEAD_DIM
        w_l = jnp.concatenate([w1[lo:lo + HEAD_DIM], w1[hi:hi + HEAD_DIM]], axis=1)
        both = both + _dot(x_l, w_l)
    first = both[:, :CMP_HIDDEN]
    second = pltpu.roll(both[:, CMP_HIDDEN:], n_chunks - 1, axis=0)
    pe = jnp.broadcast_to(pe_ref[...], (8, CMP_BLOCK * HEAD_DIM)).astype(BF16)
    const = _dot(pe, w1)[0:1]
    hid = first + second + const + b1_ref[...]
    act = (hid * jax.nn.sigmoid(hid)).astype(BF16)
    if feature_major:
        o_ref[...] = _dot_nt(w2_ref[...].astype(BF16), act).astype(BF16)
    else:
        o_ref[...] = _dot(act, w2_ref[...].astype(BF16)).astype(BF16)


def _compress(x, which, pe, w1, b1, w2, layer_j, batch, feature_major):
    seq = x.shape[1] // batch
    n_chunks = seq // CMP_STRIDE
    if feature_major:
        w2_spec = pl.BlockSpec((HEAD_DIM, CMP_HIDDEN), lambda g, b: (0, 0))
        out_block, out_dims = (None, None, HEAD_DIM, n_chunks), (NSA_GROUPS, batch, HEAD_DIM, n_chunks)
    else:
        w2_spec = pl.BlockSpec((None, None, CMP_HIDDEN, HEAD_DIM), lambda g, b: (layer_j, which, 0, 0))
        out_block, out_dims = (None, None, n_chunks, HEAD_DIM), (NSA_GROUPS, batch, n_chunks, HEAD_DIM)
    return pl.pallas_call(
        functools.partial(_compress_body, feature_major=feature_major),
        grid=(NSA_GROUPS, batch),
        in_specs=[
            pl.BlockSpec((None, seq, HEAD_DIM), lambda g, b: (which * NSA_GROUPS + g, b, 0)),
            pl.BlockSpec((None, None, 1, CMP_BLOCK * HEAD_DIM), lambda g, b: (layer_j, which, 0, 0)),
            pl.BlockSpec((None, None, CMP_BLOCK * HEAD_DIM, CMP_HIDDEN), lambda g, b: (layer_j, which, 0, 0)),
            pl.BlockSpec((None, None, 1, CMP_HIDDEN), lambda g, b: (layer_j, which, 0, 0)),
            w2_spec,
        ],
        out_specs=pl.BlockSpec(out_block, lambda g, b: (g, b, 0, 0)),
        out_shape=jax.ShapeDtypeStruct(out_dims, BF16),
        compiler_params=_params(("parallel", "parallel")),
        name="compress",
    )(x, pe, w1, b1, w2)


def _lane_tile(x, n):
    return jnp.concatenate([x] * n, axis=1)


def _load_q_t(q_ref):
    return jnp.concatenate([q_ref[r * HEAD_DIM:(r + 1) * HEAD_DIM, :] for r in range(HEADS_PER_GROUP)], axis=1)


def _rope_feature_major(q, cos, sin):
    x1 = q[0:ROPE_HALF].astype(F32)
    x2 = q[ROPE_HALF:ROPE_DIM].astype(F32)
    r1 = (x1 * cos - x2 * sin).astype(BF16)
    r2 = (x2 * cos + x1 * sin).astype(BF16)
    return jnp.concatenate([r1, r2, q[ROPE_DIM:]], axis=0)


def _split3(x):
    hi = x.astype(BF16)
    r1 = x - hi.astype(F32)
    mid = r1.astype(BF16)
    lo = (r1 - mid.astype(F32)).astype(BF16)
    return hi, mid, lo


def _gate_row(gates_ref, branch, g, r):
    return gates_ref[pl.ds(branch * NSA_HEADS + g * HEADS_PER_GROUP + r, 1), :]


def _cmp_attn_body(q_ref, kc_ref, vc_ref, gates_ref, o_ref, sel_ref, *, tq, n_cmp):
    g = pl.program_id(1)
    qi = pl.program_id(2)
    t0 = qi * tq
    q = _load_q_t(q_ref)
    s = _dot(kc_ref[...], q)

    tpos = t0 + lax.broadcasted_iota(jnp.int32, (n_cmp, tq), 1)
    cend = lax.broadcasted_iota(jnp.int32, (n_cmp, tq), 0) * CMP_STRIDE + (CMP_BLOCK - 1)
    visible = cend <= tpos
    bias = _lane_tile(jnp.where(visible, 0.0, MASK_VALUE), HEADS_PER_GROUP)
    okf = _lane_tile(jnp.where(visible, 1.0, 0.0), HEADS_PER_GROUP)
    s = s + bias
    m = jnp.max(s, axis=0, keepdims=True)
    e = jnp.exp2(s - m) * okf
    inv = 1.0 / jnp.maximum(jnp.sum(e, axis=0, keepdims=True), 1e-30)
    p = e * inv
    o = _dot(vc_ref[...], p.astype(BF16))

    for r in range(HEADS_PER_GROUP):
        o_ref[r * HEAD_DIM:(r + 1) * HEAD_DIM, :] = _gate_row(gates_ref, 0, g, r) * o[:, r * tq:(r + 1) * tq]

    psum = p[:, 0:tq]
    for r in range(1, HEADS_PER_GROUP):
        psum = psum + p[:, r * tq:(r + 1) * tq]
    n_sel_rows = LANES // 2
    jrow = lax.broadcasted_iota(jnp.int32, (n_sel_rows, n_cmp), 0)
    ccol = lax.broadcasted_iota(jnp.int32, (n_sel_rows, n_cmp), 1)
    c_lo, c_hi = ccol * CMP_STRIDE, ccol * CMP_STRIDE + (CMP_BLOCK - 1)
    j_lo, j_hi = jrow * SEL_BLOCK, jrow * SEL_BLOCK + (SEL_BLOCK - 1)
    ov = jnp.maximum(jnp.minimum(c_hi, j_hi) - jnp.maximum(c_lo, j_lo) + 1, 0).astype(F32) / CMP_STRIDE
    ov = ov.astype(BF16)
    hi, mid, lo = _split3(psum)
    imp = _dot(ov, hi) + _dot(ov, mid) + _dot(ov, lo)

    blk = lax.broadcasted_iota(jnp.int32, (n_sel_rows, tq), 0)
    cur = (t0 + lax.broadcasted_iota(jnp.int32, (n_sel_rows, tq), 1)) // SEL_BLOCK
    forced = (blk == 0) | (blk == cur) | (blk == cur - 1)
    key = jnp.where(blk > cur, -1.0, imp)
    key = jnp.where(forced, 1e30, key)
    sel = jnp.zeros((n_sel_rows, tq), F32)
    for _ in range(SEL_TOP):
        mx = jnp.max(key, axis=0, keepdims=True)
        first = jnp.min(jnp.where(key == mx, blk, LANES), axis=0, keepdims=True)
        pick = blk == first
        sel = jnp.where(pick, 1.0, sel)
        key = jnp.where(pick, -2.0, key)
    sel = jnp.where(blk > cur, 0.0, sel)
    sel_ref[...] = jnp.concatenate([sel, jnp.zeros_like(sel)], axis=0).astype(BF16)


def _cmp_attn(feat_t, kc, vc_t, gates_t, batch, seq, tq):
    t = feat_t.shape[1]
    nq = seq // tq
    n_cmp = kc.shape[2]
    assert seq // SEL_BLOCK <= LANES // 2
    return pl.pallas_call(
        functools.partial(_cmp_attn_body, tq=tq, n_cmp=n_cmp),
        grid=(batch, NSA_GROUPS, nq),
        in_specs=[
            pl.BlockSpec((GROUP_WIDTH, tq), lambda b, g, i: (g, b * nq + i)),
            pl.BlockSpec((None, None, n_cmp, HEAD_DIM), lambda b, g, i: (g, b, 0, 0)),
            pl.BlockSpec((None, None, HEAD_DIM, n_cmp), lambda b, g, i: (g, b, 0, 0)),
            pl.BlockSpec((LANES, tq), lambda b, g, i: (0, b * nq + i)),
        ],
        out_specs=[
            pl.BlockSpec((GROUP_WIDTH, tq), lambda b, g, i: (g, b * nq + i)),
            pl.BlockSpec((None, LANES, tq), lambda b, g, i: (g, 0, b * nq + i)),
        ],
        out_shape=[
            jax.ShapeDtypeStruct((D_MODEL, t), F32),
            jax.ShapeDtypeStruct((NSA_GROUPS, LANES, t), BF16),
        ],
        compiler_params=_params(("parallel", "parallel", "parallel")),
        name="cmp_attn",
    )(feat_t, kc, vc_t, gates_t)


def _sel_attn_body(q_ref, k_ref, e_ref, v_ref, sel_ref, cos_ref, sin_ref, gates_ref, oin_ref, o_ref,
                   m_ref, l_ref, acc_ref, sa_ref, sb_ref, *, tq, tk):
    g = pl.program_id(1)
    qi = pl.program_id(2)
    t0 = qi * tq
    q = _rope_feature_major(_load_q_t(q_ref), _lane_tile(cos_ref[...], HEADS_PER_GROUP),
                            _lane_tile(sin_ref[...], HEADS_PER_GROUP))
    unselected = _lane_tile(sel_ref[...] - 1.0, HEADS_PER_GROUP)
    q_aug = jnp.concatenate([q, unselected.astype(BF16)], axis=0)

    m_ref[...] = jnp.full(m_ref.shape, MASK_VALUE, F32)
    l_ref[...] = jnp.zeros(l_ref.shape, F32)
    acc_ref[...] = jnp.zeros(acc_ref.shape, F32)

    def scores(ki):
        k0 = pl.multiple_of(ki * tk, tk)
        k_aug = jnp.concatenate([k_ref[pl.ds(k0, tk), :], e_ref[pl.ds(k0, tk), :]], axis=1)
        return _dot(k_aug, q_aug)

    def accumulate(s, ki, causal):
        k0 = pl.multiple_of(ki * tk, tk)
        if causal:
            kpos = k0 + lax.broadcasted_iota(jnp.int32, (tk, tq), 0)
            tpos = t0 + lax.broadcasted_iota(jnp.int32, (tk, tq), 1)
            s = s + _lane_tile(jnp.where(kpos <= tpos, 0.0, MASK_VALUE), HEADS_PER_GROUP)
        m_prev = m_ref[...]
        m_new = jnp.maximum(m_prev, jnp.max(s, axis=0, keepdims=True))
        alpha = jnp.exp2(m_prev - m_new)
        p = jnp.exp2(s - m_new)
        l_ref[...] = alpha * l_ref[...] + jnp.sum(p, axis=0, keepdims=True)
        acc_ref[...] = alpha * acc_ref[...] + _dot(v_ref[:, pl.ds(k0, tk)], p.astype(BF16))
        m_ref[...] = m_new

    n_before = t0 // tk
    sa_ref[...] = scores(0)

    def body(j, carry):
        sb_ref[...] = scores(2 * j + 1)
        accumulate(sa_ref[...], 2 * j, causal=False)
        sa_ref[...] = scores(2 * j + 2)
        accumulate(sb_ref[...], 2 * j + 1, causal=False)
        return carry

    lax.fori_loop(0, n_before // 2, body, 0)

    @pl.when(n_before % 2 == 1)
    def _():
        sb_ref[...] = scores(n_before)
        accumulate(sa_ref[...], n_before - 1, causal=False)
        accumulate(sb_ref[...], n_before, causal=True)

    @pl.when(n_before % 2 == 0)
    def _():
        accumulate(sa_ref[...], n_before, causal=True)

    o = acc_ref[...] * (1.0 / l_ref[...])
    for r in range(HEADS_PER_GROUP):
        rows = slice(r * HEAD_DIM, (r + 1) * HEAD_DIM)
        o_ref[rows, :] = oin_ref[rows, :] + _gate_row(gates_ref, 1, g, r) * o[:, r * tq:(r + 1) * tq]


def _sel_attn(feat_t, tok, selm, cos_t, sin_t, gates_t, oin, batch, seq, tq, tk):
    t = feat_t.shape[1]
    nq = seq // tq
    lanes = HEADS_PER_GROUP * tq
    v_row0 = NSA_HEADS
    assert tk % tq == 0 and seq % tk == 0
    key_block = jnp.arange(seq, dtype=jnp.int32)[:, None] // SEL_BLOCK
    block_of_key = jnp.where(key_block == jnp.arange(LANES, dtype=jnp.int32)[None, :], BLOCK_MASK_BIG, 0.0)
    return pl.pallas_call(
        functools.partial(_sel_attn_body, tq=tq, tk=tk),
        grid=(batch, NSA_GROUPS, nq),
        in_specs=[
            pl.BlockSpec((GROUP_WIDTH, tq), lambda b, g, i: (g, b * nq + i)),
            pl.BlockSpec((None, seq, HEAD_DIM), lambda b, g, i: (g, b, 0)),
            pl.BlockSpec((seq, LANES), lambda b, g, i: (0, 0)),
            pl.BlockSpec((HEAD_DIM, seq), lambda b, g, i: (v_row0 + g, b)),
            pl.BlockSpec((None, LANES, tq), lambda b, g, i: (g, 0, b * nq + i)),
            pl.BlockSpec((ROPE_HALF, tq), lambda b, g, i: (0, i)),
            pl.BlockSpec((ROPE_HALF, tq), lambda b, g, i: (0, i)),
            pl.BlockSpec((LANES, tq), lambda b, g, i: (0, b * nq + i)),
            pl.BlockSpec((GROUP_WIDTH, tq), lambda b, g, i: (g, b * nq + i)),
        ],
        out_specs=pl.BlockSpec((GROUP_WIDTH, tq), lambda b, g, i: (g, b * nq + i)),
        out_shape=jax.ShapeDtypeStruct((D_MODEL, t), F32),
        scratch_shapes=[
            pltpu.VMEM((1, lanes), F32),
            pltpu.VMEM((1, lanes), F32),
            pltpu.VMEM((HEAD_DIM, lanes), F32),
            pltpu.VMEM((tk, lanes), F32),
            pltpu.VMEM((tk, lanes), F32),
        ],
        compiler_params=_params(("parallel", "parallel", "parallel")),
        name="attn_sel",
    )(feat_t, tok, block_of_key.astype(BF16), feat_t, selm, cos_t, sin_t, gates_t, oin)


def _win_attn_body(q_ref, k_ref, v_ref, cos_ref, sin_ref, gates_ref, oin_ref, o_ref, *, tq):
    g = pl.program_id(1)
    qi = pl.program_id(2)
    t0 = qi * tq
    band = WINDOW + tq
    q = _rope_feature_major(_load_q_t(q_ref), _lane_tile(cos_ref[...], HEADS_PER_GROUP),
                            _lane_tile(sin_ref[...], HEADS_PER_GROUP))
    start = pl.multiple_of(jnp.maximum(t0 - WINDOW, 0), tq)
    s = _dot(k_ref[pl.ds(start, band), :], q)
    kpos = start + lax.broadcasted_iota(jnp.int32, (band, tq), 0)
    tpos = t0 + lax.broadcasted_iota(jnp.int32, (band, tq), 1)
    valid = (kpos <= tpos) & (kpos > tpos - WINDOW)
    s = s + _lane_tile(jnp.where(valid, 0.0, MASK_VALUE), HEADS_PER_GROUP)
    m = jnp.max(s, axis=0, keepdims=True)
    p = jnp.exp2(s - m)
    inv = 1.0 / jnp.sum(p, axis=0, keepdims=True)
    o = _dot(v_ref[:, pl.ds(start, band)], p.astype(BF16)) * inv
    for r in range(HEADS_PER_GROUP):
        rows = slice(r * HEAD_DIM, (r + 1) * HEAD_DIM)
        total = oin_ref[rows, :] + _gate_row(gates_ref, 2, g, r) * o[:, r * tq:(r + 1) * tq]
        o_ref[:, rows] = total.T.astype(BF16)


def _win_attn(feat_t, tok, cos_t, sin_t, gates_t, oin, batch, seq, tq):
    t = feat_t.shape[1]
    nq = seq // tq
    v_row0 = NSA_HEADS + NSA_GROUPS
    assert seq >= WINDOW + tq
    return pl.pallas_call(
        functools.partial(_win_attn_body, tq=tq),
        grid=(batch, NSA_GROUPS, nq),
        in_specs=[
            pl.BlockSpec((GROUP_WIDTH, tq), lambda b, g, i: (g, b * nq + i)),
            pl.BlockSpec((None, seq, HEAD_DIM), lambda b, g, i: (NSA_GROUPS + g, b, 0)),
            pl.BlockSpec((HEAD_DIM, seq), lambda b, g, i: (v_row0 + g, b)),
            pl.BlockSpec((ROPE_HALF, tq), lambda b, g, i: (0, i)),
            pl.BlockSpec((ROPE_HALF, tq), lambda b, g, i: (0, i)),
            pl.BlockSpec((LANES, tq), lambda b, g, i: (0, b * nq + i)),
            pl.BlockSpec((GROUP_WIDTH, tq), lambda b, g, i: (g, b * nq + i)),
        ],
        out_specs=pl.BlockSpec((tq, GROUP_WIDTH), lambda b, g, i: (b * nq + i, g)),
        out_shape=jax.ShapeDtypeStruct((t, D_MODEL), BF16),
        compiler_params=_params(("parallel", "parallel", "parallel")),
        name="attn_win",
    )(feat_t, tok, feat_t, cos_t, sin_t, gates_t, oin)


def _nsa_mixer(h, norm_rows, layer, layer_j, batch, seq, w_in, w_out, phi_pe, phi_w1, phi_b1, phi_w2):
    t = h.shape[0]
    w_gate = jnp.pad(w_in[layer_j, :, QKV_WIDTH:], ((0, 0), (0, LANES - N_GATES)))
    cos_tok, sin_tok, cos_t, sin_t = _rope_tables(seq)

    feat_t, gates_t, xn = _nsa_inproj_feat(h, norm_rows, layer, w_in, layer_j, w_gate, tm=1024)
    x_cmp, tok = _nsa_inproj_tok(xn, w_in, layer_j, cos_tok, sin_tok, seq, tm=min(1024, seq))

    pe = phi_pe.reshape(phi_pe.shape[0], 2, 1, CMP_BLOCK * HEAD_DIM)
    b1 = phi_b1.reshape(phi_b1.shape[0], 2, 1, CMP_HIDDEN)
    kc = _compress(x_cmp, 0, pe, phi_w1, b1, phi_w2, layer_j, batch, feature_major=False)
    vc_t = _compress(x_cmp, 1, pe, phi_w1, b1, phi_w2[layer_j, 1].T, layer_j, batch, feature_major=True)

    o1, selm = _cmp_attn(feat_t, kc, vc_t, gates_t, batch, seq, tq=512)
    o2 = _sel_attn(feat_t, tok, selm, cos_t, sin_t, gates_t, o1, batch, seq, tq=256, tk=512)
    o3 = _win_attn(feat_t, tok, cos_t, sin_t, gates_t, o2, batch, seq, tq=256)
    return _outproj(o3, w_out, layer_j, h, norm_rows, layer, 3, tm=min(1024, t), tn=512)


def _conv_inproj_body(h_ref, halo_ref, g_ref, wb_ref, wc_ref, wu_ref, cw_ref, o_ref, xn_ref, *, tiles_per_seq):
    i = pl.program_id(0)
    j = pl.program_id(1)

    @pl.when(j == 0)
    def _():
        keep = jnp.where(i % tiles_per_seq == 0, 0.0, 1.0)
        xn_ref[0:CONV_HALO] = (_rms(halo_ref[...], g_ref[...]) * keep).astype(BF16)
        xn_ref[CONV_HALO:] = _rms(h_ref[...], g_ref[...]).astype(BF16)

    x = xn_ref[...]
    bg = _dot(x, wb_ref[...].astype(BF16))
    z = _dot(x, wc_ref[...].astype(BF16)) * _dot(x, wu_ref[...].astype(BF16))
    cw = cw_ref[...]
    conv = cw[2:3] * z + cw[1:2] * pltpu.roll(z, 1, axis=0) + cw[0:1] * pltpu.roll(z, 2, axis=0)
    o_ref[...] = (bg * conv)[CONV_HALO:].astype(BF16)


def _conv_inproj(h, norm_rows, layer, w_in, conv_w, layer_j, seq, tm, tn):
    t = h.shape[0]
    nj = D_MODEL // tn
    halo_blocks = tm // CONV_HALO
    return pl.pallas_call(
        functools.partial(_conv_inproj_body, tiles_per_seq=seq // tm),
        grid=(t // tm, nj),
        in_specs=[
            pl.BlockSpec((tm, D_MODEL), lambda i, j: (i, 0), pipeline_mode=pl.Buffered(1)),
            pl.BlockSpec((CONV_HALO, D_MODEL), lambda i, j: (jnp.maximum(i * halo_blocks - 1, 0), 0)),
            _norm_row_spec(layer, 2),
            pl.BlockSpec((None, D_MODEL, tn), lambda i, j: (layer_j, 0, j)),
            pl.BlockSpec((None, D_MODEL, tn), lambda i, j: (layer_j, 0, nj + j)),
            pl.BlockSpec((None, D_MODEL, tn), lambda i, j: (layer_j, 0, 2 * nj + j)),
            pl.BlockSpec((None, CONV_WIDTH, tn), lambda i, j: (layer_j, 0, j)),
        ],
        out_specs=pl.BlockSpec((tm, tn), lambda i, j: (i, j)),
        out_shape=jax.ShapeDtypeStruct((t, D_MODEL), BF16),
        scratch_shapes=[pltpu.VMEM((tm + CONV_HALO, D_MODEL), BF16)],
        compiler_params=_params(("parallel", "arbitrary")),
        name="conv_inproj",
    )(h, h, norm_rows, w_in, w_in, w_in, conv_w)


def _sgu_inproj_body(h_ref, g_ref, w_ref, lng_ref, lnb_ref, ws_ref, bs_ref, o_ref, xn_ref, z_ref, *, tm, tn):
    j = pl.program_id(1)

    @pl.when(j == 0)
    def _():
        xn_ref[...] = _rms(h_ref[...], g_ref[...]).astype(BF16)

    col = pl.multiple_of(j * tn, tn)
    z_ref[:, pl.ds(col, tn)] = jax.nn.gelu(_dot(xn_ref[...], w_ref[...].astype(BF16)))

    @pl.when(j == pl.num_programs(1) - 1)
    def _():
        v = z_ref[:, D_MODEL:]
        mu = jnp.mean(v, axis=-1, keepdims=True)
        var = jnp.mean(jnp.square(v - mu), axis=-1, keepdims=True)
        vn = ((v - mu) * lax.rsqrt(var + EPS) * lng_ref[...] + lnb_ref[...]).astype(BF16)
        row = lax.broadcasted_iota(jnp.int32, (SGU_CHUNK, SGU_CHUNK), 0)
        colm = lax.broadcasted_iota(jnp.int32, (SGU_CHUNK, SGU_CHUNK), 1)
        bs = bs_ref[...]
        for grp in range(SGU_GROUPS):
            ws = jnp.where(colm <= row, ws_ref[grp], 0.0).astype(BF16)
            cols = slice(grp * SGU_GROUP_DIM, (grp + 1) * SGU_GROUP_DIM)
            for c in range(tm // SGU_CHUNK):
                rws = slice(c * SGU_CHUNK, (c + 1) * SGU_CHUNK)
                sv = _dot(ws, vn[rws, cols]) + bs[:, grp:grp + 1]
                o_ref[rws, cols] = (z_ref[rws, cols] * sv).astype(BF16)


def _sgu_inproj(h, norm_rows, layer, w_in, ln_g, ln_b, w_s, b_s_t, layer_j, tm, tn):
    t = h.shape[0]
    return pl.pallas_call(
        functools.partial(_sgu_inproj_body, tm=tm, tn=tn),
        grid=(t // tm, 2 * D_MODEL // tn),
        in_specs=[
            pl.BlockSpec((tm, D_MODEL), lambda i, j: (i, 0)),
            _norm_row_spec(layer, 2),
            pl.BlockSpec((None, D_MODEL, tn), lambda i, j: (layer_j, 0, j)),
            pl.BlockSpec((None, 1, D_MODEL), lambda i, j: (layer_j, 0, 0)),
            pl.BlockSpec((None, 1, D_MODEL), lambda i, j: (layer_j, 0, 0)),
            pl.BlockSpec((None, SGU_GROUPS, SGU_CHUNK, SGU_CHUNK), lambda i, j: (layer_j, 0, 0, 0)),
            pl.BlockSpec((None, SGU_CHUNK, SGU_GROUPS), lambda i, j: (layer_j, 0, 0)),
        ],
        out_specs=pl.BlockSpec((tm, D_MODEL), lambda i, j: (i, 0)),
        out_shape=jax.ShapeDtypeStruct((t, D_MODEL), BF16),
        scratch_shapes=[pltpu.VMEM((tm, D_MODEL), BF16), pltpu.VMEM((tm, 2 * D_MODEL), F32)],
        compiler_params=_params(("parallel", "arbitrary")),
        name="sgu_inproj",
    )(h, norm_rows, w_in, ln_g, ln_b, w_s, b_s_t)


def kernel(x, p, norm_g, ffn1_wg, ffn1_wu, ffn1_wd, ffn2_wg, ffn2_wu, ffn2_wd, ple_wg, ple_wp, nsa_w_in, nsa_w_out, nsa_phi_pe, nsa_phi_w1, nsa_phi_b1, nsa_phi_w2, conv_w_in, conv_w, conv_w_out, sgu_w_in, sgu_ln_g, sgu_ln_b, sgu_w_s, sgu_b_s, sgu_w_out):
    batch, seq, d = x.shape
    depth = p.shape[0]
    t = batch * seq
    assert d == D_MODEL and seq % 512 == 0
    h = x.reshape(t, d)
    p2 = p.reshape(depth, t, PLE_DIM)
    norm_rows = norm_g.reshape(depth * N_NORMS, 1, d)
    sgu_ln_g3 = sgu_ln_g.reshape(-1, 1, d)
    sgu_ln_b3 = sgu_ln_b.reshape(-1, 1, d)
    sgu_b_s_t = jnp.swapaxes(sgu_b_s, 1, 2)
    tm_big = min(1024, t)

    for layer in range(depth):
        layer_j = layer // N_MIXERS
        h = _ffn(h, norm_rows, layer, 0, 1, ffn1_wg, ffn1_wu, ffn1_wd, tm=tm_big, tf=256)
        kind = layer % N_MIXERS
        if kind == 0:
            h = _nsa_mixer(h, norm_rows, layer, layer_j, batch, seq, nsa_w_in, nsa_w_out, nsa_phi_pe,
                           nsa_phi_w1, nsa_phi_b1, nsa_phi_w2)
        elif kind == 1:
            a = _conv_inproj(h, norm_rows, layer, conv_w_in, conv_w, layer_j, seq, tm=min(1024, seq), tn=512)
            h = _outproj(a, conv_w_out, layer_j, h, norm_rows, layer, 3, tm=tm_big, tn=512)
        else:
            a = _sgu_inproj(h, norm_rows, layer, sgu_w_in, sgu_ln_g3, sgu_ln_b3, sgu_w_s, sgu_b_s_t, layer_j,
                            tm=512, tn=512)
            h = _outproj(a, sgu_w_out, layer_j, h, norm_rows, layer, 3, tm=tm_big, tn=512)
        h = _ffn(h, norm_rows, layer, 4, 5, ffn2_wg, ffn2_wu, ffn2_wd, tm=tm_big, tf=256)
        h = _ple(h, p2, norm_rows, layer, ple_wg, ple_wp, tm=tm_big, tn=512)
    return h.reshape(batch, seq, d)
```

```python
import functools
import math

import jax
import jax.numpy as jnp
from jax import lax
from jax.experimental import pallas as pl
from jax.experimental.pallas import tpu as pltpu

F32 = jnp.float32
BF16 = jnp.bfloat16

EPS = 1e-6
LANES = 128
D_MODEL = 2048
D_FF = 5632
PLE_DIM = 256
N_NORMS = 8
N_MIXERS = 3
NSA_HEADS = 16
NSA_GROUPS = 4
HEADS_PER_GROUP = NSA_HEADS // NSA_GROUPS
HEAD_DIM = D_MODEL // NSA_HEADS
GROUP_WIDTH = HEADS_PER_GROUP * HEAD_DIM
KV_WIDTH = NSA_GROUPS * HEAD_DIM
ROPE_DIM = HEAD_DIM // 4
ROPE_HALF = ROPE_DIM // 2
ROPE_THETA = 500000.0
CMP_BLOCK = 32
CMP_STRIDE = 16
CMP_HIDDEN = 256
SEL_BLOCK = 64
SEL_TOP = 16
WINDOW = 512
QKV_WIDTH = NSA_HEADS * HEAD_DIM + 6 * KV_WIDTH
N_GATES = 3 * NSA_HEADS
SGU_CHUNK = 128
SGU_GROUPS = 8
SGU_GROUP_DIM = D_MODEL // SGU_GROUPS
CONV_WIDTH = 3
CONV_HALO = 16

MASK_VALUE = -1e30
BLOCK_MASK_BIG = 2.0 ** 100
SCORE_SCALE = HEAD_DIM ** -0.5
EXP2_SCALE = SCORE_SCALE * math.log2(math.e)
VMEM_LIMIT = 60 * 1024 * 1024


def _params(semantics):
    return pltpu.CompilerParams(dimension_semantics=semantics, vmem_limit_bytes=VMEM_LIMIT)


def _rms(x, g):
    return x * lax.rsqrt(jnp.mean(x * x, axis=-1, keepdims=True) + EPS) * g


def _dot(a, b):
    return jnp.dot(a, b, preferred_element_type=F32)


def _dot_nt(a, b):
    return lax.dot_general(a, b, (((1,), (1,)), ((), ())), preferred_element_type=F32)


def _norm_row_spec(layer, k):
    idx = layer * N_NORMS + k
    return pl.BlockSpec((None, 1, D_MODEL), lambda i, j: (idx, 0, 0))


def _ffn_body(h_ref, gpre_ref, gpost_ref, wg_ref, wu_ref, wd_ref, o_ref, xn_ref):
    f = pl.program_id(1)

    @pl.when(f == 0)
    def _():
        xn_ref[...] = _rms(h_ref[...], gpre_ref[...]).astype(BF16)
        o_ref[...] = jnp.zeros_like(o_ref)

    x = xn_ref[...]
    gate = _dot(x, wg_ref[...].astype(BF16))
    up = _dot(x, wu_ref[...].astype(BF16))
    act = (gate * jax.nn.sigmoid(gate) * up).astype(BF16)
    o_ref[...] += _dot(act, wd_ref[...].astype(BF16))

    @pl.when(f == pl.num_programs(1) - 1)
    def _():
        o_ref[...] = h_ref[...] + 0.5 * _rms(o_ref[...], gpost_ref[...])


def _ffn(h, norm_rows, layer, k_pre, k_post, wg, wu, wd, tm, tf):
    t = h.shape[0]
    return pl.pallas_call(
        _ffn_body,
        grid=(t // tm, D_FF // tf),
        in_specs=[
            pl.BlockSpec((tm, D_MODEL), lambda i, f: (i, 0)),
            _norm_row_spec(layer, k_pre),
            _norm_row_spec(layer, k_post),
            pl.BlockSpec((None, D_MODEL, tf), lambda i, f: (layer, 0, f)),
            pl.BlockSpec((None, D_MODEL, tf), lambda i, f: (layer, 0, f)),
            pl.BlockSpec((None, tf, D_MODEL), lambda i, f: (layer, f, 0)),
        ],
        out_specs=pl.BlockSpec((tm, D_MODEL), lambda i, f: (i, 0)),
        out_shape=jax.ShapeDtypeStruct((t, D_MODEL), F32),
        scratch_shapes=[pltpu.VMEM((tm, D_MODEL), BF16)],
        compiler_params=_params(("parallel", "arbitrary")),
        name="ffn",
    )(h, norm_rows, norm_rows, wg, wu, wd)


def _outproj_body(a_ref, w_ref, h_ref, g_ref, o_ref, *, tn):
    j = pl.program_id(1)
    col = pl.multiple_of(j * tn, tn)
    o_ref[:, pl.ds(col, tn)] = _dot(a_ref[...], w_ref[...].astype(BF16))

    @pl.when(j == pl.num_programs(1) - 1)
    def _():
        o_ref[...] = h_ref[...] + _rms(o_ref[...], g_ref[...])


def _outproj(a, w, w_layer, h, norm_rows, layer, k_norm, tm, tn):
    t, kdim = a.shape
    return pl.pallas_call(
        functools.partial(_outproj_body, tn=tn),
        grid=(t // tm, D_MODEL // tn),
        in_specs=[
            pl.BlockSpec((tm, kdim), lambda i, j: (i, 0)),
            pl.BlockSpec((None, kdim, tn), lambda i, j: (w_layer, 0, j)),
            pl.BlockSpec((tm, D_MODEL), lambda i, j: (i, 0)),
            _norm_row_spec(layer, k_norm),
        ],
        out_specs=pl.BlockSpec((tm, D_MODEL), lambda i, j: (i, 0)),
        out_shape=jax.ShapeDtypeStruct((t, D_MODEL), F32),
        compiler_params=_params(("parallel", "arbitrary")),
        name="outproj",
    )(a, w, h, norm_rows)


def _ple_body(h_ref, p_ref, gpre_ref, gpost_ref, wg_ref, wp_ref, o_ref, xn_ref, pb_ref, *, tn):
    j = pl.program_id(1)

    @pl.when(j == 0)
    def _():
        xn_ref[...] = _rms(h_ref[...], gpre_ref[...]).astype(BF16)
        pb_ref[...] = p_ref[...].astype(BF16)

    gate = jax.nn.sigmoid(_dot(xn_ref[...], wg_ref[...].astype(BF16)))
    emb = _dot(pb_ref[...], wp_ref[...].astype(BF16))
    col = pl.multiple_of(j * tn, tn)
    o_ref[:, pl.ds(col, tn)] = gate * emb

    @pl.when(j == pl.num_programs(1) - 1)
    def _():
        o_ref[...] = h_ref[...] + _rms(o_ref[...], gpost_ref[...])


def _ple(h, p, norm_rows, layer, wg, wp, tm, tn):
    t = h.shape[0]
    return pl.pallas_call(
        functools.partial(_ple_body, tn=tn),
        grid=(t // tm, D_MODEL // tn),
        in_specs=[
            pl.BlockSpec((tm, D_MODEL), lambda i, j: (i, 0)),
            pl.BlockSpec((None, tm, PLE_DIM), lambda i, j: (layer, i, 0)),
            _norm_row_spec(layer, 6),
            _norm_row_spec(layer, 7),
            pl.BlockSpec((None, D_MODEL, tn), lambda i, j: (layer, 0, j)),
            pl.BlockSpec((None, PLE_DIM, tn), lambda i, j: (layer, 0, j)),
        ],
        out_specs=pl.BlockSpec((tm, D_MODEL), lambda i, j: (i, 0)),
        out_shape=jax.ShapeDtypeStruct((t, D_MODEL), F32),
        scratch_shapes=[pltpu.VMEM((tm, D_MODEL), BF16), pltpu.VMEM((tm, PLE_DIM), BF16)],
        compiler_params=_params(("parallel", "arbitrary")),
        name="ple",
    )(h, p, norm_rows, norm_rows, wg, wp)


def _nsa_inproj_feat_body(h_ref, g_ref, w_ref, wg_ref, feat_ref, gates_ref, xn_ref, *, n_q_tiles):
    j = pl.program_id(1)

    @pl.when(j == 0)
    def _():
        xn = _rms(h_ref[...], g_ref[...]).astype(BF16)
        xn_ref[...] = xn
        gates_ref[...] = jax.nn.sigmoid(_dot(xn, wg_ref[...].astype(BF16))).T

    row_scale = jnp.where(j < n_q_tiles, EXP2_SCALE, 1.0)
    feat_ref[...] = (_dot(xn_ref[...], w_ref[...].astype(BF16)) * row_scale).T.astype(BF16)


def _nsa_inproj_feat(h, norm_rows, layer, w_in, w_layer, w_gate, tm):
    t = h.shape[0]
    tn = KV_WIDTH
    n_q_tiles = NSA_HEADS * HEAD_DIM // tn
    n_feat = NSA_HEADS * HEAD_DIM + 2 * KV_WIDTH
    return pl.pallas_call(
        functools.partial(_nsa_inproj_feat_body, n_q_tiles=n_q_tiles),
        grid=(t // tm, n_feat // tn),
        in_specs=[
            pl.BlockSpec((tm, D_MODEL), lambda i, j: (i, 0)),
            _norm_row_spec(layer, 2),
            pl.BlockSpec((None, D_MODEL, tn),
                         lambda i, j: (w_layer, 0, j + 3 * (j // n_q_tiles) + j // (n_q_tiles + 1))),
            pl.BlockSpec((D_MODEL, LANES), lambda i, j: (0, 0)),
        ],
        out_specs=[
            pl.BlockSpec((tn, tm), lambda i, j: (j, i)),
            pl.BlockSpec((LANES, tm), lambda i, j: (0, i)),
            pl.BlockSpec((tm, D_MODEL), lambda i, j: (i, 0)),
        ],
        out_shape=[
            jax.ShapeDtypeStruct((n_feat, t), BF16),
            jax.ShapeDtypeStruct((LANES, t), F32),
            jax.ShapeDtypeStruct((t, D_MODEL), BF16),
        ],
        compiler_params=_params(("parallel", "arbitrary")),
        name="nsa_inproj_feat",
    )(h, norm_rows, w_in, w_gate)


def _rope_token_major(x, cos, sin):
    lane = lax.broadcasted_iota(jnp.int32, x.shape, 1)
    partner = jnp.where(lane < ROPE_HALF, pltpu.roll(x, LANES - ROPE_HALF, axis=1),
                        pltpu.roll(x, ROPE_HALF, axis=1))
    return x * cos + partner * sin


def _nsa_inproj_tok_body(xn_ref, wc_ref, wk_ref, cos_ref, sin_ref, oc_ref, ok_ref):
    xn = xn_ref[...]
    res_c = _dot(xn, wc_ref[...].astype(BF16))
    res_k = _dot(xn, wk_ref[...].astype(BF16))
    cos, sin = cos_ref[...], sin_ref[...]
    for c in range(NSA_GROUPS):
        cols = slice(c * LANES, (c + 1) * LANES)
        oc_ref[c] = res_c[:, cols]
        ok_ref[c] = _rope_token_major(res_k[:, cols], cos, sin).astype(BF16)


def _nsa_inproj_tok(xn, w_in, w_layer, cos_tok, sin_tok, seq, tm):
    t = xn.shape[0]
    per_seq = seq // tm
    q_blocks = NSA_HEADS * HEAD_DIM // KV_WIDTH
    slab_spec = pl.BlockSpec((NSA_GROUPS, tm, LANES), lambda i, j: (j, i, 0))
    return pl.pallas_call(
        _nsa_inproj_tok_body,
        grid=(t // tm, 2),
        in_specs=[
            pl.BlockSpec((tm, D_MODEL), lambda i, j: (i, 0)),
            pl.BlockSpec((None, D_MODEL, KV_WIDTH), lambda i, j: (w_layer, 0, q_blocks + j)),
            pl.BlockSpec((None, D_MODEL, KV_WIDTH), lambda i, j: (w_layer, 0, q_blocks + 2 + 2 * j)),
            pl.BlockSpec((tm, LANES), lambda i, j: (i % per_seq, 0)),
            pl.BlockSpec((tm, LANES), lambda i, j: (i % per_seq, 0)),
        ],
        out_specs=[slab_spec, slab_spec],
        out_shape=[
            jax.ShapeDtypeStruct((2 * NSA_GROUPS, t, LANES), F32),
            jax.ShapeDtypeStruct((2 * NSA_GROUPS, t, LANES), BF16),
        ],
        compiler_params=_params(("parallel", "arbitrary")),
        name="nsa_inproj_tok",
    )(xn, w_in, w_in, cos_tok, sin_tok)


def _rope_tables(seq):
    inv_freq = jnp.power(jnp.float32(ROPE_THETA), -jnp.arange(0, ROPE_DIM, 2, dtype=F32) / ROPE_DIM)
    ang = jnp.arange(seq, dtype=F32)[:, None] * inv_freq[None, :]
    cos, sin = jnp.cos(ang), jnp.sin(ang)
    cos_tok = jnp.concatenate([cos, cos, jnp.ones((seq, LANES - ROPE_DIM), F32)], axis=1)
    sin_tok = jnp.concatenate([-sin, sin, jnp.zeros((seq, LANES - ROPE_DIM), F32)], axis=1)
    return cos_tok, sin_tok, cos.T, sin.T


def _compress_body(x_ref, pe_ref, w1_ref, b1_ref, w2_ref, o_ref, *, feature_major):
    n_chunks = x_ref.shape[0] // CMP_STRIDE
    w1 = w1_ref[...].astype(BF16)
    both = jnp.zeros((n_chunks, 2 * CMP_HIDDEN), F32)
    for l in range(CMP_STRIDE):
        x_l = x_ref[pl.ds(l, n_chunks, stride=CMP_STRIDE), :].astype(BF16)
        lo, hi = l * HEAD_DIM, (CMP_STRIDE + l) * HEAD_DIM
        w_l = jnp.concatenate([w1[lo:lo + HEAD_DIM], w1[hi:hi + HEAD_DIM]], axis=1)
        both = both + _dot(x_l, w_l)
    first = both[:, :CMP_HIDDEN]
    second = pltpu.roll(both[:, CMP_HIDDEN:], n_chunks - 1, axis=0)
    pe = jnp.broadcast_to(pe_ref[...], (8, CMP_BLOCK * HEAD_DIM)).astype(BF16)
    const = _dot(pe, w1)[0:1]
    hid = first + second + const + b1_ref[...]
    act = (hid * jax.nn.sigmoid(hid)).astype(BF16)
    if feature_major:
        o_ref[...] = _dot_nt(w2_ref[...].astype(BF16), act).astype(BF16)
    else:
        o_ref[...] = _dot(act, w2_ref[...].astype(BF16)).astype(BF16)


def _compress(x, which, pe, w1, b1, w2, layer_j, batch, feature_major):
    seq = x.shape[1] // batch
    n_chunks = seq // CMP_STRIDE
    if feature_major:
        w2_spec = pl.BlockSpec((HEAD_DIM, CMP_HIDDEN), lambda g, b: (0, 0))
        out_block, out_dims = (None, None, HEAD_DIM, n_chunks), (NSA_GROUPS, batch, HEAD_DIM, n_chunks)
    else:
        w2_spec = pl.BlockSpec((None, None, CMP_HIDDEN, HEAD_DIM), lambda g, b: (layer_j, which, 0, 0))
        out_block, out_dims = (None, None, n_chunks, HEAD_DIM), (NSA_GROUPS, batch, n_chunks, HEAD_DIM)
    return pl.pallas_call(
        functools.partial(_compress_body, feature_major=feature_major),
        grid=(NSA_GROUPS, batch),
        in_specs=[
            pl.BlockSpec((None, seq, HEAD_DIM), lambda g, b: (which * NSA_GROUPS + g, b, 0)),
            pl.BlockSpec((None, None, 1, CMP_BLOCK * HEAD_DIM), lambda g, b: (layer_j, which, 0, 0)),
            pl.BlockSpec((None, None, CMP_BLOCK * HEAD_DIM, CMP_HIDDEN), lambda g, b: (layer_j, which, 0, 0)),
            pl.BlockSpec((None, None, 1, CMP_HIDDEN), lambda g, b: (layer_j, which, 0, 0)),
            w2_spec,
        ],
        out_specs=pl.BlockSpec(out_block, lambda g, b: (g, b, 0, 0)),
        out_shape=jax.ShapeDtypeStruct(out_dims, BF16),
        compiler_params=_params(("parallel", "parallel")),
        name="compress",
    )(x, pe, w1, b1, w2)


def _lane_tile(x, n):
    return jnp.concatenate([x] * n, axis=1)


def _load_q_t(q_ref, lo=0, n=None):
    n = q_ref.shape[1] if n is None else n
    return jnp.concatenate([q_ref[r * HEAD_DIM:(r + 1) * HEAD_DIM, lo:lo + n] for r in range(HEADS_PER_GROUP)],
                           axis=1)


def _rope_feature_major(q, cos, sin):
    x1 = q[0:ROPE_HALF].astype(F32)
    x2 = q[ROPE_HALF:ROPE_DIM].astype(F32)
    r1 = (x1 * cos - x2 * sin).astype(BF16)
    r2 = (x2 * cos + x1 * sin).astype(BF16)
    return jnp.concatenate([r1, r2, q[ROPE_DIM:]], axis=0)


def _split3(x):
    hi = x.astype(BF16)
    r1 = x - hi.astype(F32)
    mid = r1.astype(BF16)
    lo = (r1 - mid.astype(F32)).astype(BF16)
    return hi, mid, lo


def _gate_row(gates_ref, branch, g, r):
    return gates_ref[pl.ds(branch * NSA_HEADS + g * HEADS_PER_GROUP + r, 1), :]


def _cmp_attn_body(q_ref, kc_ref, vc_ref, gates_ref, o_ref, sel_ref, *score_refs, tq, n_cmp):
    n_sub = len(score_refs)
    ts = tq // n_sub
    kc = kc_ref[...]
    for u, s_ref in enumerate(score_refs):
        s_ref[...] = _dot(kc, _load_q_t(q_ref, u * ts, ts))
    for u, s_ref in enumerate(score_refs):
        _cmp_attn_subtile(s_ref[...], vc_ref, gates_ref, o_ref, sel_ref, pl.program_id(2) * tq + u * ts, u * ts,
                          ts, n_cmp)


def _cmp_attn_subtile(s, vc_ref, gates_ref, o_ref, sel_ref, t0, lane0, tq, n_cmp):
    g = pl.program_id(1)
    out_lanes = slice(lane0, lane0 + tq)

    tpos = t0 + lax.broadcasted_iota(jnp.int32, (n_cmp, tq), 1)
    cend = lax.broadcasted_iota(jnp.int32, (n_cmp, tq), 0) * CMP_STRIDE + (CMP_BLOCK - 1)
    visible = cend <= tpos
    bias = _lane_tile(jnp.where(visible, 0.0, MASK_VALUE), HEADS_PER_GROUP)
    okf = _lane_tile(jnp.where(visible, 1.0, 0.0), HEADS_PER_GROUP)
    s = s + bias
    m = jnp.max(s, axis=0, keepdims=True)
    e = jnp.exp2(s - m) * okf
    inv = 1.0 / jnp.maximum(jnp.sum(e, axis=0, keepdims=True), 1e-30)
    p = e * inv
    o = _dot(vc_ref[...], p.astype(BF16))

    for r in range(HEADS_PER_GROUP):
        gate = _gate_row(gates_ref, 0, g, r)[:, out_lanes]
        o_ref[r * HEAD_DIM:(r + 1) * HEAD_DIM, out_lanes] = gate * o[:, r * tq:(r + 1) * tq]

    psum = p[:, 0:tq]
    for r in range(1, HEADS_PER_GROUP):
        psum = psum + p[:, r * tq:(r + 1) * tq]
    n_sel_rows = LANES // 2
    jrow = lax.broadcasted_iota(jnp.int32, (n_sel_rows, n_cmp), 0)
    ccol = lax.broadcasted_iota(jnp.int32, (n_sel_rows, n_cmp), 1)
    c_lo, c_hi = ccol * CMP_STRIDE, ccol * CMP_STRIDE + (CMP_BLOCK - 1)
    j_lo, j_hi = jrow * SEL_BLOCK, jrow * SEL_BLOCK + (SEL_BLOCK - 1)
    ov = jnp.maximum(jnp.minimum(c_hi, j_hi) - jnp.maximum(c_lo, j_lo) + 1, 0).astype(F32) / CMP_STRIDE
    ov = ov.astype(BF16)
    hi, mid, lo = _split3(psum)
    imp = _dot(ov, hi) + _dot(ov, mid) + _dot(ov, lo)

    blk = lax.broadcasted_iota(jnp.int32, (n_sel_rows, tq), 0)
    cur = (t0 + lax.broadcasted_iota(jnp.int32, (n_sel_rows, tq), 1)) // SEL_BLOCK
    forced = (blk == 0) | (blk == cur) | (blk == cur - 1)
    key = jnp.where(blk > cur, -1.0, imp)
    key = jnp.where(forced, 1e30, key)
    sel = jnp.zeros((n_sel_rows, tq), F32)
    for _ in range(SEL_TOP):
        mx = jnp.max(key, axis=0, keepdims=True)
        first = jnp.min(jnp.where(key == mx, blk, LANES), axis=0, keepdims=True)
        pick = blk == first
        sel = jnp.where(pick, 1.0, sel)
        key = jnp.where(pick, -2.0, key)
    sel = jnp.where(blk > cur, 0.0, sel)
    sel_ref[:, out_lanes] = jnp.concatenate([sel, jnp.zeros_like(sel)], axis=0).astype(BF16)


def _cmp_attn(feat_t, kc, vc_t, gates_t, batch, seq, tq, ts):
    t = feat_t.shape[1]
    nq = seq // tq
    n_cmp = kc.shape[2]
    assert seq // SEL_BLOCK <= LANES // 2
    return pl.pallas_call(
        functools.partial(_cmp_attn_body, tq=tq, n_cmp=n_cmp),
        grid=(batch, NSA_GROUPS, nq),
        in_specs=[
            pl.BlockSpec((GROUP_WIDTH, tq), lambda b, g, i: (g, b * nq + i)),
            pl.BlockSpec((None, None, n_cmp, HEAD_DIM), lambda b, g, i: (g, b, 0, 0)),
            pl.BlockSpec((None, None, HEAD_DIM, n_cmp), lambda b, g, i: (g, b, 0, 0)),
            pl.BlockSpec((LANES, tq), lambda b, g, i: (0, b * nq + i)),
        ],
        out_specs=[
            pl.BlockSpec((GROUP_WIDTH, tq), lambda b, g, i: (g, b * nq + i)),
            pl.BlockSpec((None, LANES, tq), lambda b, g, i: (g, 0, b * nq + i)),
        ],
        out_shape=[
            jax.ShapeDtypeStruct((D_MODEL, t), F32),
            jax.ShapeDtypeStruct((NSA_GROUPS, LANES, t), BF16),
        ],
        scratch_shapes=[pltpu.VMEM((n_cmp, HEADS_PER_GROUP * ts), F32) for _ in range(tq // ts)],
        compiler_params=_params(("parallel", "parallel", "parallel")),
        name="cmp_attn",
    )(feat_t, kc, vc_t, gates_t)


def _sel_attn_body(q_ref, k_ref, e_ref, v_ref, sel_ref, cos_ref, sin_ref, gates_ref, oin_ref, o_ref,
                   m_ref, l_ref, acc_ref, sa_ref, sb_ref, *, tq, tk):
    g = pl.program_id(1)
    qi = pl.program_id(2)
    t0 = qi * tq
    q = _rope_feature_major(_load_q_t(q_ref), _lane_tile(cos_ref[...], HEADS_PER_GROUP),
                            _lane_tile(sin_ref[...], HEADS_PER_GROUP))
    unselected = _lane_tile(sel_ref[...] - 1.0, HEADS_PER_GROUP)
    q_aug = jnp.concatenate([q, unselected.astype(BF16)], axis=0)

    m_ref[...] = jnp.full(m_ref.shape, MASK_VALUE, F32)
    l_ref[...] = jnp.zeros(l_ref.shape, F32)
    acc_ref[...] = jnp.zeros(acc_ref.shape, F32)

    def scores(ki):
        k0 = pl.multiple_of(ki * tk, tk)
        k_aug = jnp.concatenate([k_ref[pl.ds(k0, tk), :], e_ref[pl.ds(k0, tk), :]], axis=1)
        return _dot(k_aug, q_aug)

    def accumulate(s, ki, causal):
        k0 = pl.multiple_of(ki * tk, tk)
        if causal:
            kpos = k0 + lax.broadcasted_iota(jnp.int32, (tk, tq), 0)
            tpos = t0 + lax.broadcasted_iota(jnp.int32, (tk, tq), 1)
            s = s + _lane_tile(jnp.where(kpos <= tpos, 0.0, MASK_VALUE), HEADS_PER_GROUP)
        m_prev = m_ref[...]
        m_new = jnp.maximum(m_prev, jnp.max(s, axis=0, keepdims=True))
        alpha = jnp.exp2(m_prev - m_new)
        p = jnp.exp2(s - m_new)
        l_ref[...] = alpha * l_ref[...] + jnp.sum(p, axis=0, keepdims=True)
        acc_ref[...] = alpha * acc_ref[...] + _dot(v_ref[:, pl.ds(k0, tk)], p.astype(BF16))
        m_ref[...] = m_new

    n_before = t0 // tk
    sa_ref[...] = scores(0)

    def body(j, carry):
        sb_ref[...] = scores(2 * j + 1)
        accumulate(sa_ref[...], 2 * j, causal=False)
        sa_ref[...] = scores(2 * j + 2)
        accumulate(sb_ref[...], 2 * j + 1, causal=False)
        return carry

    lax.fori_loop(0, n_before // 2, body, 0)

    @pl.when(n_before % 2 == 1)
    def _():
        sb_ref[...] = scores(n_before)
        accumulate(sa_ref[...], n_before - 1, causal=False)
        accumulate(sb_ref[...], n_before, causal=True)

    @pl.when(n_before % 2 == 0)
    def _():
        accumulate(sa_ref[...], n_before, causal=True)

    o = acc_ref[...] * (1.0 / l_ref[...])
    for r in range(HEADS_PER_GROUP):
        rows = slice(r * HEAD_DIM, (r + 1) * HEAD_DIM)
        o_ref[rows, :] = oin_ref[rows, :] + _gate_row(gates_ref, 1, g, r) * o[:, r * tq:(r + 1) * tq]


def _sel_attn(feat_t, tok, selm, cos_t, sin_t, gates_t, oin, batch, seq, tq, tk):
    t = feat_t.shape[1]
    nq = seq // tq
    lanes = HEADS_PER_GROUP * tq
    v_row0 = NSA_HEADS
    assert tk % tq == 0 and seq % tk == 0
    key_block = jnp.arange(seq, dtype=jnp.int32)[:, None] // SEL_BLOCK
    block_of_key = jnp.where(key_block == jnp.arange(LANES, dtype=jnp.int32)[None, :], BLOCK_MASK_BIG, 0.0)
    return pl.pallas_call(
        functools.partial(_sel_attn_body, tq=tq, tk=tk),
        grid=(batch, NSA_GROUPS, nq),
        in_specs=[
            pl.BlockSpec((GROUP_WIDTH, tq), lambda b, g, i: (g, b * nq + i)),
            pl.BlockSpec((None, seq, HEAD_DIM), lambda b, g, i: (g, b, 0)),
            pl.BlockSpec((seq, LANES), lambda b, g, i: (0, 0)),
            pl.BlockSpec((HEAD_DIM, seq), lambda b, g, i: (v_row0 + g, b)),
            pl.BlockSpec((None, LANES, tq), lambda b, g, i: (g, 0, b * nq + i)),
            pl.BlockSpec((ROPE_HALF, tq), lambda b, g, i: (0, i)),
            pl.BlockSpec((ROPE_HALF, tq), lambda b, g, i: (0, i)),
            pl.BlockSpec((LANES, tq), lambda b, g, i: (0, b * nq + i)),
            pl.BlockSpec((GROUP_WIDTH, tq), lambda b, g, i: (g, b * nq + i)),
        ],
        out_specs=pl.BlockSpec((GROUP_WIDTH, tq), lambda b, g, i: (g, b * nq + i)),
        out_shape=jax.ShapeDtypeStruct((D_MODEL, t), F32),
        scratch_shapes=[
            pltpu.VMEM((1, lanes), F32),
            pltpu.VMEM((1, lanes), F32),
            pltpu.VMEM((HEAD_DIM, lanes), F32),
            pltpu.VMEM((tk, lanes), F32),
            pltpu.VMEM((tk, lanes), F32),
        ],
        compiler_params=_params(("parallel", "parallel", "parallel")),
        name="attn_sel",
    )(feat_t, tok, block_of_key.astype(BF16), feat_t, selm, cos_t, sin_t, gates_t, oin)


def _win_attn_body(q_ref, k_ref, v_ref, bias_ref, cos_ref, sin_ref, gates_ref, oin_ref, o_ref, *score_refs, tq):
    g = pl.program_id(1)
    n_sub = len(score_refs)
    ts = tq // n_sub
    band = WINDOW + ts
    starts = []
    for u, s_ref in enumerate(score_refs):
        t0 = pl.program_id(2) * tq + u * ts
        lanes = slice(u * ts, (u + 1) * ts)
        q = _rope_feature_major(_load_q_t(q_ref, u * ts, ts), _lane_tile(cos_ref[:, lanes], HEADS_PER_GROUP),
                                _lane_tile(sin_ref[:, lanes], HEADS_PER_GROUP))
        start = pl.multiple_of(jnp.maximum(t0 - WINDOW, 0), ts)
        s_ref[...] = _dot(k_ref[pl.ds(start, band), :], q)
        starts.append((t0, start, lanes))
    for s_ref, (t0, start, lanes) in zip(score_refs, starts):
        bias = bias_ref[jnp.minimum(t0 // ts, WINDOW // ts)]
        s = s_ref[...] + _lane_tile(bias, HEADS_PER_GROUP)
        m = jnp.max(s, axis=0, keepdims=True)
        p = jnp.exp2(s - m)
        inv = 1.0 / jnp.sum(p, axis=0, keepdims=True)
        o = _dot(v_ref[:, pl.ds(start, band)], p.astype(BF16)) * inv
        for r in range(HEADS_PER_GROUP):
            rows = slice(r * HEAD_DIM, (r + 1) * HEAD_DIM)
            total = oin_ref[rows, lanes] + _gate_row(gates_ref, 2, g, r)[:, lanes] * o[:, r * ts:(r + 1) * ts]
            o_ref[lanes, rows] = total.T.astype(BF16)


def _win_attn(feat_t, tok, cos_t, sin_t, gates_t, oin, batch, seq, tq, ts):
    t = feat_t.shape[1]
    nq = seq // tq
    v_row0 = NSA_HEADS + NSA_GROUPS
    assert seq >= WINDOW + ts and tq % ts == 0 and WINDOW % ts == 0
    band = WINDOW + ts
    rel = jnp.arange(band, dtype=jnp.int32)[None, :, None] - jnp.arange(ts, dtype=jnp.int32)[None, None, :]
    d = (jnp.arange(WINDOW // ts + 1, dtype=jnp.int32) * ts)[:, None, None]
    win_bias = jnp.where((rel <= d) & (rel > d - WINDOW), 0.0, MASK_VALUE).astype(F32)
    return pl.pallas_call(
        functools.partial(_win_attn_body, tq=tq),
        grid=(batch, NSA_GROUPS, nq),
        in_specs=[
            pl.BlockSpec((GROUP_WIDTH, tq), lambda b, g, i: (g, b * nq + i)),
            pl.BlockSpec((None, seq, HEAD_DIM), lambda b, g, i: (NSA_GROUPS + g, b, 0)),
            pl.BlockSpec((HEAD_DIM, seq), lambda b, g, i: (v_row0 + g, b)),
            pl.BlockSpec((WINDOW // ts + 1, band, ts), lambda b, g, i: (0, 0, 0)),
            pl.BlockSpec((ROPE_HALF, tq), lambda b, g, i: (0, i)),
            pl.BlockSpec((ROPE_HALF, tq), lambda b, g, i: (0, i)),
            pl.BlockSpec((LANES, tq), lambda b, g, i: (0, b * nq + i)),
            pl.BlockSpec((GROUP_WIDTH, tq), lambda b, g, i: (g, b * nq + i)),
        ],
        out_specs=pl.BlockSpec((tq, GROUP_WIDTH), lambda b, g, i: (b * nq + i, g)),
        out_shape=jax.ShapeDtypeStruct((t, D_MODEL), BF16),
        scratch_shapes=[pltpu.VMEM((WINDOW + ts, HEADS_PER_GROUP * ts), F32) for _ in range(tq // ts)],
        compiler_params=_params(("parallel", "parallel", "parallel")),
        name="attn_win",
    )(feat_t, tok, feat_t, win_bias, cos_t, sin_t, gates_t, oin)


def _nsa_mixer(h, norm_rows, layer, layer_j, batch, seq, w_in, w_out, phi_pe, phi_w1, phi_b1, phi_w2):
    t = h.shape[0]
    w_gate = jnp.pad(w_in[layer_j, :, QKV_WIDTH:], ((0, 0), (0, LANES - N_GATES)))
    cos_tok, sin_tok, cos_t, sin_t = _rope_tables(seq)

    feat_t, gates_t, xn = _nsa_inproj_feat(h, norm_rows, layer, w_in, layer_j, w_gate, tm=1024)
    x_cmp, tok = _nsa_inproj_tok(xn, w_in, layer_j, cos_tok, sin_tok, seq, tm=min(1024, seq))

    pe = phi_pe.reshape(phi_pe.shape[0], 2, 1, CMP_BLOCK * HEAD_DIM)
    b1 = phi_b1.reshape(phi_b1.shape[0], 2, 1, CMP_HIDDEN)
    kc = _compress(x_cmp, 0, pe, phi_w1, b1, phi_w2, layer_j, batch, feature_major=False)
    vc_t = _compress(x_cmp, 1, pe, phi_w1, b1, phi_w2[layer_j, 1].T, layer_j, batch, feature_major=True)

    o1, selm = _cmp_attn(feat_t, kc, vc_t, gates_t, batch, seq, tq=1024, ts=256)
    o2 = _sel_attn(feat_t, tok, selm, cos_t, sin_t, gates_t, o1, batch, seq, tq=256, tk=512)
    o3 = _win_attn(feat_t, tok, cos_t, sin_t, gates_t, o2, batch, seq, tq=1024, ts=256)
    return _outproj(o3, w_out, layer_j, h, norm_rows, layer, 3, tm=min(1024, t), tn=512)


def _conv_inproj_body(h_ref, halo_ref, g_ref, wb_ref, wc_ref, wu_ref, cw_ref, o_ref, xn_ref, *, tiles_per_seq):
    i = pl.program_id(0)
    j = pl.program_id(1)

    @pl.when(j == 0)
    def _():
        keep = jnp.where(i % tiles_per_seq == 0, 0.0, 1.0)
        xn_ref[0:CONV_HALO] = (_rms(halo_ref[...], g_ref[...]) * keep).astype(BF16)
        xn_ref[CONV_HALO:] = _rms(h_ref[...], g_ref[...]).astype(BF16)

    x = xn_ref[...]
    bg = _dot(x, wb_ref[...].astype(BF16))
    z = _dot(x, wc_ref[...].astype(BF16)) * _dot(x, wu_ref[...].astype(BF16))
    cw = cw_ref[...]
    conv = cw[2:3] * z + cw[1:2] * pltpu.roll(z, 1, axis=0) + cw[0:1] * pltpu.roll(z, 2, axis=0)
    o_ref[...] = (bg * conv)[CONV_HALO:].astype(BF16)


def _conv_inproj(h, norm_rows, layer, w_in, conv_w, layer_j, seq, tm, tn):
    t = h.shape[0]
    nj = D_MODEL // tn
    halo_blocks = tm // CONV_HALO
    return pl.pallas_call(
        functools.partial(_conv_inproj_body, tiles_per_seq=seq // tm),
        grid=(t // tm, nj),
        in_specs=[
            pl.BlockSpec((tm, D_MODEL), lambda i, j: (i, 0)),
            pl.BlockSpec((CONV_HALO, D_MODEL), lambda i, j: (jnp.maximum(i * halo_blocks - 1, 0), 0)),
            _norm_row_spec(layer, 2),
            pl.BlockSpec((None, D_MODEL, tn), lambda i, j: (layer_j, 0, j)),
            pl.BlockSpec((None, D_MODEL, tn), lambda i, j: (layer_j, 0, nj + j)),
            pl.BlockSpec((None, D_MODEL, tn), lambda i, j: (layer_j, 0, 2 * nj + j)),
            pl.BlockSpec((None, CONV_WIDTH, tn), lambda i, j: (layer_j, 0, j)),
        ],
        out_specs=pl.BlockSpec((tm, tn), lambda i, j: (i, j)),
        out_shape=jax.ShapeDtypeStruct((t, D_MODEL), BF16),
        scratch_shapes=[pltpu.VMEM((tm + CONV_HALO, D_MODEL), BF16)],
        compiler_params=_params(("parallel", "arbitrary")),
        name="conv_inproj",
    )(h, h, norm_rows, w_in, w_in, w_in, conv_w)


def _sgu_inproj_body(h_ref, g_ref, w_ref, lng_ref, lnb_ref, ws_ref, bs_ref, o_ref, xn_ref, z_ref, *, tm, tn):
    j = pl.program_id(1)

    @pl.when(j == 0)
    def _():
        xn_ref[...] = _rms(h_ref[...], g_ref[...]).astype(BF16)

    col = pl.multiple_of(j * tn, tn)
    z_ref[:, pl.ds(col, tn)] = jax.nn.gelu(_dot(xn_ref[...], w_ref[...].astype(BF16)))

    @pl.when(j == pl.num_programs(1) - 1)
    def _():
        v = z_ref[:, D_MODEL:]
        mu = jnp.mean(v, axis=-1, keepdims=True)
        var = jnp.mean(jnp.square(v - mu), axis=-1, keepdims=True)
        vn = ((v - mu) * lax.rsqrt(var + EPS) * lng_ref[...] + lnb_ref[...]).astype(BF16)
        row = lax.broadcasted_iota(jnp.int32, (SGU_CHUNK, SGU_CHUNK), 0)
        colm = lax.broadcasted_iota(jnp.int32, (SGU_CHUNK, SGU_CHUNK), 1)
        bs = bs_ref[...]
        for grp in range(SGU_GROUPS):
            ws = jnp.where(colm <= row, ws_ref[grp], 0.0).astype(BF16)
            cols = slice(grp * SGU_GROUP_DIM, (grp + 1) * SGU_GROUP_DIM)
            for c in range(tm // SGU_CHUNK):
                rws = slice(c * SGU_CHUNK, (c + 1) * SGU_CHUNK)
                sv = _dot(ws, vn[rws, cols]) + bs[:, grp:grp + 1]
                o_ref[rws, cols] = (z_ref[rws, cols] * sv).astype(BF16)


def _sgu_inproj(h, norm_rows, layer, w_in, ln_g, ln_b, w_s, b_s_t, layer_j, tm, tn):
    t = h.shape[0]
    return pl.pallas_call(
        functools.partial(_sgu_inproj_body, tm=tm, tn=tn),
        grid=(t // tm, 2 * D_MODEL // tn),
        in_specs=[
            pl.BlockSpec((tm, D_MODEL), lambda i, j: (i, 0)),
            _norm_row_spec(layer, 2),
            pl.BlockSpec((None, D_MODEL, tn), lambda i, j: (layer_j, 0, j)),
            pl.BlockSpec((None, 1, D_MODEL), lambda i, j: (layer_j, 0, 0)),
            pl.BlockSpec((None, 1, D_MODEL), lambda i, j: (layer_j, 0, 0)),
            pl.BlockSpec((None, SGU_GROUPS, SGU_CHUNK, SGU_CHUNK), lambda i, j: (layer_j, 0, 0, 0)),
            pl.BlockSpec((None, SGU_CHUNK, SGU_GROUPS), lambda i, j: (layer_j, 0, 0)),
        ],
        out_specs=pl.BlockSpec((tm, D_MODEL), lambda i, j: (i, 0)),
        out_shape=jax.ShapeDtypeStruct((t, D_MODEL), BF16),
        scratch_shapes=[pltpu.VMEM((tm, D_MODEL), BF16), pltpu.VMEM((tm, 2 * D_MODEL), F32)],
        compiler_params=_params(("parallel", "arbitrary")),
        name="sgu_inproj",
    )(h, norm_rows, w_in, ln_g, ln_b, w_s, b_s_t)


def kernel(x, p, norm_g, ffn1_wg, ffn1_wu, ffn1_wd, ffn2_wg, ffn2_wu, ffn2_wd, ple_wg, ple_wp, nsa_w_in, nsa_w_out, nsa_phi_pe, nsa_phi_w1, nsa_phi_b1, nsa_phi_w2, conv_w_in, conv_w, conv_w_out, sgu_w_in, sgu_ln_g, sgu_ln_b, sgu_w_s, sgu_b_s, sgu_w_out):
    batch, seq, d = x.shape
    depth = p.shape[0]
    t = batch * seq
    assert d == D_MODEL and seq % 512 == 0
    h = x.reshape(t, d)
    p2 = p.reshape(depth, t, PLE_DIM)
    norm_rows = norm_g.reshape(depth * N_NORMS, 1, d)
    sgu_ln_g3 = sgu_ln_g.reshape(-1, 1, d)
    sgu_ln_b3 = sgu_ln_b.reshape(-1, 1, d)
    sgu_b_s_t = jnp.swapaxes(sgu_b_s, 1, 2)
    tm_big = min(1024, t)

    for layer in range(depth):
        layer_j = layer // N_MIXERS
        h = _ffn(h, norm_rows, layer, 0, 1, ffn1_wg, ffn1_wu, ffn1_wd, tm=tm_big, tf=256)
        kind = layer % N_MIXERS
        if kind == 0:
            h = _nsa_mixer(h, norm_rows, layer, layer_j, batch, seq, nsa_w_in, nsa_w_out, nsa_phi_pe,
                           nsa_phi_w1, nsa_phi_b1, nsa_phi_w2)
        elif kind == 1:
            a = _conv_inproj(h, norm_rows, layer, conv_w_in, conv_w, layer_j, seq, tm=min(1024, seq), tn=512)
            h = _outproj(a, conv_w_out, layer_j, h, norm_rows, layer, 3, tm=tm_big, tn=512)
        else:
            a = _sgu_inproj(h, norm_rows, layer, sgu_w_in, sgu_ln_g3, sgu_ln_b3, sgu_w_s, sgu_b_s_t, layer_j,
                            tm=512, tn=512)
            h = _outproj(a, sgu_w_out, layer_j, h, norm_rows, layer, 3, tm=tm_big, tn=512)
        h = _ffn(h, norm_rows, layer, 4, 5, ffn2_wg, ffn2_wu, ffn2_wd, tm=tm_big, tf=256)
        h = _ple(h, p2, norm_rows, layer, ple_wg, ple_wp, tm=tm_big, tn=512)
    return h.reshape(batch, seq, d)
```

```python
import functools
import math

import jax
import jax.numpy as jnp
from jax import lax
from jax.experimental import pallas as pl
from jax.experimental.pallas import tpu as pltpu

F32 = jnp.float32
BF16 = jnp.bfloat16

EPS = 1e-6
LANES = 128
D_MODEL = 2048
D_FF = 5632
PLE_DIM = 256
N_NORMS = 8
N_MIXERS = 3
NSA_HEADS = 16
NSA_GROUPS = 4
HEADS_PER_GROUP = NSA_HEADS // NSA_GROUPS
HEAD_DIM = D_MODEL // NSA_HEADS
GROUP_WIDTH = HEADS_PER_GROUP * HEAD_DIM
KV_WIDTH = NSA_GROUPS * HEAD_DIM
ROPE_DIM = HEAD_DIM // 4
ROPE_HALF = ROPE_DIM // 2
ROPE_THETA = 500000.0
CMP_BLOCK = 32
CMP_STRIDE = 16
CMP_HIDDEN = 256
SEL_BLOCK = 64
SEL_TOP = 16
WINDOW = 512
QKV_WIDTH = NSA_HEADS * HEAD_DIM + 6 * KV_WIDTH
N_GATES = 3 * NSA_HEADS
SGU_CHUNK = 128
SGU_GROUPS = 8
SGU_GROUP_DIM = D_MODEL // SGU_GROUPS
CONV_WIDTH = 3
CONV_HALO = 16

MASK_VALUE = -1e30
BLOCK_MASK_BIG = 2.0 ** 100
SCORE_SCALE = HEAD_DIM ** -0.5
EXP2_SCALE = SCORE_SCALE * math.log2(math.e)
VMEM_LIMIT = 60 * 1024 * 1024


def _params(semantics):
    return pltpu.CompilerParams(dimension_semantics=semantics, vmem_limit_bytes=VMEM_LIMIT)


def _rms(x, g):
    return x * lax.rsqrt(jnp.mean(x * x, axis=-1, keepdims=True) + EPS) * g


def _dot(a, b):
    return jnp.dot(a, b, preferred_element_type=F32)


def _dot_nt(a, b):
    return lax.dot_general(a, b, (((1,), (1,)), ((), ())), preferred_element_type=F32)


def _norm_row_spec(layer, k):
    idx = layer * N_NORMS + k
    return pl.BlockSpec((None, 1, D_MODEL), lambda i, j: (idx, 0, 0))


def _ffn_body(h_ref, gpre_ref, gpost_ref, wg_ref, wu_ref, wd_ref, o_ref, xn_ref):
    f = pl.program_id(1)

    @pl.when(f == 0)
    def _():
        xn_ref[...] = _rms(h_ref[...], gpre_ref[...]).astype(BF16)
        o_ref[...] = jnp.zeros_like(o_ref)

    x = xn_ref[...]
    gate = _dot(x, wg_ref[...].astype(BF16))
    up = _dot(x, wu_ref[...].astype(BF16))
    act = (gate * jax.nn.sigmoid(gate) * up).astype(BF16)
    o_ref[...] += _dot(act, wd_ref[...].astype(BF16))

    @pl.when(f == pl.num_programs(1) - 1)
    def _():
        o_ref[...] = h_ref[...] + 0.5 * _rms(o_ref[...], gpost_ref[...])


def _ffn(h, norm_rows, layer, k_pre, k_post, wg, wu, wd, tm, tf):
    t = h.shape[0]
    return pl.pallas_call(
        _ffn_body,
        grid=(t // tm, D_FF // tf),
        in_specs=[
            pl.BlockSpec((tm, D_MODEL), lambda i, f: (i, 0)),
            _norm_row_spec(layer, k_pre),
            _norm_row_spec(layer, k_post),
            pl.BlockSpec((None, D_MODEL, tf), lambda i, f: (layer, 0, f)),
            pl.BlockSpec((None, D_MODEL, tf), lambda i, f: (layer, 0, f)),
            pl.BlockSpec((None, tf, D_MODEL), lambda i, f: (layer, f, 0)),
        ],
        out_specs=pl.BlockSpec((tm, D_MODEL), lambda i, f: (i, 0)),
        out_shape=jax.ShapeDtypeStruct((t, D_MODEL), F32),
        scratch_shapes=[pltpu.VMEM((tm, D_MODEL), BF16)],
        compiler_params=_params(("parallel", "arbitrary")),
        name="ffn",
    )(h, norm_rows, norm_rows, wg, wu, wd)


def _outproj_body(a_ref, w_ref, h_ref, g_ref, o_ref, *, tn):
    j = pl.program_id(1)
    col = pl.multiple_of(j * tn, tn)
    o_ref[:, pl.ds(col, tn)] = _dot(a_ref[...], w_ref[...].astype(BF16))

    @pl.when(j == pl.num_programs(1) - 1)
    def _():
        o_ref[...] = h_ref[...] + _rms(o_ref[...], g_ref[...])


def _outproj(a, w, w_layer, h, norm_rows, layer, k_norm, tm, tn):
    t, kdim = a.shape
    return pl.pallas_call(
        functools.partial(_outproj_body, tn=tn),
        grid=(t // tm, D_MODEL // tn),
        in_specs=[
            pl.BlockSpec((tm, kdim), lambda i, j: (i, 0)),
            pl.BlockSpec((None, kdim, tn), lambda i, j: (w_layer, 0, j)),
            pl.BlockSpec((tm, D_MODEL), lambda i, j: (i, 0)),
            _norm_row_spec(layer, k_norm),
        ],
        out_specs=pl.BlockSpec((tm, D_MODEL), lambda i, j: (i, 0)),
        out_shape=jax.ShapeDtypeStruct((t, D_MODEL), F32),
        compiler_params=_params(("parallel", "arbitrary")),
        name="outproj",
    )(a, w, h, norm_rows)


def _ple_body(h_ref, p_ref, gpre_ref, gpost_ref, wg_ref, wp_ref, o_ref, xn_ref, pb_ref, *, tn):
    j = pl.program_id(1)

    @pl.when(j == 0)
    def _():
        xn_ref[...] = _rms(h_ref[...], gpre_ref[...]).astype(BF16)
        pb_ref[...] = p_ref[...].astype(BF16)

    gate = jax.nn.sigmoid(_dot(xn_ref[...], wg_ref[...].astype(BF16)))
    emb = _dot(pb_ref[...], wp_ref[...].astype(BF16))
    col = pl.multiple_of(j * tn, tn)
    o_ref[:, pl.ds(col, tn)] = gate * emb

    @pl.when(j == pl.num_programs(1) - 1)
    def _():
        o_ref[...] = h_ref[...] + _rms(o_ref[...], gpost_ref[...])


def _ple(h, p, norm_rows, layer, wg, wp, tm, tn):
    t = h.shape[0]
    return pl.pallas_call(
        functools.partial(_ple_body, tn=tn),
        grid=(t // tm, D_MODEL // tn),
        in_specs=[
            pl.BlockSpec((tm, D_MODEL), lambda i, j: (i, 0)),
            pl.BlockSpec((None, tm, PLE_DIM), lambda i, j: (layer, i, 0)),
            _norm_row_spec(layer, 6),
            _norm_row_spec(layer, 7),
            pl.BlockSpec((None, D_MODEL, tn), lambda i, j: (layer, 0, j)),
            pl.BlockSpec((None, PLE_DIM, tn), lambda i, j: (layer, 0, j)),
        ],
        out_specs=pl.BlockSpec((tm, D_MODEL), lambda i, j: (i, 0)),
        out_shape=jax.ShapeDtypeStruct((t, D_MODEL), F32),
        scratch_shapes=[pltpu.VMEM((tm, D_MODEL), BF16), pltpu.VMEM((tm, PLE_DIM), BF16)],
        compiler_params=_params(("parallel", "arbitrary")),
        name="ple",
    )(h, p, norm_rows, norm_rows, wg, wp)


def _nsa_inproj_feat_body(h_ref, g_ref, w_ref, wg_ref, feat_ref, gates_ref, xn_ref, *, n_q_tiles):
    j = pl.program_id(1)

    @pl.when(j == 0)
    def _():
        xn = _rms(h_ref[...], g_ref[...]).astype(BF16)
        xn_ref[...] = xn
        gates_ref[...] = jax.nn.sigmoid(_dot_nt(wg_ref[...].astype(BF16), xn))

    row_scale = jnp.where(j < n_q_tiles, EXP2_SCALE, 1.0)
    feat_ref[...] = (_dot_nt(w_ref[...].astype(BF16), xn_ref[...]) * row_scale).astype(BF16)


def _nsa_inproj_feat(h, norm_rows, layer, w_in_t, w_layer, w_gate_t, tm):
    t = h.shape[0]
    tn = KV_WIDTH
    n_q_tiles = NSA_HEADS * HEAD_DIM // tn
    n_feat = NSA_HEADS * HEAD_DIM + 2 * KV_WIDTH
    return pl.pallas_call(
        functools.partial(_nsa_inproj_feat_body, n_q_tiles=n_q_tiles),
        grid=(t // tm, n_feat // tn),
        in_specs=[
            pl.BlockSpec((tm, D_MODEL), lambda i, j: (i, 0)),
            _norm_row_spec(layer, 2),
            pl.BlockSpec((None, tn, D_MODEL),
                         lambda i, j: (w_layer, j + 3 * (j // n_q_tiles) + j // (n_q_tiles + 1), 0)),
            pl.BlockSpec((LANES, D_MODEL), lambda i, j: (0, 0)),
        ],
        out_specs=[
            pl.BlockSpec((tn, tm), lambda i, j: (j, i)),
            pl.BlockSpec((LANES, tm), lambda i, j: (0, i)),
            pl.BlockSpec((tm, D_MODEL), lambda i, j: (i, 0)),
        ],
        out_shape=[
            jax.ShapeDtypeStruct((n_feat, t), BF16),
            jax.ShapeDtypeStruct((LANES, t), F32),
            jax.ShapeDtypeStruct((t, D_MODEL), BF16),
        ],
        compiler_params=_params(("parallel", "arbitrary")),
        name="nsa_inproj_feat",
    )(h, norm_rows, w_in_t, w_gate_t)


def _rope_token_major(x, cos, sin):
    lane = lax.broadcasted_iota(jnp.int32, x.shape, 1)
    partner = jnp.where(lane < ROPE_HALF, pltpu.roll(x, LANES - ROPE_HALF, axis=1),
                        pltpu.roll(x, ROPE_HALF, axis=1))
    return x * cos + partner * sin


def _nsa_inproj_tok_body(xn_ref, wc_ref, wk_ref, cos_ref, sin_ref, oc_ref, ok_ref):
    xn = xn_ref[...]
    res_c = _dot_nt(xn, wc_ref[...].astype(BF16))
    res_k = _dot_nt(xn, wk_ref[...].astype(BF16))
    cos, sin = cos_ref[...], sin_ref[...]
    for c in range(NSA_GROUPS):
        cols = slice(c * LANES, (c + 1) * LANES)
        oc_ref[c] = res_c[:, cols]
        ok_ref[c] = _rope_token_major(res_k[:, cols], cos, sin).astype(BF16)


def _nsa_inproj_tok(xn, w_in_t, w_layer, cos_tok, sin_tok, seq, tm):
    t = xn.shape[0]
    per_seq = seq // tm
    q_blocks = NSA_HEADS * HEAD_DIM // KV_WIDTH
    slab_spec = pl.BlockSpec((NSA_GROUPS, tm, LANES), lambda i, j: (j, i, 0))
    return pl.pallas_call(
        _nsa_inproj_tok_body,
        grid=(t // tm, 2),
        in_specs=[
            pl.BlockSpec((tm, D_MODEL), lambda i, j: (i, 0)),
            pl.BlockSpec((None, KV_WIDTH, D_MODEL), lambda i, j: (w_layer, q_blocks + j, 0)),
            pl.BlockSpec((None, KV_WIDTH, D_MODEL), lambda i, j: (w_layer, q_blocks + 2 + 2 * j, 0)),
            pl.BlockSpec((tm, LANES), lambda i, j: (i % per_seq, 0)),
            pl.BlockSpec((tm, LANES), lambda i, j: (i % per_seq, 0)),
        ],
        out_specs=[slab_spec, slab_spec],
        out_shape=[
            jax.ShapeDtypeStruct((2 * NSA_GROUPS, t, LANES), F32),
            jax.ShapeDtypeStruct((2 * NSA_GROUPS, t, LANES), BF16),
        ],
        compiler_params=_params(("parallel", "arbitrary")),
        name="nsa_inproj_tok",
    )(xn, w_in_t, w_in_t, cos_tok, sin_tok)


def _rope_tables(seq):
    inv_freq = jnp.power(jnp.float32(ROPE_THETA), -jnp.arange(0, ROPE_DIM, 2, dtype=F32) / ROPE_DIM)
    ang = jnp.arange(seq, dtype=F32)[:, None] * inv_freq[None, :]
    cos, sin = jnp.cos(ang), jnp.sin(ang)
    cos_tok = jnp.concatenate([cos, cos, jnp.ones((seq, LANES - ROPE_DIM), F32)], axis=1)
    sin_tok = jnp.concatenate([-sin, sin, jnp.zeros((seq, LANES - ROPE_DIM), F32)], axis=1)
    return cos_tok, sin_tok, cos.T, sin.T


def _compress_body(x_ref, pe_ref, w1_ref, b1_ref, w2_ref, o_ref, *, feature_major):
    n_chunks = x_ref.shape[0] // CMP_STRIDE
    w1 = w1_ref[...].astype(BF16)
    both = jnp.zeros((n_chunks, 2 * CMP_HIDDEN), F32)
    for l in range(CMP_STRIDE):
        x_l = x_ref[pl.ds(l, n_chunks, stride=CMP_STRIDE), :].astype(BF16)
        lo, hi = l * HEAD_DIM, (CMP_STRIDE + l) * HEAD_DIM
        w_l = jnp.concatenate([w1[lo:lo + HEAD_DIM], w1[hi:hi + HEAD_DIM]], axis=1)
        both = both + _dot(x_l, w_l)
    first = both[:, :CMP_HIDDEN]
    second = pltpu.roll(both[:, CMP_HIDDEN:], n_chunks - 1, axis=0)
    pe = jnp.broadcast_to(pe_ref[...], (8, CMP_BLOCK * HEAD_DIM)).astype(BF16)
    const = _dot(pe, w1)[0:1]
    hid = first + second + const + b1_ref[...]
    act = (hid * jax.nn.sigmoid(hid)).astype(BF16)
    if feature_major:
        o_ref[...] = _dot_nt(w2_ref[...].astype(BF16), act).astype(BF16)
    else:
        o_ref[...] = _dot(act, w2_ref[...].astype(BF16)).astype(BF16)


def _compress(x, which, pe, w1, b1, w2, layer_j, batch, feature_major):
    seq = x.shape[1] // batch
    n_chunks = seq // CMP_STRIDE
    if feature_major:
        w2_spec = pl.BlockSpec((HEAD_DIM, CMP_HIDDEN), lambda g, b: (0, 0))
        out_block, out_dims = (None, None, HEAD_DIM, n_chunks), (NSA_GROUPS, batch, HEAD_DIM, n_chunks)
    else:
        w2_spec = pl.BlockSpec((None, None, CMP_HIDDEN, HEAD_DIM), lambda g, b: (layer_j, which, 0, 0))
        out_block, out_dims = (None, None, n_chunks, HEAD_DIM), (NSA_GROUPS, batch, n_chunks, HEAD_DIM)
    return pl.pallas_call(
        functools.partial(_compress_body, feature_major=feature_major),
        grid=(NSA_GROUPS, batch),
        in_specs=[
            pl.BlockSpec((None, seq, HEAD_DIM), lambda g, b: (which * NSA_GROUPS + g, b, 0)),
            pl.BlockSpec((None, None, 1, CMP_BLOCK * HEAD_DIM), lambda g, b: (layer_j, which, 0, 0)),
            pl.BlockSpec((None, None, CMP_BLOCK * HEAD_DIM, CMP_HIDDEN), lambda g, b: (layer_j, which, 0, 0)),
            pl.BlockSpec((None, None, 1, CMP_HIDDEN), lambda g, b: (layer_j, which, 0, 0)),
            w2_spec,
        ],
        out_specs=pl.BlockSpec(out_block, lambda g, b: (g, b, 0, 0)),
        out_shape=jax.ShapeDtypeStruct(out_dims, BF16),
        compiler_params=_params(("parallel", "parallel")),
        name="compress",
    )(x, pe, w1, b1, w2)


def _lane_tile(x, n):
    return jnp.concatenate([x] * n, axis=1)


def _load_q_t(q_ref, lo=0, n=None):
    n = q_ref.shape[1] if n is None else n
    return jnp.concatenate([q_ref[r * HEAD_DIM:(r + 1) * HEAD_DIM, lo:lo + n] for r in range(HEADS_PER_GROUP)],
                           axis=1)


def _rope_feature_major(q, cos, sin):
    x1 = q[0:ROPE_HALF].astype(F32)
    x2 = q[ROPE_HALF:ROPE_DIM].astype(F32)
    r1 = (x1 * cos - x2 * sin).astype(BF16)
    r2 = (x2 * cos + x1 * sin).astype(BF16)
    return jnp.concatenate([r1, r2, q[ROPE_DIM:]], axis=0)


def _split3(x):
    hi = x.astype(BF16)
    r1 = x - hi.astype(F32)
    mid = r1.astype(BF16)
    lo = (r1 - mid.astype(F32)).astype(BF16)
    return hi, mid, lo


def _gate_row(gates_ref, branch, g, r):
    return gates_ref[pl.ds(branch * NSA_HEADS + g * HEADS_PER_GROUP + r, 1), :]


def _cmp_attn_body(q_ref, kc_ref, vc_ref, gates_ref, o_ref, sel_ref, *score_refs, tq, n_cmp):
    n_sub = len(score_refs)
    ts = tq // n_sub
    kc = kc_ref[...]
    for u, s_ref in enumerate(score_refs):
        s_ref[...] = _dot(kc, _load_q_t(q_ref, u * ts, ts))
    for u, s_ref in enumerate(score_refs):
        _cmp_attn_subtile(s_ref[...], vc_ref, gates_ref, o_ref, sel_ref, pl.program_id(2) * tq + u * ts, u * ts,
                          ts, n_cmp)


def _cmp_attn_subtile(s, vc_ref, gates_ref, o_ref, sel_ref, t0, lane0, tq, n_cmp):
    g = pl.program_id(1)
    out_lanes = slice(lane0, lane0 + tq)

    tpos = t0 + lax.broadcasted_iota(jnp.int32, (n_cmp, tq), 1)
    cend = lax.broadcasted_iota(jnp.int32, (n_cmp, tq), 0) * CMP_STRIDE + (CMP_BLOCK - 1)
    visible = cend <= tpos
    bias = _lane_tile(jnp.where(visible, 0.0, MASK_VALUE), HEADS_PER_GROUP)
    okf = _lane_tile(jnp.where(visible, 1.0, 0.0), HEADS_PER_GROUP)
    s = s + bias
    m = jnp.max(s, axis=0, keepdims=True)
    e = jnp.exp2(s - m) * okf
    inv = 1.0 / jnp.maximum(jnp.sum(e, axis=0, keepdims=True), 1e-30)
    p = e * inv
    o = _dot(vc_ref[...], p.astype(BF16))

    for r in range(HEADS_PER_GROUP):
        gate = _gate_row(gates_ref, 0, g, r)[:, out_lanes]
        o_ref[r * HEAD_DIM:(r + 1) * HEAD_DIM, out_lanes] = gate * o[:, r * tq:(r + 1) * tq]

    psum = p[:, 0:tq]
    for r in range(1, HEADS_PER_GROUP):
        psum = psum + p[:, r * tq:(r + 1) * tq]
    n_sel_rows = LANES // 2
    jrow = lax.broadcasted_iota(jnp.int32, (n_sel_rows, n_cmp), 0)
    ccol = lax.broadcasted_iota(jnp.int32, (n_sel_rows, n_cmp), 1)
    c_lo, c_hi = ccol * CMP_STRIDE, ccol * CMP_STRIDE + (CMP_BLOCK - 1)
    j_lo, j_hi = jrow * SEL_BLOCK, jrow * SEL_BLOCK + (SEL_BLOCK - 1)
    ov = jnp.maximum(jnp.minimum(c_hi, j_hi) - jnp.maximum(c_lo, j_lo) + 1, 0).astype(F32) / CMP_STRIDE
    ov = ov.astype(BF16)
    hi, mid, lo = _split3(psum)
    imp = _dot(ov, hi) + _dot(ov, mid) + _dot(ov, lo)

    blk = lax.broadcasted_iota(jnp.int32, (n_sel_rows, tq), 0)
    cur = (t0 + lax.broadcasted_iota(jnp.int32, (n_sel_rows, tq), 1)) // SEL_BLOCK
    forced = (blk == 0) | (blk == cur) | (blk == cur - 1)
    key = jnp.where(blk > cur, -1.0, imp)
    key = jnp.where(forced, 1e30, key)
    sel = jnp.zeros((n_sel_rows, tq), F32)
    for _ in range(SEL_TOP):
        mx = jnp.max(key, axis=0, keepdims=True)
        first = jnp.min(jnp.where(key == mx, blk, LANES), axis=0, keepdims=True)
        pick = blk == first
        sel = jnp.where(pick, 1.0, sel)
        key = jnp.where(pick, -2.0, key)
    sel = jnp.where(blk > cur, 0.0, sel)
    sel_ref[:, out_lanes] = jnp.concatenate([sel, jnp.zeros_like(sel)], axis=0).astype(BF16)


def _cmp_attn(feat_t, kc, vc_t, gates_t, batch, seq, tq, ts):
    t = feat_t.shape[1]
    nq = seq // tq
    n_cmp = kc.shape[2]
    assert seq // SEL_BLOCK <= LANES // 2
    return pl.pallas_call(
        functools.partial(_cmp_attn_body, tq=tq, n_cmp=n_cmp),
        grid=(batch, NSA_GROUPS, nq),
        in_specs=[
            pl.BlockSpec((GROUP_WIDTH, tq), lambda b, g, i: (g, b * nq + i)),
            pl.BlockSpec((None, None, n_cmp, HEAD_DIM), lambda b, g, i: (g, b, 0, 0)),
            pl.BlockSpec((None, None, HEAD_DIM, n_cmp), lambda b, g, i: (g, b, 0, 0)),
            pl.BlockSpec((LANES, tq), lambda b, g, i: (0, b * nq + i)),
        ],
        out_specs=[
            pl.BlockSpec((GROUP_WIDTH, tq), lambda b, g, i: (g, b * nq + i)),
            pl.BlockSpec((None, LANES, tq), lambda b, g, i: (g, 0, b * nq + i)),
        ],
        out_shape=[
            jax.ShapeDtypeStruct((D_MODEL, t), F32),
            jax.ShapeDtypeStruct((NSA_GROUPS, LANES, t), BF16),
        ],
        scratch_shapes=[pltpu.VMEM((n_cmp, HEADS_PER_GROUP * ts), F32) for _ in range(tq // ts)],
        compiler_params=_params(("parallel", "parallel", "parallel")),
        name="cmp_attn",
    )(feat_t, kc, vc_t, gates_t)


def _sel_attn_body(q_ref, k_ref, e_ref, v_ref, sel_ref, cos_ref, sin_ref, gates_ref, oin_ref, o_ref,
                   m_ref, l_ref, acc_ref, sa_ref, sb_ref, *, tq, tk):
    g = pl.program_id(1)
    qi = pl.program_id(2)
    t0 = qi * tq
    q = _rope_feature_major(_load_q_t(q_ref), _lane_tile(cos_ref[...], HEADS_PER_GROUP),
                            _lane_tile(sin_ref[...], HEADS_PER_GROUP))
    unselected = _lane_tile(sel_ref[...] - 1.0, HEADS_PER_GROUP)
    q_aug = jnp.concatenate([q, unselected.astype(BF16)], axis=0)

    m_ref[...] = jnp.full(m_ref.shape, MASK_VALUE, F32)
    l_ref[...] = jnp.zeros(l_ref.shape, F32)
    acc_ref[...] = jnp.zeros(acc_ref.shape, F32)

    def scores(ki):
        k0 = pl.multiple_of(ki * tk, tk)
        k_aug = jnp.concatenate([k_ref[pl.ds(k0, tk), :], e_ref[pl.ds(k0, tk), :]], axis=1)
        return _dot(k_aug, q_aug)

    def accumulate(s, ki, causal):
        k0 = pl.multiple_of(ki * tk, tk)
        if causal:
            kpos = k0 + lax.broadcasted_iota(jnp.int32, (tk, tq), 0)
            tpos = t0 + lax.broadcasted_iota(jnp.int32, (tk, tq), 1)
            s = s + _lane_tile(jnp.where(kpos <= tpos, 0.0, MASK_VALUE), HEADS_PER_GROUP)
        m_prev = m_ref[...]
        m_new = jnp.maximum(m_prev, jnp.max(s, axis=0, keepdims=True))
        alpha = jnp.exp2(m_prev - m_new)
        p = jnp.exp2(s - m_new)
        l_ref[...] = alpha * l_ref[...] + jnp.sum(p, axis=0, keepdims=True)
        acc_ref[...] = alpha * acc_ref[...] + _dot(v_ref[:, pl.ds(k0, tk)], p.astype(BF16))
        m_ref[...] = m_new

    n_before = t0 // tk
    sa_ref[...] = scores(0)

    def body(j, carry):
        sb_ref[...] = scores(2 * j + 1)
        accumulate(sa_ref[...], 2 * j, causal=False)
        sa_ref[...] = scores(2 * j + 2)
        accumulate(sb_ref[...], 2 * j + 1, causal=False)
        return carry

    lax.fori_loop(0, n_before // 2, body, 0)

    @pl.when(n_before % 2 == 1)
    def _():
        sb_ref[...] = scores(n_before)
        accumulate(sa_ref[...], n_before - 1, causal=False)
        accumulate(sb_ref[...], n_before, causal=True)

    @pl.when(n_before % 2 == 0)
    def _():
        accumulate(sa_ref[...], n_before, causal=True)

    o = acc_ref[...] * (1.0 / l_ref[...])
    for r in range(HEADS_PER_GROUP):
        rows = slice(r * HEAD_DIM, (r + 1) * HEAD_DIM)
        o_ref[rows, :] = oin_ref[rows, :] + _gate_row(gates_ref, 1, g, r) * o[:, r * tq:(r + 1) * tq]


def _sel_attn(feat_t, tok, selm, cos_t, sin_t, gates_t, oin, batch, seq, tq, tk):
    t = feat_t.shape[1]
    nq = seq // tq
    lanes = HEADS_PER_GROUP * tq
    v_row0 = NSA_HEADS
    assert tk % tq == 0 and seq % tk == 0
    key_block = jnp.arange(seq, dtype=jnp.int32)[:, None] // SEL_BLOCK
    block_of_key = jnp.where(key_block == jnp.arange(LANES, dtype=jnp.int32)[None, :], BLOCK_MASK_BIG, 0.0)
    return pl.pallas_call(
        functools.partial(_sel_attn_body, tq=tq, tk=tk),
        grid=(batch, NSA_GROUPS, nq),
        in_specs=[
            pl.BlockSpec((GROUP_WIDTH, tq), lambda b, g, i: (g, b * nq + i)),
            pl.BlockSpec((None, seq, HEAD_DIM), lambda b, g, i: (g, b, 0)),
            pl.BlockSpec((seq, LANES), lambda b, g, i: (0, 0)),
            pl.BlockSpec((HEAD_DIM, seq), lambda b, g, i: (v_row0 + g, b)),
            pl.BlockSpec((None, LANES, tq), lambda b, g, i: (g, 0, b * nq + i)),
            pl.BlockSpec((ROPE_HALF, tq), lambda b, g, i: (0, i)),
            pl.BlockSpec((ROPE_HALF, tq), lambda b, g, i: (0, i)),
            pl.BlockSpec((LANES, tq), lambda b, g, i: (0, b * nq + i)),
            pl.BlockSpec((GROUP_WIDTH, tq), lambda b, g, i: (g, b * nq + i)),
        ],
        out_specs=pl.BlockSpec((GROUP_WIDTH, tq), lambda b, g, i: (g, b * nq + i)),
        out_shape=jax.ShapeDtypeStruct((D_MODEL, t), F32),
        scratch_shapes=[
            pltpu.VMEM((1, lanes), F32),
            pltpu.VMEM((1, lanes), F32),
            pltpu.VMEM((HEAD_DIM, lanes), F32),
            pltpu.VMEM((tk, lanes), F32),
            pltpu.VMEM((tk, lanes), F32),
        ],
        compiler_params=_params(("parallel", "parallel", "parallel")),
        name="attn_sel",
    )(feat_t, tok, block_of_key.astype(BF16), feat_t, selm, cos_t, sin_t, gates_t, oin)


def _win_attn_body(q_ref, k_ref, v_ref, bias_ref, cos_ref, sin_ref, gates_ref, oin_ref, o_ref, *score_refs, tq):
    g = pl.program_id(1)
    n_sub = len(score_refs)
    ts = tq // n_sub
    band = WINDOW + ts
    starts = []
    for u, s_ref in enumerate(score_refs):
        t0 = pl.program_id(2) * tq + u * ts
        lanes = slice(u * ts, (u + 1) * ts)
        q = _rope_feature_major(_load_q_t(q_ref, u * ts, ts), _lane_tile(cos_ref[:, lanes], HEADS_PER_GROUP),
                                _lane_tile(sin_ref[:, lanes], HEADS_PER_GROUP))
        start = pl.multiple_of(jnp.maximum(t0 - WINDOW, 0), ts)
        s_ref[...] = _dot(k_ref[pl.ds(start, band), :], q)
        starts.append((t0, start, lanes))
    for s_ref, (t0, start, lanes) in zip(score_refs, starts):
        bias = bias_ref[jnp.minimum(t0 // ts, WINDOW // ts)]
        s = s_ref[...] + _lane_tile(bias, HEADS_PER_GROUP)
        m = jnp.max(s, axis=0, keepdims=True)
        p = jnp.exp2(s - m)
        inv = 1.0 / jnp.sum(p, axis=0, keepdims=True)
        o = _dot(v_ref[:, pl.ds(start, band)], p.astype(BF16)) * inv
        for r in range(HEADS_PER_GROUP):
            rows = slice(r * HEAD_DIM, (r + 1) * HEAD_DIM)
            total = oin_ref[rows, lanes] + _gate_row(gates_ref, 2, g, r)[:, lanes] * o[:, r * ts:(r + 1) * ts]
            o_ref[lanes, rows] = total.T.astype(BF16)


def _win_attn(feat_t, tok, cos_t, sin_t, gates_t, oin, batch, seq, tq, ts):
    t = feat_t.shape[1]
    nq = seq // tq
    v_row0 = NSA_HEADS + NSA_GROUPS
    assert seq >= WINDOW + ts and tq % ts == 0 and WINDOW % ts == 0
    band = WINDOW + ts
    rel = jnp.arange(band, dtype=jnp.int32)[None, :, None] - jnp.arange(ts, dtype=jnp.int32)[None, None, :]
    d = (jnp.arange(WINDOW // ts + 1, dtype=jnp.int32) * ts)[:, None, None]
    win_bias = jnp.where((rel <= d) & (rel > d - WINDOW), 0.0, MASK_VALUE).astype(F32)
    return pl.pallas_call(
        functools.partial(_win_attn_body, tq=tq),
        grid=(batch, NSA_GROUPS, nq),
        in_specs=[
            pl.BlockSpec((GROUP_WIDTH, tq), lambda b, g, i: (g, b * nq + i)),
            pl.BlockSpec((None, seq, HEAD_DIM), lambda b, g, i: (NSA_GROUPS + g, b, 0)),
            pl.BlockSpec((HEAD_DIM, seq), lambda b, g, i: (v_row0 + g, b)),
            pl.BlockSpec((WINDOW // ts + 1, band, ts), lambda b, g, i: (0, 0, 0)),
            pl.BlockSpec((ROPE_HALF, tq), lambda b, g, i: (0, i)),
            pl.BlockSpec((ROPE_HALF, tq), lambda b, g, i: (0, i)),
            pl.BlockSpec((LANES, tq), lambda b, g, i: (0, b * nq + i)),
            pl.BlockSpec((GROUP_WIDTH, tq), lambda b, g, i: (g, b * nq + i)),
        ],
        out_specs=pl.BlockSpec((tq, GROUP_WIDTH), lambda b, g, i: (b * nq + i, g)),
        out_shape=jax.ShapeDtypeStruct((t, D_MODEL), BF16),
        scratch_shapes=[pltpu.VMEM((WINDOW + ts, HEADS_PER_GROUP * ts), F32) for _ in range(tq // ts)],
        compiler_params=_params(("parallel", "parallel", "parallel")),
        name="attn_win",
    )(feat_t, tok, feat_t, win_bias, cos_t, sin_t, gates_t, oin)


def _nsa_mixer(h, norm_rows, layer, layer_j, batch, seq, w_in, w_out, phi_pe, phi_w1, phi_b1, phi_w2):
    t = h.shape[0]
    w_in_t = jnp.swapaxes(w_in, 1, 2)
    w_gate_t = jnp.pad(w_in_t[layer_j, QKV_WIDTH:, :], ((0, LANES - N_GATES), (0, 0)))
    cos_tok, sin_tok, cos_t, sin_t = _rope_tables(seq)

    feat_t, gates_t, xn = _nsa_inproj_feat(h, norm_rows, layer, w_in_t, layer_j, w_gate_t, tm=1024)
    x_cmp, tok = _nsa_inproj_tok(xn, w_in_t, layer_j, cos_tok, sin_tok, seq, tm=min(1024, seq))

    pe = phi_pe.reshape(phi_pe.shape[0], 2, 1, CMP_BLOCK * HEAD_DIM)
    b1 = phi_b1.reshape(phi_b1.shape[0], 2, 1, CMP_HIDDEN)
    kc = _compress(x_cmp, 0, pe, phi_w1, b1, phi_w2, layer_j, batch, feature_major=False)
    vc_t = _compress(x_cmp, 1, pe, phi_w1, b1, phi_w2[layer_j, 1].T, layer_j, batch, feature_major=True)

    o1, selm = _cmp_attn(feat_t, kc, vc_t, gates_t, batch, seq, tq=1024, ts=256)
    o2 = _sel_attn(feat_t, tok, selm, cos_t, sin_t, gates_t, o1, batch, seq, tq=256, tk=512)
    o3 = _win_attn(feat_t, tok, cos_t, sin_t, gates_t, o2, batch, seq, tq=1024, ts=256)
    return _outproj(o3, w_out, layer_j, h, norm_rows, layer, 3, tm=min(1024, t), tn=512)


def _conv_inproj_body(h_ref, halo_ref, g_ref, wb_ref, wc_ref, wu_ref, cw_ref, o_ref, xn_ref, *, tiles_per_seq):
    i = pl.program_id(0)
    j = pl.program_id(1)

    @pl.when(j == 0)
    def _():
        keep = jnp.where(i % tiles_per_seq == 0, 0.0, 1.0)
        xn_ref[0:CONV_HALO] = (_rms(halo_ref[...], g_ref[...]) * keep).astype(BF16)
        xn_ref[CONV_HALO:] = _rms(h_ref[...], g_ref[...]).astype(BF16)

    x = xn_ref[...]
    bg = _dot(x, wb_ref[...].astype(BF16))
    z = _dot(x, wc_ref[...].astype(BF16)) * _dot(x, wu_ref[...].astype(BF16))
    cw = cw_ref[...]
    conv = cw[2:3] * z + cw[1:2] * pltpu.roll(z, 1, axis=0) + cw[0:1] * pltpu.roll(z, 2, axis=0)
    o_ref[...] = (bg * conv)[CONV_HALO:].astype(BF16)


def _conv_inproj(h, norm_rows, layer, w_in, conv_w, layer_j, seq, tm, tn):
    t = h.shape[0]
    nj = D_MODEL // tn
    halo_blocks = tm // CONV_HALO
    return pl.pallas_call(
        functools.partial(_conv_inproj_body, tiles_per_seq=seq // tm),
        grid=(t // tm, nj),
        in_specs=[
            pl.BlockSpec((tm, D_MODEL), lambda i, j: (i, 0)),
            pl.BlockSpec((CONV_HALO, D_MODEL), lambda i, j: (jnp.maximum(i * halo_blocks - 1, 0), 0)),
            _norm_row_spec(layer, 2),
            pl.BlockSpec((None, D_MODEL, tn), lambda i, j: (layer_j, 0, j)),
            pl.BlockSpec((None, D_MODEL, tn), lambda i, j: (layer_j, 0, nj + j)),
            pl.BlockSpec((None, D_MODEL, tn), lambda i, j: (layer_j, 0, 2 * nj + j)),
            pl.BlockSpec((None, CONV_WIDTH, tn), lambda i, j: (layer_j, 0, j)),
        ],
        out_specs=pl.BlockSpec((tm, tn), lambda i, j: (i, j)),
        out_shape=jax.ShapeDtypeStruct((t, D_MODEL), BF16),
        scratch_shapes=[pltpu.VMEM((tm + CONV_HALO, D_MODEL), BF16)],
        compiler_params=_params(("parallel", "arbitrary")),
        name="conv_inproj",
    )(h, h, norm_rows, w_in, w_in, w_in, conv_w)


def _sgu_inproj_body(h_ref, g_ref, w_ref, lng_ref, lnb_ref, ws_ref, bs_ref, o_ref, xn_ref, z_ref, *, tm, tn):
    j = pl.program_id(1)

    @pl.when(j == 0)
    def _():
        xn_ref[...] = _rms(h_ref[...], g_ref[...]).astype(BF16)

    col = pl.multiple_of(j * tn, tn)
    z_ref[:, pl.ds(col, tn)] = jax.nn.gelu(_dot(xn_ref[...], w_ref[...].astype(BF16)))

    @pl.when(j == pl.num_programs(1) - 1)
    def _():
        row = lax.broadcasted_iota(jnp.int32, (SGU_CHUNK, SGU_CHUNK), 0)
        colm = lax.broadcasted_iota(jnp.int32, (SGU_CHUNK, SGU_CHUNK), 1)
        ws = [jnp.where(colm <= row, ws_ref[grp], 0.0).astype(BF16) for grp in range(SGU_GROUPS)]
        bs = bs_ref[...]
        lng, lnb = lng_ref[...], lnb_ref[...]
        for c in range(tm // SGU_CHUNK):
            rws = slice(c * SGU_CHUNK, (c + 1) * SGU_CHUNK)
            v = z_ref[rws, D_MODEL:]
            mu = jnp.mean(v, axis=-1, keepdims=True)
            var = jnp.mean(jnp.square(v - mu), axis=-1, keepdims=True)
            vn = ((v - mu) * lax.rsqrt(var + EPS) * lng + lnb).astype(BF16)
            for grp in range(SGU_GROUPS):
                cols = slice(grp * SGU_GROUP_DIM, (grp + 1) * SGU_GROUP_DIM)
                sv = _dot(ws[grp], vn[:, cols]) + bs[:, grp:grp + 1]
                o_ref[rws, cols] = (z_ref[rws, cols] * sv).astype(BF16)


def _sgu_inproj(h, norm_rows, layer, w_in, ln_g, ln_b, w_s, b_s_t, layer_j, tm, tn):
    t = h.shape[0]
    return pl.pallas_call(
        functools.partial(_sgu_inproj_body, tm=tm, tn=tn),
        grid=(t // tm, 2 * D_MODEL // tn),
        in_specs=[
            pl.BlockSpec((tm, D_MODEL), lambda i, j: (i, 0)),
            _norm_row_spec(layer, 2),
            pl.BlockSpec((None, D_MODEL, tn), lambda i, j: (layer_j, 0, j)),
            pl.BlockSpec((None, 1, D_MODEL), lambda i, j: (layer_j, 0, 0)),
            pl.BlockSpec((None, 1, D_MODEL), lambda i, j: (layer_j, 0, 0)),
            pl.BlockSpec((None, SGU_GROUPS, SGU_CHUNK, SGU_CHUNK), lambda i, j: (layer_j, 0, 0, 0)),
            pl.BlockSpec((None, SGU_CHUNK, SGU_GROUPS), lambda i, j: (layer_j, 0, 0)),
        ],
        out_specs=pl.BlockSpec((tm, D_MODEL), lambda i, j: (i, 0)),
        out_shape=jax.ShapeDtypeStruct((t, D_MODEL), BF16),
        scratch_shapes=[pltpu.VMEM((tm, D_MODEL), BF16), pltpu.VMEM((tm, 2 * D_MODEL), F32)],
        compiler_params=_params(("parallel", "arbitrary")),
        name="sgu_inproj",
    )(h, norm_rows, w_in, ln_g, ln_b, w_s, b_s_t)


def kernel(x, p, norm_g, ffn1_wg, ffn1_wu, ffn1_wd, ffn2_wg, ffn2_wu, ffn2_wd, ple_wg, ple_wp, nsa_w_in, nsa_w_out, nsa_phi_pe, nsa_phi_w1, nsa_phi_b1, nsa_phi_w2, conv_w_in, conv_w, conv_w_out, sgu_w_in, sgu_ln_g, sgu_ln_b, sgu_w_s, sgu_b_s, sgu_w_out):
    batch, seq, d = x.shape
    depth = p.shape[0]
    t = batch * seq
    assert d == D_MODEL and seq % 512 == 0
    h = x.reshape(t, d)
    p2 = p.reshape(depth, t, PLE_DIM)
    norm_rows = norm_g.reshape(depth * N_NORMS, 1, d)
    sgu_ln_g3 = sgu_ln_g.reshape(-1, 1, d)
    sgu_ln_b3 = sgu_ln_b.reshape(-1, 1, d)
    sgu_b_s_t = jnp.swapaxes(sgu_b_s, 1, 2)
    tm_big = min(1024, t)

    for layer in range(depth):
        layer_j = layer // N_MIXERS
        h = _ffn(h, norm_rows, layer, 0, 1, ffn1_wg, ffn1_wu, ffn1_wd, tm=tm_big, tf=256)
        kind = layer % N_MIXERS
        if kind == 0:
            h = _nsa_mixer(h, norm_rows, layer, layer_j, batch, seq, nsa_w_in, nsa_w_out, nsa_phi_pe,
                           nsa_phi_w1, nsa_phi_b1, nsa_phi_w2)
        elif kind == 1:
            a = _conv_inproj(h, norm_rows, layer, conv_w_in, conv_w, layer_j, seq, tm=min(1024, seq), tn=512)
            h = _outproj(a, conv_w_out, layer_j, h, norm_rows, layer, 3, tm=tm_big, tn=512)
        else:
            a = _sgu_inproj(h, norm_rows, layer, sgu_w_in, sgu_ln_g3, sgu_ln_b3, sgu_w_s, sgu_b_s_t, layer_j,
                            tm=tm_big, tn=256)
            h = _outproj(a, sgu_w_out, layer_j, h, norm_rows, layer, 3, tm=tm_big, tn=512)
        h = _ffn(h, norm_rows, layer, 4, 5, ffn2_wg, ffn2_wu, ffn2_wd, tm=tm_big, tf=256)
        h = _ple(h, p2, norm_rows, layer, ple_wg, ple_wp, tm=tm_big, tn=512)
    return h.reshape(batch, seq, d)
```

```python
import functools
import math

import jax
import jax.numpy as jnp
from jax import lax
from jax.experimental import pallas as pl
from jax.experimental.pallas import tpu as pltpu

F32 = jnp.float32
BF16 = jnp.bfloat16

EPS = 1e-6
LANES = 128
D_MODEL = 2048
D_FF = 5632
PLE_DIM = 256
N_NORMS = 8
N_MIXERS = 3
NSA_HEADS = 16
NSA_GROUPS = 4
HEADS_PER_GROUP = NSA_HEADS // NSA_GROUPS
HEAD_DIM = D_MODEL // NSA_HEADS
GROUP_WIDTH = HEADS_PER_GROUP * HEAD_DIM
KV_WIDTH = NSA_GROUPS * HEAD_DIM
ROPE_DIM = HEAD_DIM // 4
ROPE_HALF = ROPE_DIM // 2
ROPE_THETA = 500000.0
CMP_BLOCK = 32
CMP_STRIDE = 16
CMP_HIDDEN = 256
SEL_BLOCK = 64
SEL_TOP = 16
WINDOW = 512
QKV_WIDTH = NSA_HEADS * HEAD_DIM + 6 * KV_WIDTH
N_GATES = 3 * NSA_HEADS
SGU_CHUNK = 128
SGU_GROUPS = 8
SGU_GROUP_DIM = D_MODEL // SGU_GROUPS
CONV_WIDTH = 3
CONV_HALO = 16

MASK_VALUE = -1e30
BLOCK_MASK_BIG = 2.0 ** 100
SCORE_SCALE = HEAD_DIM ** -0.5
EXP2_SCALE = SCORE_SCALE * math.log2(math.e)
VMEM_LIMIT = 60 * 1024 * 1024


def _params(semantics):
    return pltpu.CompilerParams(dimension_semantics=semantics, vmem_limit_bytes=VMEM_LIMIT)


def _rms(x, g):
    return x * lax.rsqrt(jnp.mean(x * x, axis=-1, keepdims=True) + EPS) * g


def _rms_ref(src_ref, inv_ref, g_ref):
    x = src_ref[...]
    inv_ref[...] = lax.rsqrt(jnp.mean(x * x, axis=-1, keepdims=True) + EPS)
    return src_ref[...] * inv_ref[...] * g_ref[...]


def _inv_scratch(rows):
    return pltpu.VMEM((rows, 1), F32)


def _dot(a, b):
    return jnp.dot(a, b, preferred_element_type=F32)


def _dot_nt(a, b):
    return lax.dot_general(a, b, (((1,), (1,)), ((), ())), preferred_element_type=F32)


def _norm_row_spec(layer, k):
    idx = layer * N_NORMS + k
    return pl.BlockSpec((None, 1, D_MODEL), lambda i, j: (idx, 0, 0))


def _ffn_body(h_ref, gpre_ref, gpost_ref, wg_ref, wu_ref, wd_ref, o_ref, xn_ref, inv_ref):
    f = pl.program_id(1)

    @pl.when(f == 0)
    def _():
        xn_ref[...] = _rms_ref(h_ref, inv_ref, gpre_ref).astype(BF16)
        o_ref[...] = jnp.zeros_like(o_ref)

    x = xn_ref[...]
    gate = _dot(x, wg_ref[...].astype(BF16))
    up = _dot(x, wu_ref[...].astype(BF16))
    act = (gate * jax.nn.sigmoid(gate) * up).astype(BF16)
    o_ref[...] += _dot(act, wd_ref[...].astype(BF16))

    @pl.when(f == pl.num_programs(1) - 1)
    def _():
        o_ref[...] = h_ref[...] + 0.5 * _rms_ref(o_ref, inv_ref, gpost_ref)


def _ffn(h, norm_rows, layer, k_pre, k_post, wg, wu, wd, tm, tf):
    t = h.shape[0]
    return pl.pallas_call(
        _ffn_body,
        grid=(t // tm, D_FF // tf),
        in_specs=[
            pl.BlockSpec((tm, D_MODEL), lambda i, f: (i, 0)),
            _norm_row_spec(layer, k_pre),
            _norm_row_spec(layer, k_post),
            pl.BlockSpec((None, D_MODEL, tf), lambda i, f: (layer, 0, f)),
            pl.BlockSpec((None, D_MODEL, tf), lambda i, f: (layer, 0, f)),
            pl.BlockSpec((None, tf, D_MODEL), lambda i, f: (layer, f, 0)),
        ],
        out_specs=pl.BlockSpec((tm, D_MODEL), lambda i, f: (i, 0)),
        out_shape=jax.ShapeDtypeStruct((t, D_MODEL), F32),
        scratch_shapes=[pltpu.VMEM((tm, D_MODEL), BF16), _inv_scratch(tm)],
        compiler_params=_params(("parallel", "arbitrary")),
        name="ffn",
    )(h, norm_rows, norm_rows, wg, wu, wd)


def _outproj_body(a_ref, w_ref, h_ref, g_ref, o_ref, inv_ref, *, tn):
    j = pl.program_id(1)
    col = pl.multiple_of(j * tn, tn)
    o_ref[:, pl.ds(col, tn)] = _dot(a_ref[...], w_ref[...].astype(BF16))

    @pl.when(j == pl.num_programs(1) - 1)
    def _():
        o_ref[...] = h_ref[...] + _rms_ref(o_ref, inv_ref, g_ref)


def _outproj(a, w, w_layer, h, norm_rows, layer, k_norm, tm, tn):
    t, kdim = a.shape
    return pl.pallas_call(
        functools.partial(_outproj_body, tn=tn),
        grid=(t // tm, D_MODEL // tn),
        in_specs=[
            pl.BlockSpec((tm, kdim), lambda i, j: (i, 0)),
            pl.BlockSpec((None, kdim, tn), lambda i, j: (w_layer, 0, j)),
            pl.BlockSpec((tm, D_MODEL), lambda i, j: (i, 0)),
            _norm_row_spec(layer, k_norm),
        ],
        out_specs=pl.BlockSpec((tm, D_MODEL), lambda i, j: (i, 0)),
        out_shape=jax.ShapeDtypeStruct((t, D_MODEL), F32),
        scratch_shapes=[_inv_scratch(tm)],
        compiler_params=_params(("parallel", "arbitrary")),
        name="outproj",
    )(a, w, h, norm_rows)


def _ple_body(h_ref, p_ref, gpre_ref, gpost_ref, wg_ref, wp_ref, o_ref, xn_ref, pb_ref, inv_ref, *, tn):
    j = pl.program_id(1)

    @pl.when(j == 0)
    def _():
        xn_ref[...] = _rms_ref(h_ref, inv_ref, gpre_ref).astype(BF16)
        pb_ref[...] = p_ref[...].astype(BF16)

    gate = jax.nn.sigmoid(_dot(xn_ref[...], wg_ref[...].astype(BF16)))
    emb = _dot(pb_ref[...], wp_ref[...].astype(BF16))
    col = pl.multiple_of(j * tn, tn)
    o_ref[:, pl.ds(col, tn)] = gate * emb

    @pl.when(j == pl.num_programs(1) - 1)
    def _():
        o_ref[...] = h_ref[...] + _rms_ref(o_ref, inv_ref, gpost_ref)


def _ple(h, p, norm_rows, layer, wg, wp, tm, tn):
    t = h.shape[0]
    return pl.pallas_call(
        functools.partial(_ple_body, tn=tn),
        grid=(t // tm, D_MODEL // tn),
        in_specs=[
            pl.BlockSpec((tm, D_MODEL), lambda i, j: (i, 0)),
            pl.BlockSpec((None, tm, PLE_DIM), lambda i, j: (layer, i, 0)),
            _norm_row_spec(layer, 6),
            _norm_row_spec(layer, 7),
            pl.BlockSpec((None, D_MODEL, tn), lambda i, j: (layer, 0, j)),
            pl.BlockSpec((None, PLE_DIM, tn), lambda i, j: (layer, 0, j)),
        ],
        out_specs=pl.BlockSpec((tm, D_MODEL), lambda i, j: (i, 0)),
        out_shape=jax.ShapeDtypeStruct((t, D_MODEL), F32),
        scratch_shapes=[pltpu.VMEM((tm, D_MODEL), BF16), pltpu.VMEM((tm, PLE_DIM), BF16), _inv_scratch(tm)],
        compiler_params=_params(("parallel", "arbitrary")),
        name="ple",
    )(h, p, norm_rows, norm_rows, wg, wp)


def _nsa_inproj_feat_body(h_ref, g_ref, w_ref, wg_ref, feat_ref, gates_ref, xn_ref, inv_ref, *, n_q_tiles):
    j = pl.program_id(1)

    @pl.when(j == 0)
    def _():
        xn = _rms_ref(h_ref, inv_ref, g_ref).astype(BF16)
        xn_ref[...] = xn
        gates_ref[...] = jax.nn.sigmoid(_dot_nt(wg_ref[...].astype(BF16), xn))

    row_scale = jnp.where(j < n_q_tiles, EXP2_SCALE, 1.0)
    feat_ref[...] = (_dot_nt(w_ref[...].astype(BF16), xn_ref[...]) * row_scale).astype(BF16)


def _nsa_inproj_feat(h, norm_rows, layer, w_in_t, w_layer, w_gate_t, tm):
    t = h.shape[0]
    tn = KV_WIDTH
    n_q_tiles = NSA_HEADS * HEAD_DIM // tn
    n_feat = NSA_HEADS * HEAD_DIM + 2 * KV_WIDTH
    return pl.pallas_call(
        functools.partial(_nsa_inproj_feat_body, n_q_tiles=n_q_tiles),
        grid=(t // tm, n_feat // tn),
        in_specs=[
            pl.BlockSpec((tm, D_MODEL), lambda i, j: (i, 0)),
            _norm_row_spec(layer, 2),
            pl.BlockSpec((None, tn, D_MODEL),
                         lambda i, j: (w_layer, j + 3 * (j // n_q_tiles) + j // (n_q_tiles + 1), 0)),
            pl.BlockSpec((LANES, D_MODEL), lambda i, j: (0, 0)),
        ],
        out_specs=[
            pl.BlockSpec((tn, tm), lambda i, j: (j, i)),
            pl.BlockSpec((LANES, tm), lambda i, j: (0, i)),
            pl.BlockSpec((tm, D_MODEL), lambda i, j: (i, 0)),
        ],
        out_shape=[
            jax.ShapeDtypeStruct((n_feat, t), BF16),
            jax.ShapeDtypeStruct((LANES, t), F32),
            jax.ShapeDtypeStruct((t, D_MODEL), BF16),
        ],
        scratch_shapes=[_inv_scratch(tm)],
        compiler_params=_params(("parallel", "arbitrary")),
        name="nsa_inproj_feat",
    )(h, norm_rows, w_in_t, w_gate_t)


def _rope_token_major(x, cos, sin):
    lane = lax.broadcasted_iota(jnp.int32, x.shape, 1)
    partner = jnp.where(lane < ROPE_HALF, pltpu.roll(x, LANES - ROPE_HALF, axis=1),
                        pltpu.roll(x, ROPE_HALF, axis=1))
    return x * cos + partner * sin


def _nsa_inproj_tok_body(xn_ref, wc_ref, wk_ref, cos_ref, sin_ref, oc_ref, ok_ref):
    xn = xn_ref[...]
    res_c = _dot_nt(xn, wc_ref[...].astype(BF16))
    res_k = _dot_nt(xn, wk_ref[...].astype(BF16))
    cos, sin = cos_ref[...], sin_ref[...]
    for c in range(NSA_GROUPS):
        cols = slice(c * LANES, (c + 1) * LANES)
        oc_ref[c] = res_c[:, cols]
        ok_ref[c] = _rope_token_major(res_k[:, cols], cos, sin).astype(BF16)


def _nsa_inproj_tok(xn, w_in_t, w_layer, cos_tok, sin_tok, seq, tm):
    t = xn.shape[0]
    per_seq = seq // tm
    q_blocks = NSA_HEADS * HEAD_DIM // KV_WIDTH
    slab_spec = pl.BlockSpec((NSA_GROUPS, tm, LANES), lambda i, j: (j, i, 0))
    return pl.pallas_call(
        _nsa_inproj_tok_body,
        grid=(t // tm, 2),
        in_specs=[
            pl.BlockSpec((tm, D_MODEL), lambda i, j: (i, 0)),
            pl.BlockSpec((None, KV_WIDTH, D_MODEL), lambda i, j: (w_layer, q_blocks + j, 0)),
            pl.BlockSpec((None, KV_WIDTH, D_MODEL), lambda i, j: (w_layer, q_blocks + 2 + 2 * j, 0)),
            pl.BlockSpec((tm, LANES), lambda i, j: (i % per_seq, 0)),
            pl.BlockSpec((tm, LANES), lambda i, j: (i % per_seq, 0)),
        ],
        out_specs=[slab_spec, slab_spec],
        out_shape=[
            jax.ShapeDtypeStruct((2 * NSA_GROUPS, t, LANES), F32),
            jax.ShapeDtypeStruct((2 * NSA_GROUPS, t, LANES), BF16),
        ],
        compiler_params=_params(("parallel", "arbitrary")),
        name="nsa_inproj_tok",
    )(xn, w_in_t, w_in_t, cos_tok, sin_tok)


def _rope_tables(seq):
    inv_freq = jnp.power(jnp.float32(ROPE_THETA), -jnp.arange(0, ROPE_DIM, 2, dtype=F32) / ROPE_DIM)
    ang = jnp.arange(seq, dtype=F32)[:, None] * inv_freq[None, :]
    cos, sin = jnp.cos(ang), jnp.sin(ang)
    cos_tok = jnp.concatenate([cos, cos, jnp.ones((seq, LANES - ROPE_DIM), F32)], axis=1)
    sin_tok = jnp.concatenate([-sin, sin, jnp.zeros((seq, LANES - ROPE_DIM), F32)], axis=1)
    return cos_tok, sin_tok, cos.T, sin.T


def _compress_body(x_ref, pe_ref, w1_ref, b1_ref, w2_ref, o_ref, *, feature_major):
    n_chunks = x_ref.shape[0] // CMP_STRIDE
    w1 = w1_ref[...].astype(BF16)
    both = jnp.zeros((n_chunks, 2 * CMP_HIDDEN), F32)
    for l in range(CMP_STRIDE):
        x_l = x_ref[pl.ds(l, n_chunks, stride=CMP_STRIDE), :].astype(BF16)
        lo, hi = l * HEAD_DIM, (CMP_STRIDE + l) * HEAD_DIM
        w_l = jnp.concatenate([w1[lo:lo + HEAD_DIM], w1[hi:hi + HEAD_DIM]], axis=1)
        both = both + _dot(x_l, w_l)
    first = both[:, :CMP_HIDDEN]
    second = pltpu.roll(both[:, CMP_HIDDEN:], n_chunks - 1, axis=0)
    pe = jnp.broadcast_to(pe_ref[...], (8, CMP_BLOCK * HEAD_DIM)).astype(BF16)
    const = _dot(pe, w1)[0:1]
    hid = first + second + const + b1_ref[...]
    act = (hid * jax.nn.sigmoid(hid)).astype(BF16)
    if feature_major:
        o_ref[...] = _dot_nt(w2_ref[...].astype(BF16), act).astype(BF16)
    else:
        o_ref[...] = _dot(act, w2_ref[...].astype(BF16)).astype(BF16)


def _compress(x, which, pe, w1, b1, w2, layer_j, batch, feature_major):
    seq = x.shape[1] // batch
    n_chunks = seq // CMP_STRIDE
    if feature_major:
        w2_spec = pl.BlockSpec((HEAD_DIM, CMP_HIDDEN), lambda g, b: (0, 0))
        out_block, out_dims = (None, None, HEAD_DIM, n_chunks), (NSA_GROUPS, batch, HEAD_DIM, n_chunks)
    else:
        w2_spec = pl.BlockSpec((None, None, CMP_HIDDEN, HEAD_DIM), lambda g, b: (layer_j, which, 0, 0))
        out_block, out_dims = (None, None, n_chunks, HEAD_DIM), (NSA_GROUPS, batch, n_chunks, HEAD_DIM)
    return pl.pallas_call(
        functools.partial(_compress_body, feature_major=feature_major),
        grid=(NSA_GROUPS, batch),
        in_specs=[
            pl.BlockSpec((None, seq, HEAD_DIM), lambda g, b: (which * NSA_GROUPS + g, b, 0)),
            pl.BlockSpec((None, None, 1, CMP_BLOCK * HEAD_DIM), lambda g, b: (layer_j, which, 0, 0)),
            pl.BlockSpec((None, None, CMP_BLOCK * HEAD_DIM, CMP_HIDDEN), lambda g, b: (layer_j, which, 0, 0)),
            pl.BlockSpec((None, None, 1, CMP_HIDDEN), lambda g, b: (layer_j, which, 0, 0)),
            w2_spec,
        ],
        out_specs=pl.BlockSpec(out_block, lambda g, b: (g, b, 0, 0)),
        out_shape=jax.ShapeDtypeStruct(out_dims, BF16),
        compiler_params=_params(("parallel", "parallel")),
        name="compress",
    )(x, pe, w1, b1, w2)


def _lane_tile(x, n):
    return jnp.concatenate([x] * n, axis=1)


def _load_q_t(q_ref, lo=0, n=None):
    n = q_ref.shape[1] if n is None else n
    return jnp.concatenate([q_ref[r * HEAD_DIM:(r + 1) * HEAD_DIM, lo:lo + n] for r in range(HEADS_PER_GROUP)],
                           axis=1)


def _rope_feature_major(q, cos, sin):
    x1 = q[0:ROPE_HALF].astype(F32)
    x2 = q[ROPE_HALF:ROPE_DIM].astype(F32)
    r1 = (x1 * cos - x2 * sin).astype(BF16)
    r2 = (x2 * cos + x1 * sin).astype(BF16)
    return jnp.concatenate([r1, r2, q[ROPE_DIM:]], axis=0)


def _split3(x):
    hi = x.astype(BF16)
    r1 = x - hi.astype(F32)
    mid = r1.astype(BF16)
    lo = (r1 - mid.astype(F32)).astype(BF16)
    return hi, mid, lo


def _gate_row(gates_ref, branch, g, r):
    return gates_ref[pl.ds(branch * NSA_HEADS + g * HEADS_PER_GROUP + r, 1), :]


def _cmp_attn_body(q_ref, kc_ref, vc_ref, gates_ref, o_ref, sel_ref, *score_refs, tq, n_cmp):
    n_sub = len(score_refs)
    ts = tq // n_sub
    kc = kc_ref[...]
    for u, s_ref in enumerate(score_refs):
        s_ref[...] = _dot(kc, _load_q_t(q_ref, u * ts, ts))
    for u, s_ref in enumerate(score_refs):
        _cmp_attn_subtile(s_ref[...], vc_ref, gates_ref, o_ref, sel_ref, pl.program_id(2) * tq + u * ts, u * ts,
                          ts, n_cmp)


def _cmp_attn_subtile(s, vc_ref, gates_ref, o_ref, sel_ref, t0, lane0, tq, n_cmp):
    g = pl.program_id(1)
    out_lanes = slice(lane0, lane0 + tq)

    tpos = t0 + lax.broadcasted_iota(jnp.int32, (n_cmp, tq), 1)
    cend = lax.broadcasted_iota(jnp.int32, (n_cmp, tq), 0) * CMP_STRIDE + (CMP_BLOCK - 1)
    visible = cend <= tpos
    bias = _lane_tile(jnp.where(visible, 0.0, MASK_VALUE), HEADS_PER_GROUP)
    okf = _lane_tile(jnp.where(visible, 1.0, 0.0), HEADS_PER_GROUP)
    s = s + bias
    m = jnp.max(s, axis=0, keepdims=True)
    e = jnp.exp2(s - m) * okf
    inv = 1.0 / jnp.maximum(jnp.sum(e, axis=0, keepdims=True), 1e-30)
    p = e * inv
    o = _dot(vc_ref[...], p.astype(BF16))

    for r in range(HEADS_PER_GROUP):
        gate = _gate_row(gates_ref, 0, g, r)[:, out_lanes]
        o_ref[r * HEAD_DIM:(r + 1) * HEAD_DIM, out_lanes] = gate * o[:, r * tq:(r + 1) * tq]

    psum = p[:, 0:tq]
    for r in range(1, HEADS_PER_GROUP):
        psum = psum + p[:, r * tq:(r + 1) * tq]
    n_sel_rows = LANES // 2
    jrow = lax.broadcasted_iota(jnp.int32, (n_sel_rows, n_cmp), 0)
    ccol = lax.broadcasted_iota(jnp.int32, (n_sel_rows, n_cmp), 1)
    c_lo, c_hi = ccol * CMP_STRIDE, ccol * CMP_STRIDE + (CMP_BLOCK - 1)
    j_lo, j_hi = jrow * SEL_BLOCK, jrow * SEL_BLOCK + (SEL_BLOCK - 1)
    ov = jnp.maximum(jnp.minimum(c_hi, j_hi) - jnp.maximum(c_lo, j_lo) + 1, 0).astype(F32) / CMP_STRIDE
    ov = ov.astype(BF16)
    hi, mid, lo = _split3(psum)
    imp = _dot(ov, hi) + _dot(ov, mid) + _dot(ov, lo)

    blk = lax.broadcasted_iota(jnp.int32, (n_sel_rows, tq), 0)
    cur = (t0 + lax.broadcasted_iota(jnp.int32, (n_sel_rows, tq), 1)) // SEL_BLOCK
    forced = (blk == 0) | (blk == cur) | (blk == cur - 1)
    key = jnp.where(blk > cur, -1.0, imp)
    key = jnp.where(forced, 1e30, key)
    sel = jnp.zeros((n_sel_rows, tq), F32)
    for _ in range(SEL_TOP):
        mx = jnp.max(key, axis=0, keepdims=True)
        first = jnp.min(jnp.where(key == mx, blk, LANES), axis=0, keepdims=True)
        pick = blk == first
        sel = jnp.where(pick, 1.0, sel)
        key = jnp.where(pick, -2.0, key)
    sel = jnp.where(blk > cur, 0.0, sel)
    sel_ref[:, out_lanes] = jnp.concatenate([sel, jnp.zeros_like(sel)], axis=0).astype(BF16)


def _cmp_attn(feat_t, kc, vc_t, gates_t, batch, seq, tq, ts):
    t = feat_t.shape[1]
    nq = seq // tq
    n_cmp = kc.shape[2]
    assert seq // SEL_BLOCK <= LANES // 2
    return pl.pallas_call(
        functools.partial(_cmp_attn_body, tq=tq, n_cmp=n_cmp),
        grid=(batch, NSA_GROUPS, nq),
        in_specs=[
            pl.BlockSpec((GROUP_WIDTH, tq), lambda b, g, i: (g, b * nq + i)),
            pl.BlockSpec((None, None, n_cmp, HEAD_DIM), lambda b, g, i: (g, b, 0, 0)),
            pl.BlockSpec((None, None, HEAD_DIM, n_cmp), lambda b, g, i: (g, b, 0, 0)),
            pl.BlockSpec((LANES, tq), lambda b, g, i: (0, b * nq + i)),
        ],
        out_specs=[
            pl.BlockSpec((GROUP_WIDTH, tq), lambda b, g, i: (g, b * nq + i)),
            pl.BlockSpec((None, LANES, tq), lambda b, g, i: (g, 0, b * nq + i)),
        ],
        out_shape=[
            jax.ShapeDtypeStruct((D_MODEL, t), F32),
            jax.ShapeDtypeStruct((NSA_GROUPS, LANES, t), BF16),
        ],
        scratch_shapes=[pltpu.VMEM((n_cmp, HEADS_PER_GROUP * ts), F32) for _ in range(tq // ts)],
        compiler_params=_params(("parallel", "parallel", "parallel")),
        name="cmp_attn",
    )(feat_t, kc, vc_t, gates_t)


def _sel_attn_body(q_ref, k_ref, e_ref, v_ref, sel_ref, cos_ref, sin_ref, gates_ref, oin_ref, o_ref,
                   m_ref, l_ref, acc_ref, sa_ref, sb_ref, *, tq, tk):
    g = pl.program_id(1)
    qi = pl.program_id(2)
    t0 = qi * tq
    q = _rope_feature_major(_load_q_t(q_ref), _lane_tile(cos_ref[...], HEADS_PER_GROUP),
                            _lane_tile(sin_ref[...], HEADS_PER_GROUP))
    unselected = _lane_tile(sel_ref[...] - 1.0, HEADS_PER_GROUP)
    q_aug = jnp.concatenate([q, unselected.astype(BF16)], axis=0)

    m_ref[...] = jnp.full(m_ref.shape, MASK_VALUE, F32)
    l_ref[...] = jnp.zeros(l_ref.shape, F32)
    acc_ref[...] = jnp.zeros(acc_ref.shape, F32)

    def scores(ki):
        k0 = pl.multiple_of(ki * tk, tk)
        k_aug = jnp.concatenate([k_ref[pl.ds(k0, tk), :], e_ref[pl.ds(k0, tk), :]], axis=1)
        return _dot(k_aug, q_aug)

    def accumulate(s, ki, causal):
        k0 = pl.multiple_of(ki * tk, tk)
        if causal:
            kpos = k0 + lax.broadcasted_iota(jnp.int32, (tk, tq), 0)
            tpos = t0 + lax.broadcasted_iota(jnp.int32, (tk, tq), 1)
            s = s + _lane_tile(jnp.where(kpos <= tpos, 0.0, MASK_VALUE), HEADS_PER_GROUP)
        m_prev = m_ref[...]
        m_new = jnp.maximum(m_prev, jnp.max(s, axis=0, keepdims=True))
        alpha = jnp.exp2(m_prev - m_new)
        p = jnp.exp2(s - m_new)
        l_ref[...] = alpha * l_ref[...] + jnp.sum(p, axis=0, keepdims=True)
        acc_ref[...] = alpha * acc_ref[...] + _dot(v_ref[:, pl.ds(k0, tk)], p.astype(BF16))
        m_ref[...] = m_new

    n_before = t0 // tk
    sa_ref[...] = scores(0)

    def body(j, carry):
        sb_ref[...] = scores(2 * j + 1)
        accumulate(sa_ref[...], 2 * j, causal=False)
        sa_ref[...] = scores(2 * j + 2)
        accumulate(sb_ref[...], 2 * j + 1, causal=False)
        return carry

    lax.fori_loop(0, n_before // 2, body, 0)

    @pl.when(n_before % 2 == 1)
    def _():
        sb_ref[...] = scores(n_before)
        accumulate(sa_ref[...], n_before - 1, causal=False)
        accumulate(sb_ref[...], n_before, causal=True)

    @pl.when(n_before % 2 == 0)
    def _():
        accumulate(sa_ref[...], n_before, causal=True)

    o = acc_ref[...] * (1.0 / l_ref[...])
    for r in range(HEADS_PER_GROUP):
        rows = slice(r * HEAD_DIM, (r + 1) * HEAD_DIM)
        o_ref[rows, :] = oin_ref[rows, :] + _gate_row(gates_ref, 1, g, r) * o[:, r * tq:(r + 1) * tq]


def _sel_attn(feat_t, tok, selm, cos_t, sin_t, gates_t, oin, batch, seq, tq, tk):
    t = feat_t.shape[1]
    nq = seq // tq
    lanes = HEADS_PER_GROUP * tq
    v_row0 = NSA_HEADS
    assert tk % tq == 0 and seq % tk == 0
    key_block = jnp.arange(seq, dtype=jnp.int32)[:, None] // SEL_BLOCK
    block_of_key = jnp.where(key_block == jnp.arange(LANES, dtype=jnp.int32)[None, :], BLOCK_MASK_BIG, 0.0)
    return pl.pallas_call(
        functools.partial(_sel_attn_body, tq=tq, tk=tk),
        grid=(batch, NSA_GROUPS, nq),
        in_specs=[
            pl.BlockSpec((GROUP_WIDTH, tq), lambda b, g, i: (g, b * nq + i)),
            pl.BlockSpec((None, seq, HEAD_DIM), lambda b, g, i: (g, b, 0)),
            pl.BlockSpec((seq, LANES), lambda b, g, i: (0, 0)),
            pl.BlockSpec((HEAD_DIM, seq), lambda b, g, i: (v_row0 + g, b)),
            pl.BlockSpec((None, LANES, tq), lambda b, g, i: (g, 0, b * nq + i)),
            pl.BlockSpec((ROPE_HALF, tq), lambda b, g, i: (0, i)),
            pl.BlockSpec((ROPE_HALF, tq), lambda b, g, i: (0, i)),
            pl.BlockSpec((LANES, tq), lambda b, g, i: (0, b * nq + i)),
            pl.BlockSpec((GROUP_WIDTH, tq), lambda b, g, i: (g, b * nq + i)),
        ],
        out_specs=pl.BlockSpec((GROUP_WIDTH, tq), lambda b, g, i: (g, b * nq + i)),
        out_shape=jax.ShapeDtypeStruct((D_MODEL, t), F32),
        scratch_shapes=[
            pltpu.VMEM((1, lanes), F32),
            pltpu.VMEM((1, lanes), F32),
            pltpu.VMEM((HEAD_DIM, lanes), F32),
            pltpu.VMEM((tk, lanes), F32),
            pltpu.VMEM((tk, lanes), F32),
        ],
        compiler_params=_params(("parallel", "parallel", "parallel")),
        name="attn_sel",
    )(feat_t, tok, block_of_key.astype(BF16), feat_t, selm, cos_t, sin_t, gates_t, oin)


def _win_attn_body(q_ref, k_ref, v_ref, bias_ref, cos_ref, sin_ref, gates_ref, oin_ref, o_ref, *score_refs, tq):
    g = pl.program_id(1)
    n_sub = len(score_refs)
    ts = tq // n_sub
    band = WINDOW + ts
    starts = []
    for u, s_ref in enumerate(score_refs):
        t0 = pl.program_id(2) * tq + u * ts
        lanes = slice(u * ts, (u + 1) * ts)
        q = _rope_feature_major(_load_q_t(q_ref, u * ts, ts), _lane_tile(cos_ref[:, lanes], HEADS_PER_GROUP),
                                _lane_tile(sin_ref[:, lanes], HEADS_PER_GROUP))
        start = pl.multiple_of(jnp.maximum(t0 - WINDOW, 0), ts)
        s_ref[...] = _dot(k_ref[pl.ds(start, band), :], q)
        starts.append((t0, start, lanes))
    for s_ref, (t0, start, lanes) in zip(score_refs, starts):
        bias = bias_ref[jnp.minimum(t0 // ts, WINDOW // ts)]
        s = s_ref[...] + _lane_tile(bias, HEADS_PER_GROUP)
        m = jnp.max(s, axis=0, keepdims=True)
        p = jnp.exp2(s - m)
        inv = 1.0 / jnp.sum(p, axis=0, keepdims=True)
        o = _dot(v_ref[:, pl.ds(start, band)], p.astype(BF16)) * inv
        for r in range(HEADS_PER_GROUP):
            rows = slice(r * HEAD_DIM, (r + 1) * HEAD_DIM)
            total = oin_ref[rows, lanes] + _gate_row(gates_ref, 2, g, r)[:, lanes] * o[:, r * ts:(r + 1) * ts]
            o_ref[lanes, rows] = total.T.astype(BF16)


def _win_attn(feat_t, tok, cos_t, sin_t, gates_t, oin, batch, seq, tq, ts):
    t = feat_t.shape[1]
    nq = seq // tq
    v_row0 = NSA_HEADS + NSA_GROUPS
    assert seq >= WINDOW + ts and tq % ts == 0 and WINDOW % ts == 0
    band = WINDOW + ts
    rel = jnp.arange(band, dtype=jnp.int32)[None, :, None] - jnp.arange(ts, dtype=jnp.int32)[None, None, :]
    d = (jnp.arange(WINDOW // ts + 1, dtype=jnp.int32) * ts)[:, None, None]
    win_bias = jnp.where((rel <= d) & (rel > d - WINDOW), 0.0, MASK_VALUE).astype(F32)
    return pl.pallas_call(
        functools.partial(_win_attn_body, tq=tq),
        grid=(batch, NSA_GROUPS, nq),
        in_specs=[
            pl.BlockSpec((GROUP_WIDTH, tq), lambda b, g, i: (g, b * nq + i)),
            pl.BlockSpec((None, seq, HEAD_DIM), lambda b, g, i: (NSA_GROUPS + g, b, 0)),
            pl.BlockSpec((HEAD_DIM, seq), lambda b, g, i: (v_row0 + g, b)),
            pl.BlockSpec((WINDOW // ts + 1, band, ts), lambda b, g, i: (0, 0, 0)),
            pl.BlockSpec((ROPE_HALF, tq), lambda b, g, i: (0, i)),
            pl.BlockSpec((ROPE_HALF, tq), lambda b, g, i: (0, i)),
            pl.BlockSpec((LANES, tq), lambda b, g, i: (0, b * nq + i)),
            pl.BlockSpec((GROUP_WIDTH, tq), lambda b, g, i: (g, b * nq + i)),
        ],
        out_specs=pl.BlockSpec((tq, GROUP_WIDTH), lambda b, g, i: (b * nq + i, g)),
        out_shape=jax.ShapeDtypeStruct((t, D_MODEL), BF16),
        scratch_shapes=[pltpu.VMEM((WINDOW + ts, HEADS_PER_GROUP * ts), F32) for _ in range(tq // ts)],
        compiler_params=_params(("parallel", "parallel", "parallel")),
        name="attn_win",
    )(feat_t, tok, feat_t, win_bias, cos_t, sin_t, gates_t, oin)


def _nsa_mixer(h, norm_rows, layer, layer_j, batch, seq, w_in, w_out, phi_pe, phi_w1, phi_b1, phi_w2):
    t = h.shape[0]
    w_in_t = jnp.swapaxes(w_in, 1, 2)
    w_gate_t = jnp.pad(w_in_t[layer_j, QKV_WIDTH:, :], ((0, LANES - N_GATES), (0, 0)))
    cos_tok, sin_tok, cos_t, sin_t = _rope_tables(seq)

    feat_t, gates_t, xn = _nsa_inproj_feat(h, norm_rows, layer, w_in_t, layer_j, w_gate_t, tm=1024)
    x_cmp, tok = _nsa_inproj_tok(xn, w_in_t, layer_j, cos_tok, sin_tok, seq, tm=min(1024, seq))

    pe = phi_pe.reshape(phi_pe.shape[0], 2, 1, CMP_BLOCK * HEAD_DIM)
    b1 = phi_b1.reshape(phi_b1.shape[0], 2, 1, CMP_HIDDEN)
    kc = _compress(x_cmp, 0, pe, phi_w1, b1, phi_w2, layer_j, batch, feature_major=False)
    vc_t = _compress(x_cmp, 1, pe, phi_w1, b1, phi_w2[layer_j, 1].T, layer_j, batch, feature_major=True)

    o1, selm = _cmp_attn(feat_t, kc, vc_t, gates_t, batch, seq, tq=1024, ts=256)
    o2 = _sel_attn(feat_t, tok, selm, cos_t, sin_t, gates_t, o1, batch, seq, tq=256, tk=512)
    o3 = _win_attn(feat_t, tok, cos_t, sin_t, gates_t, o2, batch, seq, tq=1024, ts=256)
    return _outproj(o3, w_out, layer_j, h, norm_rows, layer, 3, tm=min(1024, t), tn=512)


def _conv_inproj_body(h_ref, halo_ref, g_ref, wb_ref, wc_ref, wu_ref, cw_ref, o_ref, xn_ref, inv_ref, *,
                      tiles_per_seq):
    i = pl.program_id(0)
    j = pl.program_id(1)

    @pl.when(j == 0)
    def _():
        keep = jnp.where(i % tiles_per_seq == 0, 0.0, 1.0)
        xn_ref[0:CONV_HALO] = (_rms(halo_ref[...], g_ref[...]) * keep).astype(BF16)
        xn_ref[CONV_HALO:] = _rms_ref(h_ref, inv_ref, g_ref).astype(BF16)

    x = xn_ref[...]
    bg = _dot(x, wb_ref[...].astype(BF16))
    z = _dot(x, wc_ref[...].astype(BF16)) * _dot(x, wu_ref[...].astype(BF16))
    cw = cw_ref[...]
    conv = cw[2:3] * z + cw[1:2] * pltpu.roll(z, 1, axis=0) + cw[0:1] * pltpu.roll(z, 2, axis=0)
    o_ref[...] = (bg * conv)[CONV_HALO:].astype(BF16)


def _conv_inproj(h, norm_rows, layer, w_in, conv_w, layer_j, seq, tm, tn):
    t = h.shape[0]
    nj = D_MODEL // tn
    halo_blocks = tm // CONV_HALO
    return pl.pallas_call(
        functools.partial(_conv_inproj_body, tiles_per_seq=seq // tm),
        grid=(t // tm, nj),
        in_specs=[
            pl.BlockSpec((tm, D_MODEL), lambda i, j: (i, 0)),
            pl.BlockSpec((CONV_HALO, D_MODEL), lambda i, j: (jnp.maximum(i * halo_blocks - 1, 0), 0)),
            _norm_row_spec(layer, 2),
            pl.BlockSpec((None, D_MODEL, tn), lambda i, j: (layer_j, 0, j)),
            pl.BlockSpec((None, D_MODEL, tn), lambda i, j: (layer_j, 0, nj + j)),
            pl.BlockSpec((None, D_MODEL, tn), lambda i, j: (layer_j, 0, 2 * nj + j)),
            pl.BlockSpec((None, CONV_WIDTH, tn), lambda i, j: (layer_j, 0, j)),
        ],
        out_specs=pl.BlockSpec((tm, tn), lambda i, j: (i, j)),
        out_shape=jax.ShapeDtypeStruct((t, D_MODEL), BF16),
        scratch_shapes=[pltpu.VMEM((tm + CONV_HALO, D_MODEL), BF16), _inv_scratch(tm)],
        compiler_params=_params(("parallel", "arbitrary")),
        name="conv_inproj",
    )(h, h, norm_rows, w_in, w_in, w_in, conv_w)


def _sgu_inproj_body(h_ref, g_ref, w_ref, lng_ref, lnb_ref, ws_ref, bs_ref, o_ref, xn_ref, z_ref, inv_ref, *,
                     tm, tn):
    j = pl.program_id(1)

    @pl.when(j == 0)
    def _():
        xn_ref[...] = _rms_ref(h_ref, inv_ref, g_ref).astype(BF16)

    col = pl.multiple_of(j * tn, tn)
    z_ref[:, pl.ds(col, tn)] = jax.nn.gelu(_dot(xn_ref[...], w_ref[...].astype(BF16)))

    @pl.when(j == pl.num_programs(1) - 1)
    def _():
        row = lax.broadcasted_iota(jnp.int32, (SGU_CHUNK, SGU_CHUNK), 0)
        colm = lax.broadcasted_iota(jnp.int32, (SGU_CHUNK, SGU_CHUNK), 1)
        ws = [jnp.where(colm <= row, ws_ref[grp], 0.0).astype(BF16) for grp in range(SGU_GROUPS)]
        bs = bs_ref[...]
        lng, lnb = lng_ref[...], lnb_ref[...]
        for c in range(tm // SGU_CHUNK):
            rws = slice(c * SGU_CHUNK, (c + 1) * SGU_CHUNK)
            v = z_ref[rws, D_MODEL:]
            mu = jnp.mean(v, axis=-1, keepdims=True)
            var = jnp.mean(jnp.square(v - mu), axis=-1, keepdims=True)
            vn = ((v - mu) * lax.rsqrt(var + EPS) * lng + lnb).astype(BF16)
            for grp in range(SGU_GROUPS):
                cols = slice(grp * SGU_GROUP_DIM, (grp + 1) * SGU_GROUP_DIM)
                sv = _dot(ws[grp], vn[:, cols]) + bs[:, grp:grp + 1]
                o_ref[rws, cols] = (z_ref[rws, cols] * sv).astype(BF16)


def _sgu_inproj(h, norm_rows, layer, w_in, ln_g, ln_b, w_s, b_s_t, layer_j, tm, tn):
    t = h.shape[0]
    return pl.pallas_call(
        functools.partial(_sgu_inproj_body, tm=tm, tn=tn),
        grid=(t // tm, 2 * D_MODEL // tn),
        in_specs=[
            pl.BlockSpec((tm, D_MODEL), lambda i, j: (i, 0)),
            _norm_row_spec(layer, 2),
            pl.BlockSpec((None, D_MODEL, tn), lambda i, j: (layer_j, 0, j)),
            pl.BlockSpec((None, 1, D_MODEL), lambda i, j: (layer_j, 0, 0)),
            pl.BlockSpec((None, 1, D_MODEL), lambda i, j: (layer_j, 0, 0)),
            pl.BlockSpec((None, SGU_GROUPS, SGU_CHUNK, SGU_CHUNK), lambda i, j: (layer_j, 0, 0, 0)),
            pl.BlockSpec((None, SGU_CHUNK, SGU_GROUPS), lambda i, j: (layer_j, 0, 0)),
        ],
        out_specs=pl.BlockSpec((tm, D_MODEL), lambda i, j: (i, 0)),
        out_shape=jax.ShapeDtypeStruct((t, D_MODEL), BF16),
        scratch_shapes=[pltpu.VMEM((tm, D_MODEL), BF16), pltpu.VMEM((tm, 2 * D_MODEL), F32), _inv_scratch(tm)],
        compiler_params=_params(("parallel", "arbitrary")),
        name="sgu_inproj",
    )(h, norm_rows, w_in, ln_g, ln_b, w_s, b_s_t)


def kernel(x, p, norm_g, ffn1_wg, ffn1_wu, ffn1_wd, ffn2_wg, ffn2_wu, ffn2_wd, ple_wg, ple_wp, nsa_w_in, nsa_w_out, nsa_phi_pe, nsa_phi_w1, nsa_phi_b1, nsa_phi_w2, conv_w_in, conv_w, conv_w_out, sgu_w_in, sgu_ln_g, sgu_ln_b, sgu_w_s, sgu_b_s, sgu_w_out):
    batch, seq, d = x.shape
    depth = p.shape[0]
    t = batch * seq
    assert d == D_MODEL and seq % 512 == 0
    h = x.reshape(t, d)
    p2 = p.reshape(depth, t, PLE_DIM)
    norm_rows = norm_g.reshape(depth * N_NORMS, 1, d)
    sgu_ln_g3 = sgu_ln_g.reshape(-1, 1, d)
    sgu_ln_b3 = sgu_ln_b.reshape(-1, 1, d)
    sgu_b_s_t = jnp.swapaxes(sgu_b_s, 1, 2)
    tm_big = min(1024, t)

    for layer in range(depth):
        layer_j = layer // N_MIXERS
        h = _ffn(h, norm_rows, layer, 0, 1, ffn1_wg, ffn1_wu, ffn1_wd, tm=tm_big, tf=256)
        kind = layer % N_MIXERS
        if kind == 0:
            h = _nsa_mixer(h, norm_rows, layer, layer_j, batch, seq, nsa_w_in, nsa_w_out, nsa_phi_pe,
                           nsa_phi_w1, nsa_phi_b1, nsa_phi_w2)
        elif kind == 1:
            a = _conv_inproj(h, norm_rows, layer, conv_w_in, conv_w, layer_j, seq, tm=min(1024, seq), tn=512)
            h = _outproj(a, conv_w_out, layer_j, h, norm_rows, layer, 3, tm=tm_big, tn=512)
        else:
            a = _sgu_inproj(h, norm_rows, layer, sgu_w_in, sgu_ln_g3, sgu_ln_b3, sgu_w_s, sgu_b_s_t, layer_j,
                            tm=tm_big, tn=256)
            h = _outproj(a, sgu_w_out, layer_j, h, norm_rows, layer, 3, tm=tm_big, tn=512)
        h = _ffn(h, norm_rows, layer, 4, 5, ffn2_wg, ffn2_wu, ffn2_wd, tm=tm_big, tf=256)
        h = _ple(h, p2, norm_rows, layer, ple_wg, ple_wp, tm=tm_big, tn=512)
    return h.reshape(batch, seq, d)
```

```python
import functools
import math

import jax
import jax.numpy as jnp
from jax import lax
from jax.experimental import pallas as pl
from jax.experimental.pallas import tpu as pltpu

F32 = jnp.float32
BF16 = jnp.bfloat16

EPS = 1e-6
LANES = 128
D_MODEL = 2048
D_FF = 5632
PLE_DIM = 256
N_NORMS = 8
N_MIXERS = 3
NSA_HEADS = 16
NSA_GROUPS = 4
HEADS_PER_GROUP = NSA_HEADS // NSA_GROUPS
HEAD_DIM = D_MODEL // NSA_HEADS
GROUP_WIDTH = HEADS_PER_GROUP * HEAD_DIM
KV_WIDTH = NSA_GROUPS * HEAD_DIM
ROPE_DIM = HEAD_DIM // 4
ROPE_HALF = ROPE_DIM // 2
ROPE_THETA = 500000.0
CMP_BLOCK = 32
CMP_STRIDE = 16
CMP_HIDDEN = 256
SEL_BLOCK = 64
SEL_TOP = 16
WINDOW = 512
QKV_WIDTH = NSA_HEADS * HEAD_DIM + 6 * KV_WIDTH
N_GATES = 3 * NSA_HEADS
SGU_CHUNK = 128
SGU_GROUPS = 8
SGU_GROUP_DIM = D_MODEL // SGU_GROUPS
CONV_WIDTH = 3
CONV_HALO = 16

MASK_VALUE = -1e30
BLOCK_MASK_BIG = 2.0 ** 100
SCORE_SCALE = HEAD_DIM ** -0.5
EXP2_SCALE = SCORE_SCALE * math.log2(math.e)
VMEM_LIMIT = 60 * 1024 * 1024


def _params(semantics):
    return pltpu.CompilerParams(dimension_semantics=semantics, vmem_limit_bytes=VMEM_LIMIT)


def _rms(x, g):
    return x * lax.rsqrt(jnp.mean(x * x, axis=-1, keepdims=True) + EPS) * g


def _rms_ref(src_ref, inv_ref, g_ref):
    x = src_ref[...]
    inv_ref[...] = lax.rsqrt(jnp.mean(x * x, axis=-1, keepdims=True) + EPS)
    return src_ref[...] * inv_ref[...] * g_ref[...]


def _inv_scratch(rows):
    return pltpu.VMEM((rows, 1), F32)


def _dot(a, b):
    return jnp.dot(a, b, preferred_element_type=F32)


def _dot_nt(a, b):
    return lax.dot_general(a, b, (((1,), (1,)), ((), ())), preferred_element_type=F32)


def _norm_row_spec(layer, k):
    idx = layer * N_NORMS + k
    return pl.BlockSpec((None, 1, D_MODEL), lambda i, j: (idx, 0, 0))


def _ffn_body(h_ref, gpre_ref, gpost_ref, wg_ref, wu_ref, wd_ref, o_ref, xn_ref, inv_ref):
    f = pl.program_id(1)

    def down_proj_slice():
        x = xn_ref[...]
        gate = _dot(x, wg_ref[...].astype(BF16))
        up = _dot(x, wu_ref[...].astype(BF16))
        act = (gate * jax.nn.sigmoid(gate) * up).astype(BF16)
        return _dot(act, wd_ref[...].astype(BF16))

    @pl.when(f == 0)
    def _():
        xn_ref[...] = _rms_ref(h_ref, inv_ref, gpre_ref).astype(BF16)
        o_ref[...] = down_proj_slice()

    @pl.when(f > 0)
    def _():
        o_ref[...] += down_proj_slice()

    @pl.when(f == pl.num_programs(1) - 1)
    def _():
        o_ref[...] = h_ref[...] + 0.5 * _rms_ref(o_ref, inv_ref, gpost_ref)


def _ffn(h, norm_rows, layer, k_pre, k_post, wg, wu, wd, tm, tf):
    t = h.shape[0]
    return pl.pallas_call(
        _ffn_body,
        grid=(t // tm, D_FF // tf),
        in_specs=[
            pl.BlockSpec((tm, D_MODEL), lambda i, f: (i, 0)),
            _norm_row_spec(layer, k_pre),
            _norm_row_spec(layer, k_post),
            pl.BlockSpec((None, D_MODEL, tf), lambda i, f: (layer, 0, f)),
            pl.BlockSpec((None, D_MODEL, tf), lambda i, f: (layer, 0, f)),
            pl.BlockSpec((None, tf, D_MODEL), lambda i, f: (layer, f, 0)),
        ],
        out_specs=pl.BlockSpec((tm, D_MODEL), lambda i, f: (i, 0)),
        out_shape=jax.ShapeDtypeStruct((t, D_MODEL), F32),
        scratch_shapes=[pltpu.VMEM((tm, D_MODEL), BF16), _inv_scratch(tm)],
        compiler_params=_params(("parallel", "arbitrary")),
        name="ffn",
    )(h, norm_rows, norm_rows, wg, wu, wd)


def _outproj_body(a_ref, w_ref, h_ref, g_ref, o_ref, inv_ref, *, tn):
    j = pl.program_id(1)
    col = pl.multiple_of(j * tn, tn)
    o_ref[:, pl.ds(col, tn)] = _dot(a_ref[...], w_ref[...].astype(BF16))

    @pl.when(j == pl.num_programs(1) - 1)
    def _():
        o_ref[...] = h_ref[...] + _rms_ref(o_ref, inv_ref, g_ref)


def _outproj(a, w, w_layer, h, norm_rows, layer, k_norm, tm, tn):
    t, kdim = a.shape
    return pl.pallas_call(
        functools.partial(_outproj_body, tn=tn),
        grid=(t // tm, D_MODEL // tn),
        in_specs=[
            pl.BlockSpec((tm, kdim), lambda i, j: (i, 0)),
            pl.BlockSpec((None, kdim, tn), lambda i, j: (w_layer, 0, j)),
            pl.BlockSpec((tm, D_MODEL), lambda i, j: (i, 0)),
            _norm_row_spec(layer, k_norm),
        ],
        out_specs=pl.BlockSpec((tm, D_MODEL), lambda i, j: (i, 0)),
        out_shape=jax.ShapeDtypeStruct((t, D_MODEL), F32),
        scratch_shapes=[_inv_scratch(tm)],
        compiler_params=_params(("parallel", "arbitrary")),
        name="outproj",
    )(a, w, h, norm_rows)


def _ple_body(h_ref, p_ref, gpre_ref, gpost_ref, wg_ref, wp_ref, o_ref, xn_ref, pb_ref, inv_ref, *, tn):
    j = pl.program_id(1)

    def column_tile(col):
        gate = jax.nn.sigmoid(_dot(xn_ref[...], wg_ref[...].astype(BF16)))
        emb = _dot(pb_ref[...], wp_ref[...].astype(BF16))
        o_ref[:, pl.ds(col, tn)] = gate * emb

    @pl.when(j == 0)
    def _():
        xn_ref[...] = _rms_ref(h_ref, inv_ref, gpre_ref).astype(BF16)
        pb_ref[...] = p_ref[...].astype(BF16)
        column_tile(0)

    @pl.when(j > 0)
    def _():
        column_tile(pl.multiple_of(j * tn, tn))

    @pl.when(j == pl.num_programs(1) - 1)
    def _():
        o_ref[...] = h_ref[...] + _rms_ref(o_ref, inv_ref, gpost_ref)


def _ple(h, p, norm_rows, layer, wg, wp, tm, tn):
    t = h.shape[0]
    return pl.pallas_call(
        functools.partial(_ple_body, tn=tn),
        grid=(t // tm, D_MODEL // tn),
        in_specs=[
            pl.BlockSpec((tm, D_MODEL), lambda i, j: (i, 0)),
            pl.BlockSpec((None, tm, PLE_DIM), lambda i, j: (layer, i, 0)),
            _norm_row_spec(layer, 6),
            _norm_row_spec(layer, 7),
            pl.BlockSpec((None, D_MODEL, tn), lambda i, j: (layer, 0, j)),
            pl.BlockSpec((None, PLE_DIM, tn), lambda i, j: (layer, 0, j)),
        ],
        out_specs=pl.BlockSpec((tm, D_MODEL), lambda i, j: (i, 0)),
        out_shape=jax.ShapeDtypeStruct((t, D_MODEL), F32),
        scratch_shapes=[pltpu.VMEM((tm, D_MODEL), BF16), pltpu.VMEM((tm, PLE_DIM), BF16), _inv_scratch(tm)],
        compiler_params=_params(("parallel", "arbitrary")),
        name="ple",
    )(h, p, norm_rows, norm_rows, wg, wp)


def _nsa_inproj_feat_body(h_ref, g_ref, w_ref, wg_ref, feat_ref, gates_ref, xn_ref, inv_ref, *, n_q_tiles):
    j = pl.program_id(1)

    def feature_tile():
        row_scale = jnp.where(j < n_q_tiles, EXP2_SCALE, 1.0)
        feat_ref[...] = (_dot_nt(w_ref[...].astype(BF16), xn_ref[...]) * row_scale).astype(BF16)

    @pl.when(j == 0)
    def _():
        xn = _rms_ref(h_ref, inv_ref, g_ref).astype(BF16)
        xn_ref[...] = xn
        gates_ref[...] = jax.nn.sigmoid(_dot_nt(wg_ref[...].astype(BF16), xn))
        feature_tile()

    @pl.when(j > 0)
    def _():
        feature_tile()


def _nsa_inproj_feat(h, norm_rows, layer, w_in_t, w_layer, w_gate_t, tm):
    t = h.shape[0]
    tn = KV_WIDTH
    n_q_tiles = NSA_HEADS * HEAD_DIM // tn
    n_feat = NSA_HEADS * HEAD_DIM + 2 * KV_WIDTH
    return pl.pallas_call(
        functools.partial(_nsa_inproj_feat_body, n_q_tiles=n_q_tiles),
        grid=(t // tm, n_feat // tn),
        in_specs=[
            pl.BlockSpec((tm, D_MODEL), lambda i, j: (i, 0)),
            _norm_row_spec(layer, 2),
            pl.BlockSpec((None, tn, D_MODEL),
                         lambda i, j: (w_layer, j + 3 * (j // n_q_tiles) + j // (n_q_tiles + 1), 0)),
            pl.BlockSpec((LANES, D_MODEL), lambda i, j: (0, 0)),
        ],
        out_specs=[
            pl.BlockSpec((tn, tm), lambda i, j: (j, i)),
            pl.BlockSpec((LANES, tm), lambda i, j: (0, i)),
            pl.BlockSpec((tm, D_MODEL), lambda i, j: (i, 0)),
        ],
        out_shape=[
            jax.ShapeDtypeStruct((n_feat, t), BF16),
            jax.ShapeDtypeStruct((LANES, t), F32),
            jax.ShapeDtypeStruct((t, D_MODEL), BF16),
        ],
        scratch_shapes=[_inv_scratch(tm)],
        compiler_params=_params(("parallel", "arbitrary")),
        name="nsa_inproj_feat",
    )(h, norm_rows, w_in_t, w_gate_t)


def _rope_token_major(x, cos, sin):
    lane = lax.broadcasted_iota(jnp.int32, x.shape, 1)
    partner = jnp.where(lane < ROPE_HALF, pltpu.roll(x, LANES - ROPE_HALF, axis=1),
                        pltpu.roll(x, ROPE_HALF, axis=1))
    return x * cos + partner * sin


def _nsa_inproj_tok_body(xn_ref, wc_ref, wk_ref, cos_ref, sin_ref, oc_ref, ok_ref):
    xn = xn_ref[...]
    res_c = _dot_nt(xn, wc_ref[...].astype(BF16))
    res_k = _dot_nt(xn, wk_ref[...].astype(BF16))
    cos, sin = cos_ref[...], sin_ref[...]
    for c in range(NSA_GROUPS):
        cols = slice(c * LANES, (c + 1) * LANES)
        oc_ref[c] = res_c[:, cols]
        ok_ref[c] = _rope_token_major(res_k[:, cols], cos, sin).astype(BF16)


def _nsa_inproj_tok(xn, w_in_t, w_layer, cos_tok, sin_tok, seq, tm):
    t = xn.shape[0]
    per_seq = seq // tm
    q_blocks = NSA_HEADS * HEAD_DIM // KV_WIDTH
    slab_spec = pl.BlockSpec((NSA_GROUPS, tm, LANES), lambda i, j: (j, i, 0))
    return pl.pallas_call(
        _nsa_inproj_tok_body,
        grid=(t // tm, 2),
        in_specs=[
            pl.BlockSpec((tm, D_MODEL), lambda i, j: (i, 0)),
            pl.BlockSpec((None, KV_WIDTH, D_MODEL), lambda i, j: (w_layer, q_blocks + j, 0)),
            pl.BlockSpec((None, KV_WIDTH, D_MODEL), lambda i, j: (w_layer, q_blocks + 2 + 2 * j, 0)),
            pl.BlockSpec((tm, LANES), lambda i, j: (i % per_seq, 0)),
            pl.BlockSpec((tm, LANES), lambda i, j: (i % per_seq, 0)),
        ],
        out_specs=[slab_spec, slab_spec],
        out_shape=[
            jax.ShapeDtypeStruct((2 * NSA_GROUPS, t, LANES), F32),
            jax.ShapeDtypeStruct((2 * NSA_GROUPS, t, LANES), BF16),
        ],
        compiler_params=_params(("parallel", "arbitrary")),
        name="nsa_inproj_tok",
    )(xn, w_in_t, w_in_t, cos_tok, sin_tok)


def _rope_tables(seq):
    inv_freq = jnp.power(jnp.float32(ROPE_THETA), -jnp.arange(0, ROPE_DIM, 2, dtype=F32) / ROPE_DIM)
    ang = jnp.arange(seq, dtype=F32)[:, None] * inv_freq[None, :]
    cos, sin = jnp.cos(ang), jnp.sin(ang)
    cos_tok = jnp.concatenate([cos, cos, jnp.ones((seq, LANES - ROPE_DIM), F32)], axis=1)
    sin_tok = jnp.concatenate([-sin, sin, jnp.zeros((seq, LANES - ROPE_DIM), F32)], axis=1)
    return cos_tok, sin_tok, cos.T, sin.T


def _compress_body(x_ref, pe_ref, w1_ref, b1_ref, w2_ref, o_ref, *, feature_major):
    n_chunks = x_ref.shape[0] // CMP_STRIDE
    w1 = w1_ref[...].astype(BF16)
    both = jnp.zeros((n_chunks, 2 * CMP_HIDDEN), F32)
    for l in range(CMP_STRIDE):
        x_l = x_ref[pl.ds(l, n_chunks, stride=CMP_STRIDE), :].astype(BF16)
        lo, hi = l * HEAD_DIM, (CMP_STRIDE + l) * HEAD_DIM
        w_l = jnp.concatenate([w1[lo:lo + HEAD_DIM], w1[hi:hi + HEAD_DIM]], axis=1)
        both = both + _dot(x_l, w_l)
    first = both[:, :CMP_HIDDEN]
    second = pltpu.roll(both[:, CMP_HIDDEN:], n_chunks - 1, axis=0)
    pe = jnp.broadcast_to(pe_ref[...], (8, CMP_BLOCK * HEAD_DIM)).astype(BF16)
    const = _dot(pe, w1)[0:1]
    hid = first + second + const + b1_ref[...]
    act = (hid * jax.nn.sigmoid(hid)).astype(BF16)
    if feature_major:
        o_ref[...] = _dot_nt(w2_ref[...].astype(BF16), act).astype(BF16)
    else:
        o_ref[...] = _dot(act, w2_ref[...].astype(BF16)).astype(BF16)


def _compress(x, which, pe, w1, b1, w2, layer_j, batch, feature_major):
    seq = x.shape[1] // batch
    n_chunks = seq // CMP_STRIDE
    if feature_major:
        w2_spec = pl.BlockSpec((HEAD_DIM, CMP_HIDDEN), lambda g, b: (0, 0))
        out_block, out_dims = (None, None, HEAD_DIM, n_chunks), (NSA_GROUPS, batch, HEAD_DIM, n_chunks)
    else:
        w2_spec = pl.BlockSpec((None, None, CMP_HIDDEN, HEAD_DIM), lambda g, b: (layer_j, which, 0, 0))
        out_block, out_dims = (None, None, n_chunks, HEAD_DIM), (NSA_GROUPS, batch, n_chunks, HEAD_DIM)
    return pl.pallas_call(
        functools.partial(_compress_body, feature_major=feature_major),
        grid=(NSA_GROUPS, batch),
        in_specs=[
            pl.BlockSpec((None, seq, HEAD_DIM), lambda g, b: (which * NSA_GROUPS + g, b, 0)),
            pl.BlockSpec((None, None, 1, CMP_BLOCK * HEAD_DIM), lambda g, b: (layer_j, which, 0, 0)),
            pl.BlockSpec((None, None, CMP_BLOCK * HEAD_DIM, CMP_HIDDEN), lambda g, b: (layer_j, which, 0, 0)),
            pl.BlockSpec((None, None, 1, CMP_HIDDEN), lambda g, b: (layer_j, which, 0, 0)),
            w2_spec,
        ],
        out_specs=pl.BlockSpec(out_block, lambda g, b: (g, b, 0, 0)),
        out_shape=jax.ShapeDtypeStruct(out_dims, BF16),
        compiler_params=_params(("parallel", "parallel")),
        name="compress",
    )(x, pe, w1, b1, w2)


def _lane_tile(x, n):
    return jnp.concatenate([x] * n, axis=1)


def _load_q_t(q_ref, lo=0, n=None):
    n = q_ref.shape[1] if n is None else n
    return jnp.concatenate([q_ref[r * HEAD_DIM:(r + 1) * HEAD_DIM, lo:lo + n] for r in range(HEADS_PER_GROUP)],
                           axis=1)


def _rope_feature_major(q, cos, sin):
    x1 = q[0:ROPE_HALF].astype(F32)
    x2 = q[ROPE_HALF:ROPE_DIM].astype(F32)
    r1 = (x1 * cos - x2 * sin).astype(BF16)
    r2 = (x2 * cos + x1 * sin).astype(BF16)
    return jnp.concatenate([r1, r2, q[ROPE_DIM:]], axis=0)


def _split3(x):
    hi = x.astype(BF16)
    r1 = x - hi.astype(F32)
    mid = r1.astype(BF16)
    lo = (r1 - mid.astype(F32)).astype(BF16)
    return hi, mid, lo


def _gate_row(gates_ref, branch, g, r):
    return gates_ref[pl.ds(branch * NSA_HEADS + g * HEADS_PER_GROUP + r, 1), :]


def _cmp_attn_body(q_ref, kc_ref, vc_ref, gates_ref, o_ref, sel_ref, *score_refs, tq, n_cmp):
    n_sub = len(score_refs)
    ts = tq // n_sub
    kc = kc_ref[...]
    for u, s_ref in enumerate(score_refs):
        s_ref[...] = _dot(kc, _load_q_t(q_ref, u * ts, ts))
    for u, s_ref in enumerate(score_refs):
        _cmp_attn_subtile(s_ref[...], vc_ref, gates_ref, o_ref, sel_ref, pl.program_id(2) * tq + u * ts, u * ts,
                          ts, n_cmp)


def _cmp_attn_subtile(s, vc_ref, gates_ref, o_ref, sel_ref, t0, lane0, tq, n_cmp):
    g = pl.program_id(1)
    out_lanes = slice(lane0, lane0 + tq)

    tpos = t0 + lax.broadcasted_iota(jnp.int32, (n_cmp, tq), 1)
    cend = lax.broadcasted_iota(jnp.int32, (n_cmp, tq), 0) * CMP_STRIDE + (CMP_BLOCK - 1)
    visible = cend <= tpos
    bias = _lane_tile(jnp.where(visible, 0.0, MASK_VALUE), HEADS_PER_GROUP)
    okf = _lane_tile(jnp.where(visible, 1.0, 0.0), HEADS_PER_GROUP)
    s = s + bias
    m = jnp.max(s, axis=0, keepdims=True)
    e = jnp.exp2(s - m) * okf
    inv = 1.0 / jnp.maximum(jnp.sum(e, axis=0, keepdims=True), 1e-30)
    p = e * inv
    o = _dot(vc_ref[...], p.astype(BF16))

    for r in range(HEADS_PER_GROUP):
        gate = _gate_row(gates_ref, 0, g, r)[:, out_lanes]
        o_ref[r * HEAD_DIM:(r + 1) * HEAD_DIM, out_lanes] = gate * o[:, r * tq:(r + 1) * tq]

    psum = p[:, 0:tq]
    for r in range(1, HEADS_PER_GROUP):
        psum = psum + p[:, r * tq:(r + 1) * tq]
    n_sel_rows = LANES // 2
    jrow = lax.broadcasted_iota(jnp.int32, (n_sel_rows, n_cmp), 0)
    ccol = lax.broadcasted_iota(jnp.int32, (n_sel_rows, n_cmp), 1)
    c_lo, c_hi = ccol * CMP_STRIDE, ccol * CMP_STRIDE + (CMP_BLOCK - 1)
    j_lo, j_hi = jrow * SEL_BLOCK, jrow * SEL_BLOCK + (SEL_BLOCK - 1)
    ov = jnp.maximum(jnp.minimum(c_hi, j_hi) - jnp.maximum(c_lo, j_lo) + 1, 0).astype(F32) / CMP_STRIDE
    ov = ov.astype(BF16)
    hi, mid, lo = _split3(psum)
    imp = _dot(ov, hi) + _dot(ov, mid) + _dot(ov, lo)

    blk = lax.broadcasted_iota(jnp.int32, (n_sel_rows, tq), 0)
    cur = (t0 + lax.broadcasted_iota(jnp.int32, (n_sel_rows, tq), 1)) // SEL_BLOCK
    forced = (blk == 0) | (blk == cur) | (blk == cur - 1)
    key = jnp.where(blk > cur, -1.0, imp)
    key = jnp.where(forced, 1e30, key)
    sel = jnp.zeros((n_sel_rows, tq), F32)
    for _ in range(SEL_TOP):
        mx = jnp.max(key, axis=0, keepdims=True)
        first = jnp.min(jnp.where(key == mx, blk, LANES), axis=0, keepdims=True)
        pick = blk == first
        sel = jnp.where(pick, 1.0, sel)
        key = jnp.where(pick, -2.0, key)
    sel = jnp.where(blk > cur, 0.0, sel)
    sel_ref[:, out_lanes] = jnp.concatenate([sel, jnp.zeros_like(sel)], axis=0).astype(BF16)


def _cmp_attn(feat_t, kc, vc_t, gates_t, batch, seq, tq, ts):
    t = feat_t.shape[1]
    nq = seq // tq
    n_cmp = kc.shape[2]
    assert seq // SEL_BLOCK <= LANES // 2
    return pl.pallas_call(
        functools.partial(_cmp_attn_body, tq=tq, n_cmp=n_cmp),
        grid=(batch, NSA_GROUPS, nq),
        in_specs=[
            pl.BlockSpec((GROUP_WIDTH, tq), lambda b, g, i: (g, b * nq + i)),
            pl.BlockSpec((None, None, n_cmp, HEAD_DIM), lambda b, g, i: (g, b, 0, 0)),
            pl.BlockSpec((None, None, HEAD_DIM, n_cmp), lambda b, g, i: (g, b, 0, 0)),
            pl.BlockSpec((LANES, tq), lambda b, g, i: (0, b * nq + i)),
        ],
        out_specs=[
            pl.BlockSpec((GROUP_WIDTH, tq), lambda b, g, i: (g, b * nq + i)),
            pl.BlockSpec((None, LANES, tq), lambda b, g, i: (g, 0, b * nq + i)),
        ],
        out_shape=[
            jax.ShapeDtypeStruct((D_MODEL, t), F32),
            jax.ShapeDtypeStruct((NSA_GROUPS, LANES, t), BF16),
        ],
        scratch_shapes=[pltpu.VMEM((n_cmp, HEADS_PER_GROUP * ts), F32) for _ in range(tq // ts)],
        compiler_params=_params(("parallel", "parallel", "parallel")),
        name="cmp_attn",
    )(feat_t, kc, vc_t, gates_t)


def _sel_attn_body(q_ref, k_ref, e_ref, v_ref, sel_ref, cos_ref, sin_ref, gates_ref, oin_ref, o_ref,
                   m_ref, l_ref, acc_ref, sa_ref, sb_ref, *, tq, tk):
    g = pl.program_id(1)
    qi = pl.program_id(2)
    t0 = qi * tq
    q = _rope_feature_major(_load_q_t(q_ref), _lane_tile(cos_ref[...], HEADS_PER_GROUP),
                            _lane_tile(sin_ref[...], HEADS_PER_GROUP))
    unselected = _lane_tile(sel_ref[...] - 1.0, HEADS_PER_GROUP)
    q_aug = jnp.concatenate([q, unselected.astype(BF16)], axis=0)

    m_ref[...] = jnp.full(m_ref.shape, MASK_VALUE, F32)
    l_ref[...] = jnp.zeros(l_ref.shape, F32)
    acc_ref[...] = jnp.zeros(acc_ref.shape, F32)

    def scores(ki):
        k0 = pl.multiple_of(ki * tk, tk)
        k_aug = jnp.concatenate([k_ref[pl.ds(k0, tk), :], e_ref[pl.ds(k0, tk), :]], axis=1)
        return _dot(k_aug, q_aug)

    def accumulate(s, ki, causal):
        k0 = pl.multiple_of(ki * tk, tk)
        if causal:
            kpos = k0 + lax.broadcasted_iota(jnp.int32, (tk, tq), 0)
            tpos = t0 + lax.broadcasted_iota(jnp.int32, (tk, tq), 1)
            s = s + _lane_tile(jnp.where(kpos <= tpos, 0.0, MASK_VALUE), HEADS_PER_GROUP)
        m_prev = m_ref[...]
        m_new = jnp.maximum(m_prev, jnp.max(s, axis=0, keepdims=True))
        alpha = jnp.exp2(m_prev - m_new)
        p = jnp.exp2(s - m_new)
        l_ref[...] = alpha * l_ref[...] + jnp.sum(p, axis=0, keepdims=True)
        acc_ref[...] = alpha * acc_ref[...] + _dot(v_ref[:, pl.ds(k0, tk)], p.astype(BF16))
        m_ref[...] = m_new

    n_before = t0 // tk
    sa_ref[...] = scores(0)

    def body(j, carry):
        sb_ref[...] = scores(2 * j + 1)
        accumulate(sa_ref[...], 2 * j, causal=False)
        sa_ref[...] = scores(2 * j + 2)
        accumulate(sb_ref[...], 2 * j + 1, causal=False)
        return carry

    lax.fori_loop(0, n_before // 2, body, 0)

    @pl.when(n_before % 2 == 1)
    def _():
        sb_ref[...] = scores(n_before)
        accumulate(sa_ref[...], n_before - 1, causal=False)
        accumulate(sb_ref[...], n_before, causal=True)

    @pl.when(n_before % 2 == 0)
    def _():
        accumulate(sa_ref[...], n_before, causal=True)

    o = acc_ref[...] * (1.0 / l_ref[...])
    for r in range(HEADS_PER_GROUP):
        rows = slice(r * HEAD_DIM, (r + 1) * HEAD_DIM)
        o_ref[rows, :] = oin_ref[rows, :] + _gate_row(gates_ref, 1, g, r) * o[:, r * tq:(r + 1) * tq]


def _sel_attn(feat_t, tok, selm, cos_t, sin_t, gates_t, oin, batch, seq, tq, tk):
    t = feat_t.shape[1]
    nq = seq // tq
    lanes = HEADS_PER_GROUP * tq
    v_row0 = NSA_HEADS
    assert tk % tq == 0 and seq % tk == 0
    key_block = jnp.arange(seq, dtype=jnp.int32)[:, None] // SEL_BLOCK
    block_of_key = jnp.where(key_block == jnp.arange(LANES, dtype=jnp.int32)[None, :], BLOCK_MASK_BIG, 0.0)
    return pl.pallas_call(
        functools.partial(_sel_attn_body, tq=tq, tk=tk),
        grid=(batch, NSA_GROUPS, nq),
        in_specs=[
            pl.BlockSpec((GROUP_WIDTH, tq), lambda b, g, i: (g, b * nq + i)),
            pl.BlockSpec((None, seq, HEAD_DIM), lambda b, g, i: (g, b, 0)),
            pl.BlockSpec((seq, LANES), lambda b, g, i: (0, 0)),
            pl.BlockSpec((HEAD_DIM, seq), lambda b, g, i: (v_row0 + g, b)),
            pl.BlockSpec((None, LANES, tq), lambda b, g, i: (g, 0, b * nq + i)),
            pl.BlockSpec((ROPE_HALF, tq), lambda b, g, i: (0, i)),
            pl.BlockSpec((ROPE_HALF, tq), lambda b, g, i: (0, i)),
            pl.BlockSpec((LANES, tq), lambda b, g, i: (0, b * nq + i)),
            pl.BlockSpec((GROUP_WIDTH, tq), lambda b, g, i: (g, b * nq + i)),
        ],
        out_specs=pl.BlockSpec((GROUP_WIDTH, tq), lambda b, g, i: (g, b * nq + i)),
        out_shape=jax.ShapeDtypeStruct((D_MODEL, t), F32),
        scratch_shapes=[
            pltpu.VMEM((1, lanes), F32),
            pltpu.VMEM((1, lanes), F32),
            pltpu.VMEM((HEAD_DIM, lanes), F32),
            pltpu.VMEM((tk, lanes), F32),
            pltpu.VMEM((tk, lanes), F32),
        ],
        compiler_params=_params(("parallel", "parallel", "parallel")),
        name="attn_sel",
    )(feat_t, tok, block_of_key.astype(BF16), feat_t, selm, cos_t, sin_t, gates_t, oin)


def _win_attn_body(q_ref, k_ref, v_ref, bias_ref, cos_ref, sin_ref, gates_ref, oin_ref, o_ref, *score_refs, tq):
    g = pl.program_id(1)
    n_sub = len(score_refs)
    ts = tq // n_sub
    band = WINDOW + ts
    starts = []
    for u, s_ref in enumerate(score_refs):
        t0 = pl.program_id(2) * tq + u * ts
        lanes = slice(u * ts, (u + 1) * ts)
        q = _rope_feature_major(_load_q_t(q_ref, u * ts, ts), _lane_tile(cos_ref[:, lanes], HEADS_PER_GROUP),
                                _lane_tile(sin_ref[:, lanes], HEADS_PER_GROUP))
        start = pl.multiple_of(jnp.maximum(t0 - WINDOW, 0), ts)
        s_ref[...] = _dot(k_ref[pl.ds(start, band), :], q)
        starts.append((t0, start, lanes))
    for s_ref, (t0, start, lanes) in zip(score_refs, starts):
        bias = bias_ref[jnp.minimum(t0 // ts, WINDOW // ts)]
        s = s_ref[...] + _lane_tile(bias, HEADS_PER_GROUP)
        m = jnp.max(s, axis=0, keepdims=True)
        p = jnp.exp2(s - m)
        inv = 1.0 / jnp.sum(p, axis=0, keepdims=True)
        o = _dot(v_ref[:, pl.ds(start, band)], p.astype(BF16)) * inv
        for r in range(HEADS_PER_GROUP):
            rows = slice(r * HEAD_DIM, (r + 1) * HEAD_DIM)
            total = oin_ref[rows, lanes] + _gate_row(gates_ref, 2, g, r)[:, lanes] * o[:, r * ts:(r + 1) * ts]
            o_ref[lanes, rows] = total.T.astype(BF16)


def _win_attn(feat_t, tok, cos_t, sin_t, gates_t, oin, batch, seq, tq, ts):
    t = feat_t.shape[1]
    nq = seq // tq
    v_row0 = NSA_HEADS + NSA_GROUPS
    assert seq >= WINDOW + ts and tq % ts == 0 and WINDOW % ts == 0
    band = WINDOW + ts
    rel = jnp.arange(band, dtype=jnp.int32)[None, :, None] - jnp.arange(ts, dtype=jnp.int32)[None, None, :]
    d = (jnp.arange(WINDOW // ts + 1, dtype=jnp.int32) * ts)[:, None, None]
    win_bias = jnp.where((rel <= d) & (rel > d - WINDOW), 0.0, MASK_VALUE).astype(F32)
    return pl.pallas_call(
        functools.partial(_win_attn_body, tq=tq),
        grid=(batch, NSA_GROUPS, nq),
        in_specs=[
            pl.BlockSpec((GROUP_WIDTH, tq), lambda b, g, i: (g, b * nq + i)),
            pl.BlockSpec((None, seq, HEAD_DIM), lambda b, g, i: (NSA_GROUPS + g, b, 0)),
            pl.BlockSpec((HEAD_DIM, seq), lambda b, g, i: (v_row0 + g, b)),
            pl.BlockSpec((WINDOW // ts + 1, band, ts), lambda b, g, i: (0, 0, 0)),
            pl.BlockSpec((ROPE_HALF, tq), lambda b, g, i: (0, i)),
            pl.BlockSpec((ROPE_HALF, tq), lambda b, g, i: (0, i)),
            pl.BlockSpec((LANES, tq), lambda b, g, i: (0, b * nq + i)),
            pl.BlockSpec((GROUP_WIDTH, tq), lambda b, g, i: (g, b * nq + i)),
        ],
        out_specs=pl.BlockSpec((tq, GROUP_WIDTH), lambda b, g, i: (b * nq + i, g)),
        out_shape=jax.ShapeDtypeStruct((t, D_MODEL), BF16),
        scratch_shapes=[pltpu.VMEM((WINDOW + ts, HEADS_PER_GROUP * ts), F32) for _ in range(tq // ts)],
        compiler_params=_params(("parallel", "parallel", "parallel")),
        name="attn_win",
    )(feat_t, tok, feat_t, win_bias, cos_t, sin_t, gates_t, oin)


def _nsa_mixer(h, norm_rows, layer, layer_j, batch, seq, w_in, w_out, phi_pe, phi_w1, phi_b1, phi_w2):
    t = h.shape[0]
    w_in_t = jnp.swapaxes(w_in, 1, 2)
    w_gate_t = jnp.pad(w_in_t[layer_j, QKV_WIDTH:, :], ((0, LANES - N_GATES), (0, 0)))
    cos_tok, sin_tok, cos_t, sin_t = _rope_tables(seq)

    feat_t, gates_t, xn = _nsa_inproj_feat(h, norm_rows, layer, w_in_t, layer_j, w_gate_t, tm=1024)
    x_cmp, tok = _nsa_inproj_tok(xn, w_in_t, layer_j, cos_tok, sin_tok, seq, tm=min(1024, seq))

    pe = phi_pe.reshape(phi_pe.shape[0], 2, 1, CMP_BLOCK * HEAD_DIM)
    b1 = phi_b1.reshape(phi_b1.shape[0], 2, 1, CMP_HIDDEN)
    kc = _compress(x_cmp, 0, pe, phi_w1, b1, phi_w2, layer_j, batch, feature_major=False)
    vc_t = _compress(x_cmp, 1, pe, phi_w1, b1, phi_w2[layer_j, 1].T, layer_j, batch, feature_major=True)

    o1, selm = _cmp_attn(feat_t, kc, vc_t, gates_t, batch, seq, tq=1024, ts=256)
    o2 = _sel_attn(feat_t, tok, selm, cos_t, sin_t, gates_t, o1, batch, seq, tq=256, tk=512)
    o3 = _win_attn(feat_t, tok, cos_t, sin_t, gates_t, o2, batch, seq, tq=1024, ts=256)
    return _outproj(o3, w_out, layer_j, h, norm_rows, layer, 3, tm=min(1024, t), tn=512)


def _conv_inproj_body(h_ref, halo_ref, g_ref, wb_ref, wc_ref, wu_ref, cw_ref, o_ref, xn_ref, inv_ref, *,
                      tiles_per_seq):
    i = pl.program_id(0)
    j = pl.program_id(1)

    def channel_tile():
        x = xn_ref[...]
        bg = _dot(x, wb_ref[...].astype(BF16))
        z = _dot(x, wc_ref[...].astype(BF16)) * _dot(x, wu_ref[...].astype(BF16))
        cw = cw_ref[...]
        conv = cw[2:3] * z + cw[1:2] * pltpu.roll(z, 1, axis=0) + cw[0:1] * pltpu.roll(z, 2, axis=0)
        o_ref[...] = (bg * conv)[CONV_HALO:].astype(BF16)

    @pl.when(j == 0)
    def _():
        keep = jnp.where(i % tiles_per_seq == 0, 0.0, 1.0)
        xn_ref[0:CONV_HALO] = (_rms(halo_ref[...], g_ref[...]) * keep).astype(BF16)
        xn_ref[CONV_HALO:] = _rms_ref(h_ref, inv_ref, g_ref).astype(BF16)
        channel_tile()

    @pl.when(j > 0)
    def _():
        channel_tile()


def _conv_inproj(h, norm_rows, layer, w_in, conv_w, layer_j, seq, tm, tn):
    t = h.shape[0]
    nj = D_MODEL // tn
    halo_blocks = tm // CONV_HALO
    return pl.pallas_call(
        functools.partial(_conv_inproj_body, tiles_per_seq=seq // tm),
        grid=(t // tm, nj),
        in_specs=[
            pl.BlockSpec((tm, D_MODEL), lambda i, j: (i, 0)),
            pl.BlockSpec((CONV_HALO, D_MODEL), lambda i, j: (jnp.maximum(i * halo_blocks - 1, 0), 0)),
            _norm_row_spec(layer, 2),
            pl.BlockSpec((None, D_MODEL, tn), lambda i, j: (layer_j, 0, j)),
            pl.BlockSpec((None, D_MODEL, tn), lambda i, j: (layer_j, 0, nj + j)),
            pl.BlockSpec((None, D_MODEL, tn), lambda i, j: (layer_j, 0, 2 * nj + j)),
            pl.BlockSpec((None, CONV_WIDTH, tn), lambda i, j: (layer_j, 0, j)),
        ],
        out_specs=pl.BlockSpec((tm, tn), lambda i, j: (i, j)),
        out_shape=jax.ShapeDtypeStruct((t, D_MODEL), BF16),
        scratch_shapes=[pltpu.VMEM((tm + CONV_HALO, D_MODEL), BF16), _inv_scratch(tm)],
        compiler_params=_params(("parallel", "arbitrary")),
        name="conv_inproj",
    )(h, h, norm_rows, w_in, w_in, w_in, conv_w)


def _sgu_inproj_body(h_ref, g_ref, w_ref, lng_ref, lnb_ref, ws_ref, bs_ref, o_ref, xn_ref, z_ref, inv_ref, *,
                     tm, tn):
    j = pl.program_id(1)

    def column_tile(col):
        z_ref[:, pl.ds(col, tn)] = jax.nn.gelu(_dot(xn_ref[...], w_ref[...].astype(BF16)))

    @pl.when(j == 0)
    def _():
        xn_ref[...] = _rms_ref(h_ref, inv_ref, g_ref).astype(BF16)
        column_tile(0)

    @pl.when(j > 0)
    def _():
        column_tile(pl.multiple_of(j * tn, tn))

    @pl.when(j == pl.num_programs(1) - 1)
    def _():
        row = lax.broadcasted_iota(jnp.int32, (SGU_CHUNK, SGU_CHUNK), 0)
        colm = lax.broadcasted_iota(jnp.int32, (SGU_CHUNK, SGU_CHUNK), 1)
        ws = [jnp.where(colm <= row, ws_ref[grp], 0.0).astype(BF16) for grp in range(SGU_GROUPS)]
        bs = bs_ref[...]
        lng, lnb = lng_ref[...], lnb_ref[...]
        for c in range(tm // SGU_CHUNK):
            rws = slice(c * SGU_CHUNK, (c + 1) * SGU_CHUNK)
            v = z_ref[rws, D_MODEL:]
            mu = jnp.mean(v, axis=-1, keepdims=True)
            var = jnp.mean(jnp.square(v - mu), axis=-1, keepdims=True)
            vn = ((v - mu) * lax.rsqrt(var + EPS) * lng + lnb).astype(BF16)
            for grp in range(SGU_GROUPS):
                cols = slice(grp * SGU_GROUP_DIM, (grp + 1) * SGU_GROUP_DIM)
                sv = _dot(ws[grp], vn[:, cols]) + bs[:, grp:grp + 1]
                o_ref[rws, cols] = (z_ref[rws, cols] * sv).astype(BF16)


def _sgu_inproj(h, norm_rows, layer, w_in, ln_g, ln_b, w_s, b_s_t, layer_j, tm, tn):
    t = h.shape[0]
    return pl.pallas_call(
        functools.partial(_sgu_inproj_body, tm=tm, tn=tn),
        grid=(t // tm, 2 * D_MODEL // tn),
        in_specs=[
            pl.BlockSpec((tm, D_MODEL), lambda i, j: (i, 0)),
            _norm_row_spec(layer, 2),
            pl.BlockSpec((None, D_MODEL, tn), lambda i, j: (layer_j, 0, j)),
            pl.BlockSpec((None, 1, D_MODEL), lambda i, j: (layer_j, 0, 0)),
            pl.BlockSpec((None, 1, D_MODEL), lambda i, j: (layer_j, 0, 0)),
            pl.BlockSpec((None, SGU_GROUPS, SGU_CHUNK, SGU_CHUNK), lambda i, j: (layer_j, 0, 0, 0)),
            pl.BlockSpec((None, SGU_CHUNK, SGU_GROUPS), lambda i, j: (layer_j, 0, 0)),
        ],
        out_specs=pl.BlockSpec((tm, D_MODEL), lambda i, j: (i, 0)),
        out_shape=jax.ShapeDtypeStruct((t, D_MODEL), BF16),
        scratch_shapes=[pltpu.VMEM((tm, D_MODEL), BF16), pltpu.VMEM((tm, 2 * D_MODEL), F32), _inv_scratch(tm)],
        compiler_params=_params(("parallel", "arbitrary")),
        name="sgu_inproj",
    )(h, norm_rows, w_in, ln_g, ln_b, w_s, b_s_t)


def kernel(x, p, norm_g, ffn1_wg, ffn1_wu, ffn1_wd, ffn2_wg, ffn2_wu, ffn2_wd, ple_wg, ple_wp, nsa_w_in, nsa_w_out, nsa_phi_pe, nsa_phi_w1, nsa_phi_b1, nsa_phi_w2, conv_w_in, conv_w, conv_w_out, sgu_w_in, sgu_ln_g, sgu_ln_b, sgu_w_s, sgu_b_s, sgu_w_out):
    batch, seq, d = x.shape
    depth = p.shape[0]
    t = batch * seq
    assert d == D_MODEL and seq % 512 == 0
    h = x.reshape(t, d)
    p2 = p.reshape(depth, t, PLE_DIM)
    norm_rows = norm_g.reshape(depth * N_NORMS, 1, d)
    sgu_ln_g3 = sgu_ln_g.reshape(-1, 1, d)
    sgu_ln_b3 = sgu_ln_b.reshape(-1, 1, d)
    sgu_b_s_t = jnp.swapaxes(sgu_b_s, 1, 2)
    tm_big = min(1024, t)

    for layer in range(depth):
        layer_j = layer // N_MIXERS
        h = _ffn(h, norm_rows, layer, 0, 1, ffn1_wg, ffn1_wu, ffn1_wd, tm=tm_big, tf=256)
        kind = layer % N_MIXERS
        if kind == 0:
            h = _nsa_mixer(h, norm_rows, layer, layer_j, batch, seq, nsa_w_in, nsa_w_out, nsa_phi_pe,
                           nsa_phi_w1, nsa_phi_b1, nsa_phi_w2)
        elif kind == 1:
            a = _conv_inproj(h, norm_rows, layer, conv_w_in, conv_w, layer_j, seq, tm=min(1024, seq), tn=512)
            h = _outproj(a, conv_w_out, layer_j, h, norm_rows, layer, 3, tm=tm_big, tn=512)
        else:
            a = _sgu_inproj(h, norm_rows, layer, sgu_w_in, sgu_ln_g3, sgu_ln_b3, sgu_w_s, sgu_b_s_t, layer_j,
                            tm=tm_big, tn=256)
            h = _outproj(a, sgu_w_out, layer_j, h, norm_rows, layer, 3, tm=tm_big, tn=512)
        h = _ffn(h, norm_rows, layer, 4, 5, ffn2_wg, ffn2_wu, ffn2_wd, tm=tm_big, tf=256)
        h = _ple(h, p2, norm_rows, layer, ple_wg, ple_wp, tm=tm_big, tn=512)
    return h.reshape(batch, seq, d)
```

```python
import functools
import math

import jax
import jax.numpy as jnp
from jax import lax
from jax.experimental import pallas as pl
from jax.experimental.pallas import tpu as pltpu

F32 = jnp.float32
BF16 = jnp.bfloat16

EPS = 1e-6
LANES = 128
D_MODEL = 2048
D_FF = 5632
PLE_DIM = 256
N_NORMS = 8
N_MIXERS = 3
NSA_HEADS = 16
NSA_GROUPS = 4
HEADS_PER_GROUP = NSA_HEADS // NSA_GROUPS
HEAD_DIM = D_MODEL // NSA_HEADS
GROUP_WIDTH = HEADS_PER_GROUP * HEAD_DIM
KV_WIDTH = NSA_GROUPS * HEAD_DIM
ROPE_DIM = HEAD_DIM // 4
ROPE_HALF = ROPE_DIM // 2
ROPE_THETA = 500000.0
CMP_BLOCK = 32
CMP_STRIDE = 16
CMP_HIDDEN = 256
SEL_BLOCK = 64
SEL_TOP = 16
WINDOW = 512
QKV_WIDTH = NSA_HEADS * HEAD_DIM + 6 * KV_WIDTH
N_GATES = 3 * NSA_HEADS
SGU_CHUNK = 128
SGU_GROUPS = 8
SGU_GROUP_DIM = D_MODEL // SGU_GROUPS
CONV_WIDTH = 3
CONV_HALO = 16

MASK_VALUE = -1e30
BLOCK_MASK_BIG = 2.0 ** 100
SCORE_SCALE = HEAD_DIM ** -0.5
EXP2_SCALE = SCORE_SCALE * math.log2(math.e)
VMEM_LIMIT = 60 * 1024 * 1024


def _params(semantics):
    return pltpu.CompilerParams(dimension_semantics=semantics, vmem_limit_bytes=VMEM_LIMIT)


def _rms(x, g):
    return x * lax.rsqrt(jnp.mean(x * x, axis=-1, keepdims=True) + EPS) * g


def _rms_ref(src_ref, inv_ref, g_ref):
    x = src_ref[...]
    inv_ref[...] = lax.rsqrt(jnp.mean(x * x, axis=-1, keepdims=True) + EPS)
    return src_ref[...] * inv_ref[...] * g_ref[...]


def _inv_scratch(rows):
    return pltpu.VMEM((rows, 1), F32)


def _dot(a, b):
    return jnp.dot(a, b, preferred_element_type=F32)


def _dot_nt(a, b):
    return lax.dot_general(a, b, (((1,), (1,)), ((), ())), preferred_element_type=F32)


def _norm_row_spec(layer, k):
    idx = layer * N_NORMS + k
    return pl.BlockSpec((None, 1, D_MODEL), lambda i, j: (idx, 0, 0))


def _ffn_body(h_ref, gpre_ref, gpost_ref, wg_ref, wu_ref, wd_ref, o_ref, xn_ref, inv_ref):
    f = pl.program_id(1)

    def down_proj_slice():
        x = xn_ref[...]
        gate = _dot(x, wg_ref[...].astype(BF16))
        up = _dot(x, wu_ref[...].astype(BF16))
        act = (gate * jax.nn.sigmoid(gate) * up).astype(BF16)
        return _dot(act, wd_ref[...].astype(BF16))

    @pl.when(f == 0)
    def _():
        xn_ref[...] = _rms_ref(h_ref, inv_ref, gpre_ref).astype(BF16)
        o_ref[...] = down_proj_slice()

    @pl.when(f > 0)
    def _():
        o_ref[...] += down_proj_slice()

    @pl.when(f == pl.num_programs(1) - 1)
    def _():
        o_ref[...] = h_ref[...] + 0.5 * _rms_ref(o_ref, inv_ref, gpost_ref)


def _ffn(h, norm_rows, layer, k_pre, k_post, wg, wu, wd, tm, tf):
    t = h.shape[0]
    return pl.pallas_call(
        _ffn_body,
        grid=(t // tm, D_FF // tf),
        in_specs=[
            pl.BlockSpec((tm, D_MODEL), lambda i, f: (i, 0)),
            _norm_row_spec(layer, k_pre),
            _norm_row_spec(layer, k_post),
            pl.BlockSpec((None, D_MODEL, tf), lambda i, f: (layer, 0, f)),
            pl.BlockSpec((None, D_MODEL, tf), lambda i, f: (layer, 0, f)),
            pl.BlockSpec((None, tf, D_MODEL), lambda i, f: (layer, f, 0)),
        ],
        out_specs=pl.BlockSpec((tm, D_MODEL), lambda i, f: (i, 0)),
        out_shape=jax.ShapeDtypeStruct((t, D_MODEL), F32),
        scratch_shapes=[pltpu.VMEM((tm, D_MODEL), BF16), _inv_scratch(tm)],
        compiler_params=_params(("parallel", "arbitrary")),
        name="ffn",
    )(h, norm_rows, norm_rows, wg, wu, wd)


def _resident_cols(n_col_tiles):
    return lambda i, j: jnp.where(i == 0, j, n_col_tiles - 1)


def _outproj_body(a_ref, w_ref, h_ref, g_ref, o_ref, inv_ref, wb_ref, *, tn):
    j = pl.program_id(1)
    col = pl.multiple_of(j * tn, tn)

    @pl.when(pl.program_id(0) == 0)
    def _():
        wb_ref[:, pl.ds(col, tn)] = w_ref[...].astype(BF16)

    o_ref[:, pl.ds(col, tn)] = _dot(a_ref[...], wb_ref[:, pl.ds(col, tn)])

    @pl.when(j == pl.num_programs(1) - 1)
    def _():
        o_ref[...] = h_ref[...] + _rms_ref(o_ref, inv_ref, g_ref)


def _outproj(a, w, w_layer, h, norm_rows, layer, k_norm, tm, tn):
    t, kdim = a.shape
    w_col = _resident_cols(D_MODEL // tn)
    return pl.pallas_call(
        functools.partial(_outproj_body, tn=tn),
        grid=(t // tm, D_MODEL // tn),
        in_specs=[
            pl.BlockSpec((tm, kdim), lambda i, j: (i, 0)),
            pl.BlockSpec((None, kdim, tn), lambda i, j: (w_layer, 0, w_col(i, j))),
            pl.BlockSpec((tm, D_MODEL), lambda i, j: (i, 0)),
            _norm_row_spec(layer, k_norm),
        ],
        out_specs=pl.BlockSpec((tm, D_MODEL), lambda i, j: (i, 0)),
        out_shape=jax.ShapeDtypeStruct((t, D_MODEL), F32),
        scratch_shapes=[_inv_scratch(tm), pltpu.VMEM((kdim, D_MODEL), BF16)],
        compiler_params=_params(("arbitrary", "arbitrary")),
        name="outproj",
    )(a, w, h, norm_rows)


def _ple_body(h_ref, p_ref, gpre_ref, gpost_ref, wg_ref, wp_ref, o_ref, xn_ref, pb_ref, inv_ref, wgb_ref, wpb_ref,
              *, tn):
    j = pl.program_id(1)

    def column_tile(col):
        cols = pl.ds(col, tn)

        @pl.when(pl.program_id(0) == 0)
        def _():
            wgb_ref[:, cols] = wg_ref[...].astype(BF16)
            wpb_ref[:, cols] = wp_ref[...].astype(BF16)

        gate = jax.nn.sigmoid(_dot(xn_ref[...], wgb_ref[:, cols]))
        emb = _dot(pb_ref[...], wpb_ref[:, cols])
        o_ref[:, cols] = gate * emb

    @pl.when(j == 0)
    def _():
        xn_ref[...] = _rms_ref(h_ref, inv_ref, gpre_ref).astype(BF16)
        pb_ref[...] = p_ref[...].astype(BF16)
        column_tile(0)

    @pl.when(j > 0)
    def _():
        column_tile(pl.multiple_of(j * tn, tn))

    @pl.when(j == pl.num_programs(1) - 1)
    def _():
        o_ref[...] = h_ref[...] + _rms_ref(o_ref, inv_ref, gpost_ref)


def _ple(h, p, norm_rows, layer, wg, wp, tm, tn):
    t = h.shape[0]
    w_col = _resident_cols(D_MODEL // tn)
    return pl.pallas_call(
        functools.partial(_ple_body, tn=tn),
        grid=(t // tm, D_MODEL // tn),
        in_specs=[
            pl.BlockSpec((tm, D_MODEL), lambda i, j: (i, 0)),
            pl.BlockSpec((None, tm, PLE_DIM), lambda i, j: (layer, i, 0)),
            _norm_row_spec(layer, 6),
            _norm_row_spec(layer, 7),
            pl.BlockSpec((None, D_MODEL, tn), lambda i, j: (layer, 0, w_col(i, j))),
            pl.BlockSpec((None, PLE_DIM, tn), lambda i, j: (layer, 0, w_col(i, j))),
        ],
        out_specs=pl.BlockSpec((tm, D_MODEL), lambda i, j: (i, 0)),
        out_shape=jax.ShapeDtypeStruct((t, D_MODEL), F32),
        scratch_shapes=[pltpu.VMEM((tm, D_MODEL), BF16), pltpu.VMEM((tm, PLE_DIM), BF16), _inv_scratch(tm),
                        pltpu.VMEM((D_MODEL, D_MODEL), BF16), pltpu.VMEM((PLE_DIM, D_MODEL), BF16)],
        compiler_params=_params(("arbitrary", "arbitrary")),
        name="ple",
    )(h, p, norm_rows, norm_rows, wg, wp)


def _nsa_inproj_feat_body(h_ref, g_ref, w_ref, wg_ref, feat_ref, gates_ref, xn_ref, inv_ref, wb_ref, *,
                          n_q_tiles):
    j = pl.program_id(1)
    tn = w_ref.shape[0]

    def feature_tile():
        rows = pl.ds(pl.multiple_of(j * tn, tn), tn)

        @pl.when(pl.program_id(0) == 0)
        def _():
            wb_ref[rows, :] = w_ref[...].astype(BF16)

        row_scale = jnp.where(j < n_q_tiles, EXP2_SCALE, 1.0)
        feat_ref[...] = (_dot_nt(wb_ref[rows, :], xn_ref[...]) * row_scale).astype(BF16)

    @pl.when(j == 0)
    def _():
        xn = _rms_ref(h_ref, inv_ref, g_ref).astype(BF16)
        xn_ref[...] = xn
        gates_ref[...] = jax.nn.sigmoid(_dot_nt(wg_ref[...].astype(BF16), xn))
        feature_tile()

    @pl.when(j > 0)
    def _():
        feature_tile()


def _nsa_inproj_feat(h, norm_rows, layer, w_in_t, w_layer, w_gate_t, tm):
    t = h.shape[0]
    tn = KV_WIDTH
    n_q_tiles = NSA_HEADS * HEAD_DIM // tn
    n_feat = NSA_HEADS * HEAD_DIM + 2 * KV_WIDTH
    w_tile = _resident_cols(n_feat // tn)

    def w_block(i, j):
        jj = w_tile(i, j)
        return jj + 3 * (jj // n_q_tiles) + jj // (n_q_tiles + 1)

    return pl.pallas_call(
        functools.partial(_nsa_inproj_feat_body, n_q_tiles=n_q_tiles),
        grid=(t // tm, n_feat // tn),
        in_specs=[
            pl.BlockSpec((tm, D_MODEL), lambda i, j: (i, 0)),
            _norm_row_spec(layer, 2),
            pl.BlockSpec((None, tn, D_MODEL), lambda i, j: (w_layer, w_block(i, j), 0)),
            pl.BlockSpec((LANES, D_MODEL), lambda i, j: (0, 0)),
        ],
        out_specs=[
            pl.BlockSpec((tn, tm), lambda i, j: (j, i)),
            pl.BlockSpec((LANES, tm), lambda i, j: (0, i)),
            pl.BlockSpec((tm, D_MODEL), lambda i, j: (i, 0)),
        ],
        out_shape=[
            jax.ShapeDtypeStruct((n_feat, t), BF16),
            jax.ShapeDtypeStruct((LANES, t), F32),
            jax.ShapeDtypeStruct((t, D_MODEL), BF16),
        ],
        scratch_shapes=[_inv_scratch(tm), pltpu.VMEM((n_feat, D_MODEL), BF16)],
        compiler_params=_params(("arbitrary", "arbitrary")),
        name="nsa_inproj_feat",
    )(h, norm_rows, w_in_t, w_gate_t)


def _rope_token_major(x, cos, sin):
    lane = lax.broadcasted_iota(jnp.int32, x.shape, 1)
    partner = jnp.where(lane < ROPE_HALF, pltpu.roll(x, LANES - ROPE_HALF, axis=1),
                        pltpu.roll(x, ROPE_HALF, axis=1))
    return x * cos + partner * sin


def _nsa_inproj_tok_body(xn_ref, wc_ref, wk_ref, cos_ref, sin_ref, oc_ref, ok_ref, wcb_ref, wkb_ref):
    rows = pl.ds(pl.multiple_of(pl.program_id(1) * KV_WIDTH, KV_WIDTH), KV_WIDTH)

    @pl.when(pl.program_id(0) == 0)
    def _():
        wcb_ref[rows, :] = wc_ref[...].astype(BF16)
        wkb_ref[rows, :] = wk_ref[...].astype(BF16)

    xn = xn_ref[...]
    res_c = _dot_nt(xn, wcb_ref[rows, :])
    res_k = _dot_nt(xn, wkb_ref[rows, :])
    cos, sin = cos_ref[...], sin_ref[...]
    for c in range(NSA_GROUPS):
        cols = slice(c * LANES, (c + 1) * LANES)
        oc_ref[c] = res_c[:, cols]
        ok_ref[c] = _rope_token_major(res_k[:, cols], cos, sin).astype(BF16)


def _nsa_inproj_tok(xn, w_in_t, w_layer, cos_tok, sin_tok, seq, tm):
    t = xn.shape[0]
    per_seq = seq // tm
    q_blocks = NSA_HEADS * HEAD_DIM // KV_WIDTH
    slab_spec = pl.BlockSpec((NSA_GROUPS, tm, LANES), lambda i, j: (j, i, 0))
    w_step = _resident_cols(2)
    return pl.pallas_call(
        _nsa_inproj_tok_body,
        grid=(t // tm, 2),
        in_specs=[
            pl.BlockSpec((tm, D_MODEL), lambda i, j: (i, 0)),
            pl.BlockSpec((None, KV_WIDTH, D_MODEL), lambda i, j: (w_layer, q_blocks + w_step(i, j), 0)),
            pl.BlockSpec((None, KV_WIDTH, D_MODEL), lambda i, j: (w_layer, q_blocks + 2 + 2 * w_step(i, j), 0)),
            pl.BlockSpec((tm, LANES), lambda i, j: (i % per_seq, 0)),
            pl.BlockSpec((tm, LANES), lambda i, j: (i % per_seq, 0)),
        ],
        out_specs=[slab_spec, slab_spec],
        out_shape=[
            jax.ShapeDtypeStruct((2 * NSA_GROUPS, t, LANES), F32),
            jax.ShapeDtypeStruct((2 * NSA_GROUPS, t, LANES), BF16),
        ],
        scratch_shapes=[pltpu.VMEM((2 * KV_WIDTH, D_MODEL), BF16), pltpu.VMEM((2 * KV_WIDTH, D_MODEL), BF16)],
        compiler_params=_params(("arbitrary", "arbitrary")),
        name="nsa_inproj_tok",
    )(xn, w_in_t, w_in_t, cos_tok, sin_tok)


def _rope_tables(seq):
    inv_freq = jnp.power(jnp.float32(ROPE_THETA), -jnp.arange(0, ROPE_DIM, 2, dtype=F32) / ROPE_DIM)
    ang = jnp.arange(seq, dtype=F32)[:, None] * inv_freq[None, :]
    cos, sin = jnp.cos(ang), jnp.sin(ang)
    cos_tok = jnp.concatenate([cos, cos, jnp.ones((seq, LANES - ROPE_DIM), F32)], axis=1)
    sin_tok = jnp.concatenate([-sin, sin, jnp.zeros((seq, LANES - ROPE_DIM), F32)], axis=1)
    return cos_tok, sin_tok, cos.T, sin.T


def _compress_body(x_ref, pe_ref, w1_ref, b1_ref, w2_ref, o_ref, *, feature_major):
    n_chunks = x_ref.shape[0] // CMP_STRIDE
    w1 = w1_ref[...].astype(BF16)
    both = jnp.zeros((n_chunks, 2 * CMP_HIDDEN), F32)
    for l in range(CMP_STRIDE):
        x_l = x_ref[pl.ds(l, n_chunks, stride=CMP_STRIDE), :].astype(BF16)
        lo, hi = l * HEAD_DIM, (CMP_STRIDE + l) * HEAD_DIM
        w_l = jnp.concatenate([w1[lo:lo + HEAD_DIM], w1[hi:hi + HEAD_DIM]], axis=1)
        both = both + _dot(x_l, w_l)
    first = both[:, :CMP_HIDDEN]
    second = pltpu.roll(both[:, CMP_HIDDEN:], n_chunks - 1, axis=0)
    pe = jnp.broadcast_to(pe_ref[...], (8, CMP_BLOCK * HEAD_DIM)).astype(BF16)
    const = _dot(pe, w1)[0:1]
    hid = first + second + const + b1_ref[...]
    act = (hid * jax.nn.sigmoid(hid)).astype(BF16)
    if feature_major:
        o_ref[...] = _dot_nt(w2_ref[...].astype(BF16), act).astype(BF16)
    else:
        o_ref[...] = _dot(act, w2_ref[...].astype(BF16)).astype(BF16)


def _compress(x, which, pe, w1, b1, w2, layer_j, batch, feature_major):
    seq = x.shape[1] // batch
    n_chunks = seq // CMP_STRIDE
    if feature_major:
        w2_spec = pl.BlockSpec((HEAD_DIM, CMP_HIDDEN), lambda g, b: (0, 0))
        out_block, out_dims = (None, None, HEAD_DIM, n_chunks), (NSA_GROUPS, batch, HEAD_DIM, n_chunks)
    else:
        w2_spec = pl.BlockSpec((None, None, CMP_HIDDEN, HEAD_DIM), lambda g, b: (layer_j, which, 0, 0))
        out_block, out_dims = (None, None, n_chunks, HEAD_DIM), (NSA_GROUPS, batch, n_chunks, HEAD_DIM)
    return pl.pallas_call(
        functools.partial(_compress_body, feature_major=feature_major),
        grid=(NSA_GROUPS, batch),
        in_specs=[
            pl.BlockSpec((None, seq, HEAD_DIM), lambda g, b: (which * NSA_GROUPS + g, b, 0)),
            pl.BlockSpec((None, None, 1, CMP_BLOCK * HEAD_DIM), lambda g, b: (layer_j, which, 0, 0)),
            pl.BlockSpec((None, None, CMP_BLOCK * HEAD_DIM, CMP_HIDDEN), lambda g, b: (layer_j, which, 0, 0)),
            pl.BlockSpec((None, None, 1, CMP_HIDDEN), lambda g, b: (layer_j, which, 0, 0)),
            w2_spec,
        ],
        out_specs=pl.BlockSpec(out_block, lambda g, b: (g, b, 0, 0)),
        out_shape=jax.ShapeDtypeStruct(out_dims, BF16),
        compiler_params=_params(("parallel", "parallel")),
        name="compress",
    )(x, pe, w1, b1, w2)


def _lane_tile(x, n):
    return jnp.concatenate([x] * n, axis=1)


def _load_q_t(q_ref, lo=0, n=None):
    n = q_ref.shape[1] if n is None else n
    return jnp.concatenate([q_ref[r * HEAD_DIM:(r + 1) * HEAD_DIM, lo:lo + n] for r in range(HEADS_PER_GROUP)],
                           axis=1)


def _rope_feature_major(q, cos, sin):
    x1 = q[0:ROPE_HALF].astype(F32)
    x2 = q[ROPE_HALF:ROPE_DIM].astype(F32)
    r1 = (x1 * cos - x2 * sin).astype(BF16)
    r2 = (x2 * cos + x1 * sin).astype(BF16)
    return jnp.concatenate([r1, r2, q[ROPE_DIM:]], axis=0)


def _split3(x):
    hi = x.astype(BF16)
    r1 = x - hi.astype(F32)
    mid = r1.astype(BF16)
    lo = (r1 - mid.astype(F32)).astype(BF16)
    return hi, mid, lo


def _gate_row(gates_ref, branch, g, r):
    return gates_ref[pl.ds(branch * NSA_HEADS + g * HEADS_PER_GROUP + r, 1), :]


def _cmp_attn_body(q_ref, kc_ref, vc_ref, gates_ref, o_ref, sel_ref, *score_refs, tq, n_cmp):
    n_sub = len(score_refs)
    ts = tq // n_sub
    kc = kc_ref[...]
    for u, s_ref in enumerate(score_refs):
        s_ref[...] = _dot(kc, _load_q_t(q_ref, u * ts, ts))
    for u, s_ref in enumerate(score_refs):
        _cmp_attn_subtile(s_ref[...], vc_ref, gates_ref, o_ref, sel_ref, pl.program_id(2) * tq + u * ts, u * ts,
                          ts, n_cmp)


def _cmp_attn_subtile(s, vc_ref, gates_ref, o_ref, sel_ref, t0, lane0, tq, n_cmp):
    g = pl.program_id(1)
    out_lanes = slice(lane0, lane0 + tq)

    tpos = t0 + lax.broadcasted_iota(jnp.int32, (n_cmp, tq), 1)
    cend = lax.broadcasted_iota(jnp.int32, (n_cmp, tq), 0) * CMP_STRIDE + (CMP_BLOCK - 1)
    visible = cend <= tpos
    bias = _lane_tile(jnp.where(visible, 0.0, MASK_VALUE), HEADS_PER_GROUP)
    okf = _lane_tile(jnp.where(visible, 1.0, 0.0), HEADS_PER_GROUP)
    s = s + bias
    m = jnp.max(s, axis=0, keepdims=True)
    e = jnp.exp2(s - m) * okf
    inv = 1.0 / jnp.maximum(jnp.sum(e, axis=0, keepdims=True), 1e-30)
    p = e * inv
    o = _dot(vc_ref[...], p.astype(BF16))

    for r in range(HEADS_PER_GROUP):
        gate = _gate_row(gates_ref, 0, g, r)[:, out_lanes]
        o_ref[r * HEAD_DIM:(r + 1) * HEAD_DIM, out_lanes] = gate * o[:, r * tq:(r + 1) * tq]

    psum = p[:, 0:tq]
    for r in range(1, HEADS_PER_GROUP):
        psum = psum + p[:, r * tq:(r + 1) * tq]
    n_sel_rows = LANES // 2
    jrow = lax.broadcasted_iota(jnp.int32, (n_sel_rows, n_cmp), 0)
    ccol = lax.broadcasted_iota(jnp.int32, (n_sel_rows, n_cmp), 1)
    c_lo, c_hi = ccol * CMP_STRIDE, ccol * CMP_STRIDE + (CMP_BLOCK - 1)
    j_lo, j_hi = jrow * SEL_BLOCK, jrow * SEL_BLOCK + (SEL_BLOCK - 1)
    ov = jnp.maximum(jnp.minimum(c_hi, j_hi) - jnp.maximum(c_lo, j_lo) + 1, 0).astype(F32) / CMP_STRIDE
    ov = ov.astype(BF16)
    hi, mid, lo = _split3(psum)
    imp = _dot(ov, hi) + _dot(ov, mid) + _dot(ov, lo)

    blk = lax.broadcasted_iota(jnp.int32, (n_sel_rows, tq), 0)
    cur = (t0 + lax.broadcasted_iota(jnp.int32, (n_sel_rows, tq), 1)) // SEL_BLOCK
    forced = (blk == 0) | (blk == cur) | (blk == cur - 1)
    key = jnp.where(blk > cur, -1.0, imp)
    key = jnp.where(forced, 1e30, key)
    sel = jnp.zeros((n_sel_rows, tq), F32)
    for _ in range(SEL_TOP):
        mx = jnp.max(key, axis=0, keepdims=True)
        first = jnp.min(jnp.where(key == mx, blk, LANES), axis=0, keepdims=True)
        pick = blk == first
        sel = jnp.where(pick, 1.0, sel)
        key = jnp.where(pick, -2.0, key)
    sel = jnp.where(blk > cur, 0.0, sel)
    sel_ref[:, out_lanes] = jnp.concatenate([sel, jnp.zeros_like(sel)], axis=0).astype(BF16)


def _cmp_attn(feat_t, kc, vc_t, gates_t, batch, seq, tq, ts):
    t = feat_t.shape[1]
    nq = seq // tq
    n_cmp = kc.shape[2]
    assert seq // SEL_BLOCK <= LANES // 2
    return pl.pallas_call(
        functools.partial(_cmp_attn_body, tq=tq, n_cmp=n_cmp),
        grid=(batch, NSA_GROUPS, nq),
        in_specs=[
            pl.BlockSpec((GROUP_WIDTH, tq), lambda b, g, i: (g, b * nq + i)),
            pl.BlockSpec((None, None, n_cmp, HEAD_DIM), lambda b, g, i: (g, b, 0, 0)),
            pl.BlockSpec((None, None, HEAD_DIM, n_cmp), lambda b, g, i: (g, b, 0, 0)),
            pl.BlockSpec((LANES, tq), lambda b, g, i: (0, b * nq + i)),
        ],
        out_specs=[
            pl.BlockSpec((GROUP_WIDTH, tq), lambda b, g, i: (g, b * nq + i)),
            pl.BlockSpec((None, LANES, tq), lambda b, g, i: (g, 0, b * nq + i)),
        ],
        out_shape=[
            jax.ShapeDtypeStruct((D_MODEL, t), F32),
            jax.ShapeDtypeStruct((NSA_GROUPS, LANES, t), BF16),
        ],
        scratch_shapes=[pltpu.VMEM((n_cmp, HEADS_PER_GROUP * ts), F32) for _ in range(tq // ts)],
        compiler_params=_params(("parallel", "parallel", "parallel")),
        name="cmp_attn",
    )(feat_t, kc, vc_t, gates_t)


def _sel_attn_body(q_ref, k_ref, e_ref, v_ref, sel_ref, cos_ref, sin_ref, gates_ref, oin_ref, o_ref,
                   m_ref, l_ref, acc_ref, sa_ref, sb_ref, *, tq, tk):
    g = pl.program_id(1)
    qi = pl.program_id(2)
    t0 = qi * tq
    q = _rope_feature_major(_load_q_t(q_ref), _lane_tile(cos_ref[...], HEADS_PER_GROUP),
                            _lane_tile(sin_ref[...], HEADS_PER_GROUP))
    unselected = _lane_tile(sel_ref[...] - 1.0, HEADS_PER_GROUP)
    q_aug = jnp.concatenate([q, unselected.astype(BF16)], axis=0)

    m_ref[...] = jnp.full(m_ref.shape, MASK_VALUE, F32)
    l_ref[...] = jnp.zeros(l_ref.shape, F32)
    acc_ref[...] = jnp.zeros(acc_ref.shape, F32)

    def scores(ki):
        k0 = pl.multiple_of(ki * tk, tk)
        k_aug = jnp.concatenate([k_ref[pl.ds(k0, tk), :], e_ref[pl.ds(k0, tk), :]], axis=1)
        return _dot(k_aug, q_aug)

    def accumulate(s, ki, causal):
        k0 = pl.multiple_of(ki * tk, tk)
        if causal:
            kpos = k0 + lax.broadcasted_iota(jnp.int32, (tk, tq), 0)
            tpos = t0 + lax.broadcasted_iota(jnp.int32, (tk, tq), 1)
            s = s + _lane_tile(jnp.where(kpos <= tpos, 0.0, MASK_VALUE), HEADS_PER_GROUP)
        m_prev = m_ref[...]
        m_new = jnp.maximum(m_prev, jnp.max(s, axis=0, keepdims=True))
        alpha = jnp.exp2(m_prev - m_new)
        p = jnp.exp2(s - m_new)
        l_ref[...] = alpha * l_ref[...] + jnp.sum(p, axis=0, keepdims=True)
        acc_ref[...] = alpha * acc_ref[...] + _dot(v_ref[:, pl.ds(k0, tk)], p.astype(BF16))
        m_ref[...] = m_new

    n_before = t0 // tk
    sa_ref[...] = scores(0)

    def body(j, carry):
        sb_ref[...] = scores(2 * j + 1)
        accumulate(sa_ref[...], 2 * j, causal=False)
        sa_ref[...] = scores(2 * j + 2)
        accumulate(sb_ref[...], 2 * j + 1, causal=False)
        return carry

    lax.fori_loop(0, n_before // 2, body, 0)

    @pl.when(n_before % 2 == 1)
    def _():
        sb_ref[...] = scores(n_before)
        accumulate(sa_ref[...], n_before - 1, causal=False)
        accumulate(sb_ref[...], n_before, causal=True)

    @pl.when(n_before % 2 == 0)
    def _():
        accumulate(sa_ref[...], n_before, causal=True)

    o = acc_ref[...] * (1.0 / l_ref[...])
    for r in range(HEADS_PER_GROUP):
        rows = slice(r * HEAD_DIM, (r + 1) * HEAD_DIM)
        o_ref[rows, :] = oin_ref[rows, :] + _gate_row(gates_ref, 1, g, r) * o[:, r * tq:(r + 1) * tq]


def _sel_attn(feat_t, tok, selm, cos_t, sin_t, gates_t, oin, batch, seq, tq, tk):
    t = feat_t.shape[1]
    nq = seq // tq
    lanes = HEADS_PER_GROUP * tq
    v_row0 = NSA_HEADS
    assert tk % tq == 0 and seq % tk == 0
    key_block = jnp.arange(seq, dtype=jnp.int32)[:, None] // SEL_BLOCK
    block_of_key = jnp.where(key_block == jnp.arange(LANES, dtype=jnp.int32)[None, :], BLOCK_MASK_BIG, 0.0)
    return pl.pallas_call(
        functools.partial(_sel_attn_body, tq=tq, tk=tk),
        grid=(batch, NSA_GROUPS, nq),
        in_specs=[
            pl.BlockSpec((GROUP_WIDTH, tq), lambda b, g, i: (g, b * nq + i)),
            pl.BlockSpec((None, seq, HEAD_DIM), lambda b, g, i: (g, b, 0)),
            pl.BlockSpec((seq, LANES), lambda b, g, i: (0, 0)),
            pl.BlockSpec((HEAD_DIM, seq), lambda b, g, i: (v_row0 + g, b)),
            pl.BlockSpec((None, LANES, tq), lambda b, g, i: (g, 0, b * nq + i)),
            pl.BlockSpec((ROPE_HALF, tq), lambda b, g, i: (0, i)),
            pl.BlockSpec((ROPE_HALF, tq), lambda b, g, i: (0, i)),
            pl.BlockSpec((LANES, tq), lambda b, g, i: (0, b * nq + i)),
            pl.BlockSpec((GROUP_WIDTH, tq), lambda b, g, i: (g, b * nq + i)),
        ],
        out_specs=pl.BlockSpec((GROUP_WIDTH, tq), lambda b, g, i: (g, b * nq + i)),
        out_shape=jax.ShapeDtypeStruct((D_MODEL, t), F32),
        scratch_shapes=[
            pltpu.VMEM((1, lanes), F32),
            pltpu.VMEM((1, lanes), F32),
            pltpu.VMEM((HEAD_DIM, lanes), F32),
            pltpu.VMEM((tk, lanes), F32),
            pltpu.VMEM((tk, lanes), F32),
        ],
        compiler_params=_params(("parallel", "parallel", "parallel")),
        name="attn_sel",
    )(feat_t, tok, block_of_key.astype(BF16), feat_t, selm, cos_t, sin_t, gates_t, oin)


def _win_attn_body(q_ref, k_ref, v_ref, bias_ref, cos_ref, sin_ref, gates_ref, oin_ref, o_ref, *score_refs, tq):
    g = pl.program_id(1)
    n_sub = len(score_refs)
    ts = tq // n_sub
    band = WINDOW + ts
    starts = []
    for u, s_ref in enumerate(score_refs):
        t0 = pl.program_id(2) * tq + u * ts
        lanes = slice(u * ts, (u + 1) * ts)
        q = _rope_feature_major(_load_q_t(q_ref, u * ts, ts), _lane_tile(cos_ref[:, lanes], HEADS_PER_GROUP),
                                _lane_tile(sin_ref[:, lanes], HEADS_PER_GROUP))
        start = pl.multiple_of(jnp.maximum(t0 - WINDOW, 0), ts)
        s_ref[...] = _dot(k_ref[pl.ds(start, band), :], q)
        starts.append((t0, start, lanes))
    for s_ref, (t0, start, lanes) in zip(score_refs, starts):
        bias = bias_ref[jnp.minimum(t0 // ts, WINDOW // ts)]
        s = s_ref[...] + _lane_tile(bias, HEADS_PER_GROUP)
        m = jnp.max(s, axis=0, keepdims=True)
        p = jnp.exp2(s - m)
        inv = 1.0 / jnp.sum(p, axis=0, keepdims=True)
        o = _dot(v_ref[:, pl.ds(start, band)], p.astype(BF16)) * inv
        for r in range(HEADS_PER_GROUP):
            rows = slice(r * HEAD_DIM, (r + 1) * HEAD_DIM)
            total = oin_ref[rows, lanes] + _gate_row(gates_ref, 2, g, r)[:, lanes] * o[:, r * ts:(r + 1) * ts]
            o_ref[lanes, rows] = total.T.astype(BF16)


def _win_attn(feat_t, tok, cos_t, sin_t, gates_t, oin, batch, seq, tq, ts):
    t = feat_t.shape[1]
    nq = seq // tq
    v_row0 = NSA_HEADS + NSA_GROUPS
    assert seq >= WINDOW + ts and tq % ts == 0 and WINDOW % ts == 0
    band = WINDOW + ts
    rel = jnp.arange(band, dtype=jnp.int32)[None, :, None] - jnp.arange(ts, dtype=jnp.int32)[None, None, :]
    d = (jnp.arange(WINDOW // ts + 1, dtype=jnp.int32) * ts)[:, None, None]
    win_bias = jnp.where((rel <= d) & (rel > d - WINDOW), 0.0, MASK_VALUE).astype(F32)
    return pl.pallas_call(
        functools.partial(_win_attn_body, tq=tq),
        grid=(batch, NSA_GROUPS, nq),
        in_specs=[
            pl.BlockSpec((GROUP_WIDTH, tq), lambda b, g, i: (g, b * nq + i)),
            pl.BlockSpec((None, seq, HEAD_DIM), lambda b, g, i: (NSA_GROUPS + g, b, 0)),
            pl.BlockSpec((HEAD_DIM, seq), lambda b, g, i: (v_row0 + g, b)),
            pl.BlockSpec((WINDOW // ts + 1, band, ts), lambda b, g, i: (0, 0, 0)),
            pl.BlockSpec((ROPE_HALF, tq), lambda b, g, i: (0, i)),
            pl.BlockSpec((ROPE_HALF, tq), lambda b, g, i: (0, i)),
            pl.BlockSpec((LANES, tq), lambda b, g, i: (0, b * nq + i)),
            pl.BlockSpec((GROUP_WIDTH, tq), lambda b, g, i: (g, b * nq + i)),
        ],
        out_specs=pl.BlockSpec((tq, GROUP_WIDTH), lambda b, g, i: (b * nq + i, g)),
        out_shape=jax.ShapeDtypeStruct((t, D_MODEL), BF16),
        scratch_shapes=[pltpu.VMEM((WINDOW + ts, HEADS_PER_GROUP * ts), F32) for _ in range(tq // ts)],
        compiler_params=_params(("parallel", "parallel", "parallel")),
        name="attn_win",
    )(feat_t, tok, feat_t, win_bias, cos_t, sin_t, gates_t, oin)


def _nsa_mixer(h, norm_rows, layer, layer_j, batch, seq, w_in, w_out, phi_pe, phi_w1, phi_b1, phi_w2):
    t = h.shape[0]
    w_in_t = jnp.swapaxes(w_in, 1, 2)
    w_gate_t = jnp.pad(w_in_t[layer_j, QKV_WIDTH:, :], ((0, LANES - N_GATES), (0, 0)))
    cos_tok, sin_tok, cos_t, sin_t = _rope_tables(seq)

    feat_t, gates_t, xn = _nsa_inproj_feat(h, norm_rows, layer, w_in_t, layer_j, w_gate_t, tm=1024)
    x_cmp, tok = _nsa_inproj_tok(xn, w_in_t, layer_j, cos_tok, sin_tok, seq, tm=min(1024, seq))

    pe = phi_pe.reshape(phi_pe.shape[0], 2, 1, CMP_BLOCK * HEAD_DIM)
    b1 = phi_b1.reshape(phi_b1.shape[0], 2, 1, CMP_HIDDEN)
    kc = _compress(x_cmp, 0, pe, phi_w1, b1, phi_w2, layer_j, batch, feature_major=False)
    vc_t = _compress(x_cmp, 1, pe, phi_w1, b1, phi_w2[layer_j, 1].T, layer_j, batch, feature_major=True)

    o1, selm = _cmp_attn(feat_t, kc, vc_t, gates_t, batch, seq, tq=1024, ts=256)
    o2 = _sel_attn(feat_t, tok, selm, cos_t, sin_t, gates_t, o1, batch, seq, tq=256, tk=512)
    o3 = _win_attn(feat_t, tok, cos_t, sin_t, gates_t, o2, batch, seq, tq=1024, ts=256)
    return _outproj(o3, w_out, layer_j, h, norm_rows, layer, 3, tm=min(1024, t), tn=512)


def _conv_inproj_body(h_ref, halo_ref, g_ref, wb_ref, wc_ref, wu_ref, cw_ref, o_ref, xn_ref, inv_ref, *,
                      tiles_per_seq):
    i = pl.program_id(0)
    j = pl.program_id(1)

    def channel_tile():
        x = xn_ref[...]
        bg = _dot(x, wb_ref[...].astype(BF16))
        z = _dot(x, wc_ref[...].astype(BF16)) * _dot(x, wu_ref[...].astype(BF16))
        cw = cw_ref[...]
        conv = cw[2:3] * z + cw[1:2] * pltpu.roll(z, 1, axis=0) + cw[0:1] * pltpu.roll(z, 2, axis=0)
        o_ref[...] = (bg * conv)[CONV_HALO:].astype(BF16)

    @pl.when(j == 0)
    def _():
        keep = jnp.where(i % tiles_per_seq == 0, 0.0, 1.0)
        xn_ref[0:CONV_HALO] = (_rms(halo_ref[...], g_ref[...]) * keep).astype(BF16)
        xn_ref[CONV_HALO:] = _rms_ref(h_ref, inv_ref, g_ref).astype(BF16)
        channel_tile()

    @pl.when(j > 0)
    def _():
        channel_tile()


def _conv_inproj(h, norm_rows, layer, w_in, conv_w, layer_j, seq, tm, tn):
    t = h.shape[0]
    nj = D_MODEL // tn
    halo_blocks = tm // CONV_HALO
    return pl.pallas_call(
        functools.partial(_conv_inproj_body, tiles_per_seq=seq // tm),
        grid=(t // tm, nj),
        in_specs=[
            pl.BlockSpec((tm, D_MODEL), lambda i, j: (i, 0)),
            pl.BlockSpec((CONV_HALO, D_MODEL), lambda i, j: (jnp.maximum(i * halo_blocks - 1, 0), 0)),
            _norm_row_spec(layer, 2),
            pl.BlockSpec((None, D_MODEL, tn), lambda i, j: (layer_j, 0, j)),
            pl.BlockSpec((None, D_MODEL, tn), lambda i, j: (layer_j, 0, nj + j)),
            pl.BlockSpec((None, D_MODEL, tn), lambda i, j: (layer_j, 0, 2 * nj + j)),
            pl.BlockSpec((None, CONV_WIDTH, tn), lambda i, j: (layer_j, 0, j)),
        ],
        out_specs=pl.BlockSpec((tm, tn), lambda i, j: (i, j)),
        out_shape=jax.ShapeDtypeStruct((t, D_MODEL), BF16),
        scratch_shapes=[pltpu.VMEM((tm + CONV_HALO, D_MODEL), BF16), _inv_scratch(tm)],
        compiler_params=_params(("parallel", "arbitrary")),
        name="conv_inproj",
    )(h, h, norm_rows, w_in, w_in, w_in, conv_w)


def _sgu_inproj_body(h_ref, g_ref, w_ref, lng_ref, lnb_ref, ws_ref, bs_ref, o_ref, xn_ref, z_ref, inv_ref, *,
                     tm, tn):
    j = pl.program_id(1)

    def column_tile(col):
        z_ref[:, pl.ds(col, tn)] = jax.nn.gelu(_dot(xn_ref[...], w_ref[...].astype(BF16)))

    @pl.when(j == 0)
    def _():
        xn_ref[...] = _rms_ref(h_ref, inv_ref, g_ref).astype(BF16)
        column_tile(0)

    @pl.when(j > 0)
    def _():
        column_tile(pl.multiple_of(j * tn, tn))

    @pl.when(j == pl.num_programs(1) - 1)
    def _():
        row = lax.broadcasted_iota(jnp.int32, (SGU_CHUNK, SGU_CHUNK), 0)
        colm = lax.broadcasted_iota(jnp.int32, (SGU_CHUNK, SGU_CHUNK), 1)
        ws = [jnp.where(colm <= row, ws_ref[grp], 0.0).astype(BF16) for grp in range(SGU_GROUPS)]
        bs = bs_ref[...]
        lng, lnb = lng_ref[...], lnb_ref[...]
        for c in range(tm // SGU_CHUNK):
            rws = slice(c * SGU_CHUNK, (c + 1) * SGU_CHUNK)
            v = z_ref[rws, D_MODEL:]
            mu = jnp.mean(v, axis=-1, keepdims=True)
            var = jnp.mean(jnp.square(v - mu), axis=-1, keepdims=True)
            vn = ((v - mu) * lax.rsqrt(var + EPS) * lng + lnb).astype(BF16)
            for grp in range(SGU_GROUPS):
                cols = slice(grp * SGU_GROUP_DIM, (grp + 1) * SGU_GROUP_DIM)
                sv = _dot(ws[grp], vn[:, cols]) + bs[:, grp:grp + 1]
                o_ref[rws, cols] = (z_ref[rws, cols] * sv).astype(BF16)


def _sgu_inproj(h, norm_rows, layer, w_in, ln_g, ln_b, w_s, b_s_t, layer_j, tm, tn):
    t = h.shape[0]
    return pl.pallas_call(
        functools.partial(_sgu_inproj_body, tm=tm, tn=tn),
        grid=(t // tm, 2 * D_MODEL // tn),
        in_specs=[
            pl.BlockSpec((tm, D_MODEL), lambda i, j: (i, 0)),
            _norm_row_spec(layer, 2),
            pl.BlockSpec((None, D_MODEL, tn), lambda i, j: (layer_j, 0, j)),
            pl.BlockSpec((None, 1, D_MODEL), lambda i, j: (layer_j, 0, 0)),
            pl.BlockSpec((None, 1, D_MODEL), lambda i, j: (layer_j, 0, 0)),
            pl.BlockSpec((None, SGU_GROUPS, SGU_CHUNK, SGU_CHUNK), lambda i, j: (layer_j, 0, 0, 0)),
            pl.BlockSpec((None, SGU_CHUNK, SGU_GROUPS), lambda i, j: (layer_j, 0, 0)),
        ],
        out_specs=pl.BlockSpec((tm, D_MODEL), lambda i, j: (i, 0)),
        out_shape=jax.ShapeDtypeStruct((t, D_MODEL), BF16),
        scratch_shapes=[pltpu.VMEM((tm, D_MODEL), BF16), pltpu.VMEM((tm, 2 * D_MODEL), F32), _inv_scratch(tm)],
        compiler_params=_params(("parallel", "arbitrary")),
        name="sgu_inproj",
    )(h, norm_rows, w_in, ln_g, ln_b, w_s, b_s_t)


def kernel(x, p, norm_g, ffn1_wg, ffn1_wu, ffn1_wd, ffn2_wg, ffn2_wu, ffn2_wd, ple_wg, ple_wp, nsa_w_in, nsa_w_out, nsa_phi_pe, nsa_phi_w1, nsa_phi_b1, nsa_phi_w2, conv_w_in, conv_w, conv_w_out, sgu_w_in, sgu_ln_g, sgu_ln_b, sgu_w_s, sgu_b_s, sgu_w_out):
    batch, seq, d = x.shape
    depth = p.shape[0]
    t = batch * seq
    assert d == D_MODEL and seq % 512 == 0
    h = x.reshape(t, d)
    p2 = p.reshape(depth, t, PLE_DIM)
    norm_rows = norm_g.reshape(depth * N_NORMS, 1, d)
    sgu_ln_g3 = sgu_ln_g.reshape(-1, 1, d)
    sgu_ln_b3 = sgu_ln_b.reshape(-1, 1, d)
    sgu_b_s_t = jnp.swapaxes(sgu_b_s, 1, 2)
    tm_big = min(1024, t)

    for layer in range(depth):
        layer_j = layer // N_MIXERS
        h = _ffn(h, norm_rows, layer, 0, 1, ffn1_wg, ffn1_wu, ffn1_wd, tm=tm_big, tf=256)
        kind = layer % N_MIXERS
        if kind == 0:
            h = _nsa_mixer(h, norm_rows, layer, layer_j, batch, seq, nsa_w_in, nsa_w_out, nsa_phi_pe,
                           nsa_phi_w1, nsa_phi_b1, nsa_phi_w2)
        elif kind == 1:
            a = _conv_inproj(h, norm_rows, layer, conv_w_in, conv_w, layer_j, seq, tm=min(1024, seq), tn=512)
            h = _outproj(a, conv_w_out, layer_j, h, norm_rows, layer, 3, tm=tm_big, tn=512)
        else:
            a = _sgu_inproj(h, norm_rows, layer, sgu_w_in, sgu_ln_g3, sgu_ln_b3, sgu_w_s, sgu_b_s_t, layer_j,
                            tm=tm_big, tn=256)
            h = _outproj(a, sgu_w_out, layer_j, h, norm_rows, layer, 3, tm=tm_big, tn=512)
        h = _ffn(h, norm_rows, layer, 4, 5, ffn2_wg, ffn2_wu, ffn2_wd, tm=tm_big, tf=256)
        h = _ple(h, p2, norm_rows, layer, ple_wg, ple_wp, tm=tm_big, tn=512)
    return h.reshape(batch, seq, d)
```

```python
import functools
import math

import jax
import jax.numpy as jnp
from jax import lax
from jax.experimental import pallas as pl
from jax.experimental.pallas import tpu as pltpu

F32 = jnp.float32
BF16 = jnp.bfloat16

EPS = 1e-6
LANES = 128
D_MODEL = 2048
D_FF = 5632
PLE_DIM = 256
N_NORMS = 8
N_MIXERS = 3
NSA_HEADS = 16
NSA_GROUPS = 4
HEADS_PER_GROUP = NSA_HEADS // NSA_GROUPS
HEAD_DIM = D_MODEL // NSA_HEADS
GROUP_WIDTH = HEADS_PER_GROUP * HEAD_DIM
KV_WIDTH = NSA_GROUPS * HEAD_DIM
ROPE_DIM = HEAD_DIM // 4
ROPE_HALF = ROPE_DIM // 2
ROPE_THETA = 500000.0
CMP_BLOCK = 32
CMP_STRIDE = 16
CMP_HIDDEN = 256
SEL_BLOCK = 64
SEL_TOP = 16
WINDOW = 512
QKV_WIDTH = NSA_HEADS * HEAD_DIM + 6 * KV_WIDTH
N_GATES = 3 * NSA_HEADS
SGU_CHUNK = 128
SGU_GROUPS = 8
SGU_GROUP_DIM = D_MODEL // SGU_GROUPS
CONV_WIDTH = 3
CONV_HALO = 16

MASK_VALUE = -1e30
BLOCK_MASK_BIG = 2.0 ** 100
SCORE_SCALE = HEAD_DIM ** -0.5
EXP2_SCALE = SCORE_SCALE * math.log2(math.e)
VMEM_LIMIT = 60 * 1024 * 1024

TOKEN_TILE = 1024
FFN_SLICE = 256
PROJ_COLS = 512
SGU_COLS = 256
ATTN_STEP = 1024
ATTN_SUBTILE = 256
SEL_QUERIES = 512
SEL_KEYS = 512


def _params(semantics):
    return pltpu.CompilerParams(dimension_semantics=semantics, vmem_limit_bytes=VMEM_LIMIT)


def _rms(x, g):
    return x * lax.rsqrt(jnp.mean(x * x, axis=-1, keepdims=True) + EPS) * g


def _rms_ref(src_ref, inv_ref, g_ref):
    x = src_ref[...]
    inv_ref[...] = lax.rsqrt(jnp.mean(x * x, axis=-1, keepdims=True) + EPS)
    return src_ref[...] * inv_ref[...] * g_ref[...]


def _inv_scratch(rows):
    return pltpu.VMEM((rows, 1), F32)


def _dot(a, b):
    return jnp.dot(a, b, preferred_element_type=F32)


def _dot_nt(a, b):
    return lax.dot_general(a, b, (((1,), (1,)), ((), ())), preferred_element_type=F32)


def _norm_row_spec(layer, k):
    idx = layer * N_NORMS + k
    return pl.BlockSpec((None, 1, D_MODEL), lambda i, j: (idx, 0, 0))


def _ffn_body(h_ref, gpre_ref, gpost_ref, wg_ref, wu_ref, wd_ref, o_ref, xn_ref, inv_ref):
    f = pl.program_id(1)

    def down_proj_slice():
        x = xn_ref[...]
        gate = _dot(x, wg_ref[...].astype(BF16))
        up = _dot(x, wu_ref[...].astype(BF16))
        act = (gate * jax.nn.sigmoid(gate) * up).astype(BF16)
        return _dot(act, wd_ref[...].astype(BF16))

    @pl.when(f == 0)
    def _():
        xn_ref[...] = _rms_ref(h_ref, inv_ref, gpre_ref).astype(BF16)
        o_ref[...] = down_proj_slice()

    @pl.when(f > 0)
    def _():
        o_ref[...] += down_proj_slice()

    @pl.when(f == pl.num_programs(1) - 1)
    def _():
        o_ref[...] = h_ref[...] + 0.5 * _rms_ref(o_ref, inv_ref, gpost_ref)


def _ffn(h, norm_rows, layer, k_pre, k_post, wg, wu, wd, tm, tf):
    t = h.shape[0]
    return pl.pallas_call(
        _ffn_body,
        grid=(t // tm, D_FF // tf),
        in_specs=[
            pl.BlockSpec((tm, D_MODEL), lambda i, f: (i, 0)),
            _norm_row_spec(layer, k_pre),
            _norm_row_spec(layer, k_post),
            pl.BlockSpec((None, D_MODEL, tf), lambda i, f: (layer, 0, f)),
            pl.BlockSpec((None, D_MODEL, tf), lambda i, f: (layer, 0, f)),
            pl.BlockSpec((None, tf, D_MODEL), lambda i, f: (layer, f, 0)),
        ],
        out_specs=pl.BlockSpec((tm, D_MODEL), lambda i, f: (i, 0)),
        out_shape=jax.ShapeDtypeStruct((t, D_MODEL), F32),
        scratch_shapes=[pltpu.VMEM((tm, D_MODEL), BF16), _inv_scratch(tm)],
        compiler_params=_params(("parallel", "arbitrary")),
        name="ffn",
    )(h, norm_rows, norm_rows, wg, wu, wd)


def _resident_cols(n_col_tiles):
    return lambda i, j: jnp.where(i == 0, j, n_col_tiles - 1)


def _outproj_body(a_ref, w_ref, h_ref, g_ref, o_ref, inv_ref, wb_ref, *, tn):
    j = pl.program_id(1)
    col = pl.multiple_of(j * tn, tn)

    @pl.when(pl.program_id(0) == 0)
    def _():
        wb_ref[:, pl.ds(col, tn)] = w_ref[...].astype(BF16)

    o_ref[:, pl.ds(col, tn)] = _dot(a_ref[...], wb_ref[:, pl.ds(col, tn)])

    @pl.when(j == pl.num_programs(1) - 1)
    def _():
        o_ref[...] = h_ref[...] + _rms_ref(o_ref, inv_ref, g_ref)


def _outproj(a, w, w_layer, h, norm_rows, layer, k_norm, tm, tn):
    t, kdim = a.shape
    w_col = _resident_cols(D_MODEL // tn)
    return pl.pallas_call(
        functools.partial(_outproj_body, tn=tn),
        grid=(t // tm, D_MODEL // tn),
        in_specs=[
            pl.BlockSpec((tm, kdim), lambda i, j: (i, 0)),
            pl.BlockSpec((None, kdim, tn), lambda i, j: (w_layer, 0, w_col(i, j))),
            pl.BlockSpec((tm, D_MODEL), lambda i, j: (i, 0)),
            _norm_row_spec(layer, k_norm),
        ],
        out_specs=pl.BlockSpec((tm, D_MODEL), lambda i, j: (i, 0)),
        out_shape=jax.ShapeDtypeStruct((t, D_MODEL), F32),
        scratch_shapes=[_inv_scratch(tm), pltpu.VMEM((kdim, D_MODEL), BF16)],
        compiler_params=_params(("arbitrary", "arbitrary")),
        name="outproj",
    )(a, w, h, norm_rows)


def _ple_body(h_ref, p_ref, gpre_ref, gpost_ref, wg_ref, wp_ref, o_ref, xn_ref, pb_ref, inv_ref, wgb_ref, wpb_ref,
              *, tn):
    j = pl.program_id(1)

    def column_tile(col):
        cols = pl.ds(col, tn)

        @pl.when(pl.program_id(0) == 0)
        def _():
            wgb_ref[:, cols] = wg_ref[...].astype(BF16)
            wpb_ref[:, cols] = wp_ref[...].astype(BF16)

        gate = jax.nn.sigmoid(_dot(xn_ref[...], wgb_ref[:, cols]))
        emb = _dot(pb_ref[...], wpb_ref[:, cols])
        o_ref[:, cols] = gate * emb

    @pl.when(j == 0)
    def _():
        xn_ref[...] = _rms_ref(h_ref, inv_ref, gpre_ref).astype(BF16)
        pb_ref[...] = p_ref[...].astype(BF16)
        column_tile(0)

    @pl.when(j > 0)
    def _():
        column_tile(pl.multiple_of(j * tn, tn))

    @pl.when(j == pl.num_programs(1) - 1)
    def _():
        o_ref[...] = h_ref[...] + _rms_ref(o_ref, inv_ref, gpost_ref)


def _ple(h, p, norm_rows, layer, wg, wp, tm, tn):
    t = h.shape[0]
    w_col = _resident_cols(D_MODEL // tn)
    return pl.pallas_call(
        functools.partial(_ple_body, tn=tn),
        grid=(t // tm, D_MODEL // tn),
        in_specs=[
            pl.BlockSpec((tm, D_MODEL), lambda i, j: (i, 0)),
            pl.BlockSpec((None, tm, PLE_DIM), lambda i, j: (layer, i, 0)),
            _norm_row_spec(layer, 6),
            _norm_row_spec(layer, 7),
            pl.BlockSpec((None, D_MODEL, tn), lambda i, j: (layer, 0, w_col(i, j))),
            pl.BlockSpec((None, PLE_DIM, tn), lambda i, j: (layer, 0, w_col(i, j))),
        ],
        out_specs=pl.BlockSpec((tm, D_MODEL), lambda i, j: (i, 0)),
        out_shape=jax.ShapeDtypeStruct((t, D_MODEL), F32),
        scratch_shapes=[pltpu.VMEM((tm, D_MODEL), BF16), pltpu.VMEM((tm, PLE_DIM), BF16), _inv_scratch(tm),
                        pltpu.VMEM((D_MODEL, D_MODEL), BF16), pltpu.VMEM((PLE_DIM, D_MODEL), BF16)],
        compiler_params=_params(("arbitrary", "arbitrary")),
        name="ple",
    )(h, p, norm_rows, norm_rows, wg, wp)


def _nsa_inproj_feat_body(h_ref, g_ref, w_ref, wg_ref, feat_ref, gates_ref, xn_ref, inv_ref, wb_ref, *,
                          n_q_tiles):
    j = pl.program_id(1)
    tn = w_ref.shape[0]

    def feature_tile():
        rows = pl.ds(pl.multiple_of(j * tn, tn), tn)

        @pl.when(pl.program_id(0) == 0)
        def _():
            wb_ref[rows, :] = w_ref[...].astype(BF16)

        row_scale = jnp.where(j < n_q_tiles, EXP2_SCALE, 1.0)
        feat_ref[...] = (_dot_nt(wb_ref[rows, :], xn_ref[...]) * row_scale).astype(BF16)

    @pl.when(j == 0)
    def _():
        xn = _rms_ref(h_ref, inv_ref, g_ref).astype(BF16)
        xn_ref[...] = xn
        gates_ref[...] = jax.nn.sigmoid(_dot_nt(wg_ref[...].astype(BF16), xn))
        feature_tile()

    @pl.when(j > 0)
    def _():
        feature_tile()


def _nsa_inproj_feat(h, norm_rows, layer, w_in_t, w_layer, w_gate_t, tm):
    t = h.shape[0]
    tn = KV_WIDTH
    n_q_tiles = NSA_HEADS * HEAD_DIM // tn
    n_feat = NSA_HEADS * HEAD_DIM + 2 * KV_WIDTH
    w_tile = _resident_cols(n_feat // tn)

    def w_block(i, j):
        jj = w_tile(i, j)
        return jj + 3 * (jj // n_q_tiles) + jj // (n_q_tiles + 1)

    return pl.pallas_call(
        functools.partial(_nsa_inproj_feat_body, n_q_tiles=n_q_tiles),
        grid=(t // tm, n_feat // tn),
        in_specs=[
            pl.BlockSpec((tm, D_MODEL), lambda i, j: (i, 0)),
            _norm_row_spec(layer, 2),
            pl.BlockSpec((None, tn, D_MODEL), lambda i, j: (w_layer, w_block(i, j), 0)),
            pl.BlockSpec((LANES, D_MODEL), lambda i, j: (0, 0)),
        ],
        out_specs=[
            pl.BlockSpec((tn, tm), lambda i, j: (j, i)),
            pl.BlockSpec((LANES, tm), lambda i, j: (0, i)),
            pl.BlockSpec((tm, D_MODEL), lambda i, j: (i, 0)),
        ],
        out_shape=[
            jax.ShapeDtypeStruct((n_feat, t), BF16),
            jax.ShapeDtypeStruct((LANES, t), F32),
            jax.ShapeDtypeStruct((t, D_MODEL), BF16),
        ],
        scratch_shapes=[_inv_scratch(tm), pltpu.VMEM((n_feat, D_MODEL), BF16)],
        compiler_params=_params(("arbitrary", "arbitrary")),
        name="nsa_inproj_feat",
    )(h, norm_rows, w_in_t, w_gate_t)


def _rope_token_major(x, cos, sin):
    lane = lax.broadcasted_iota(jnp.int32, x.shape, 1)
    partner = jnp.where(lane < ROPE_HALF, pltpu.roll(x, LANES - ROPE_HALF, axis=1),
                        pltpu.roll(x, ROPE_HALF, axis=1))
    return x * cos + partner * sin


def _nsa_inproj_tok_body(xn_ref, wc_ref, wk_ref, cos_ref, sin_ref, oc_ref, ok_ref, wcb_ref, wkb_ref):
    rows = pl.ds(pl.multiple_of(pl.program_id(1) * KV_WIDTH, KV_WIDTH), KV_WIDTH)

    @pl.when(pl.program_id(0) == 0)
    def _():
        wcb_ref[rows, :] = wc_ref[...].astype(BF16)
        wkb_ref[rows, :] = wk_ref[...].astype(BF16)

    xn = xn_ref[...]
    res_c = _dot_nt(xn, wcb_ref[rows, :])
    res_k = _dot_nt(xn, wkb_ref[rows, :])
    cos, sin = cos_ref[...], sin_ref[...]
    for c in range(NSA_GROUPS):
        cols = slice(c * LANES, (c + 1) * LANES)
        oc_ref[c] = res_c[:, cols]
        ok_ref[c] = _rope_token_major(res_k[:, cols], cos, sin).astype(BF16)


def _nsa_inproj_tok(xn, w_in_t, w_layer, cos_tok, sin_tok, seq, tm):
    t = xn.shape[0]
    per_seq = seq // tm
    q_blocks = NSA_HEADS * HEAD_DIM // KV_WIDTH
    slab_spec = pl.BlockSpec((NSA_GROUPS, tm, LANES), lambda i, j: (j, i, 0))
    w_step = _resident_cols(2)
    return pl.pallas_call(
        _nsa_inproj_tok_body,
        grid=(t // tm, 2),
        in_specs=[
            pl.BlockSpec((tm, D_MODEL), lambda i, j: (i, 0)),
            pl.BlockSpec((None, KV_WIDTH, D_MODEL), lambda i, j: (w_layer, q_blocks + w_step(i, j), 0)),
            pl.BlockSpec((None, KV_WIDTH, D_MODEL), lambda i, j: (w_layer, q_blocks + 2 + 2 * w_step(i, j), 0)),
            pl.BlockSpec((tm, LANES), lambda i, j: (i % per_seq, 0)),
            pl.BlockSpec((tm, LANES), lambda i, j: (i % per_seq, 0)),
        ],
        out_specs=[slab_spec, slab_spec],
        out_shape=[
            jax.ShapeDtypeStruct((2 * NSA_GROUPS, t, LANES), F32),
            jax.ShapeDtypeStruct((2 * NSA_GROUPS, t, LANES), BF16),
        ],
        scratch_shapes=[pltpu.VMEM((2 * KV_WIDTH, D_MODEL), BF16), pltpu.VMEM((2 * KV_WIDTH, D_MODEL), BF16)],
        compiler_params=_params(("arbitrary", "arbitrary")),
        name="nsa_inproj_tok",
    )(xn, w_in_t, w_in_t, cos_tok, sin_tok)


def _rope_tables(seq):
    inv_freq = jnp.power(jnp.float32(ROPE_THETA), -jnp.arange(0, ROPE_DIM, 2, dtype=F32) / ROPE_DIM)
    ang = jnp.arange(seq, dtype=F32)[:, None] * inv_freq[None, :]
    cos, sin = jnp.cos(ang), jnp.sin(ang)
    cos_tok = jnp.concatenate([cos, cos, jnp.ones((seq, LANES - ROPE_DIM), F32)], axis=1)
    sin_tok = jnp.concatenate([-sin, sin, jnp.zeros((seq, LANES - ROPE_DIM), F32)], axis=1)
    return cos_tok, sin_tok, cos.T, sin.T


def _compress_body(x_ref, pe_ref, w1_ref, b1_ref, w2_ref, o_ref, *, feature_major):
    n_chunks = x_ref.shape[0] // CMP_STRIDE
    w1 = w1_ref[...].astype(BF16)
    both = jnp.zeros((n_chunks, 2 * CMP_HIDDEN), F32)
    for l in range(CMP_STRIDE):
        x_l = x_ref[pl.ds(l, n_chunks, stride=CMP_STRIDE), :].astype(BF16)
        lo, hi = l * HEAD_DIM, (CMP_STRIDE + l) * HEAD_DIM
        w_l = jnp.concatenate([w1[lo:lo + HEAD_DIM], w1[hi:hi + HEAD_DIM]], axis=1)
        both = both + _dot(x_l, w_l)
    first = both[:, :CMP_HIDDEN]
    second = pltpu.roll(both[:, CMP_HIDDEN:], n_chunks - 1, axis=0)
    pe = jnp.broadcast_to(pe_ref[...], (8, CMP_BLOCK * HEAD_DIM)).astype(BF16)
    const = _dot(pe, w1)[0:1]
    hid = first + second + const + b1_ref[...]
    act = (hid * jax.nn.sigmoid(hid)).astype(BF16)
    if feature_major:
        o_ref[...] = _dot_nt(w2_ref[...].astype(BF16), act).astype(BF16)
    else:
        o_ref[...] = _dot(act, w2_ref[...].astype(BF16)).astype(BF16)


def _compress(x, which, pe, w1, b1, w2, layer_j, batch, feature_major):
    seq = x.shape[1] // batch
    n_chunks = seq // CMP_STRIDE
    if feature_major:
        w2_spec = pl.BlockSpec((HEAD_DIM, CMP_HIDDEN), lambda g, b: (0, 0))
        out_block, out_dims = (None, None, HEAD_DIM, n_chunks), (NSA_GROUPS, batch, HEAD_DIM, n_chunks)
    else:
        w2_spec = pl.BlockSpec((None, None, CMP_HIDDEN, HEAD_DIM), lambda g, b: (layer_j, which, 0, 0))
        out_block, out_dims = (None, None, n_chunks, HEAD_DIM), (NSA_GROUPS, batch, n_chunks, HEAD_DIM)
    return pl.pallas_call(
        functools.partial(_compress_body, feature_major=feature_major),
        grid=(NSA_GROUPS, batch),
        in_specs=[
            pl.BlockSpec((None, seq, HEAD_DIM), lambda g, b: (which * NSA_GROUPS + g, b, 0)),
            pl.BlockSpec((None, None, 1, CMP_BLOCK * HEAD_DIM), lambda g, b: (layer_j, which, 0, 0)),
            pl.BlockSpec((None, None, CMP_BLOCK * HEAD_DIM, CMP_HIDDEN), lambda g, b: (layer_j, which, 0, 0)),
            pl.BlockSpec((None, None, 1, CMP_HIDDEN), lambda g, b: (layer_j, which, 0, 0)),
            w2_spec,
        ],
        out_specs=pl.BlockSpec(out_block, lambda g, b: (g, b, 0, 0)),
        out_shape=jax.ShapeDtypeStruct(out_dims, BF16),
        compiler_params=_params(("parallel", "parallel")),
        name="compress",
    )(x, pe, w1, b1, w2)


def _lane_tile(x, n):
    return jnp.concatenate([x] * n, axis=1)


def _load_q_t(q_ref, lo=0, n=None):
    n = q_ref.shape[1] if n is None else n
    return jnp.concatenate([q_ref[r * HEAD_DIM:(r + 1) * HEAD_DIM, lo:lo + n] for r in range(HEADS_PER_GROUP)],
                           axis=1)


def _rope_feature_major(q, cos, sin):
    x1 = q[0:ROPE_HALF].astype(F32)
    x2 = q[ROPE_HALF:ROPE_DIM].astype(F32)
    r1 = (x1 * cos - x2 * sin).astype(BF16)
    r2 = (x2 * cos + x1 * sin).astype(BF16)
    return jnp.concatenate([r1, r2, q[ROPE_DIM:]], axis=0)


def _split3(x):
    hi = x.astype(BF16)
    r1 = x - hi.astype(F32)
    mid = r1.astype(BF16)
    lo = (r1 - mid.astype(F32)).astype(BF16)
    return hi, mid, lo


def _gate_row(gates_ref, branch, g, r):
    return gates_ref[pl.ds(branch * NSA_HEADS + g * HEADS_PER_GROUP + r, 1), :]


def _cmp_attn_body(q_ref, kc_ref, vc_ref, gates_ref, o_ref, sel_ref, *score_refs, tq, n_cmp):
    n_sub = len(score_refs)
    ts = tq // n_sub
    kc = kc_ref[...]
    for u, s_ref in enumerate(score_refs):
        s_ref[...] = _dot(kc, _load_q_t(q_ref, u * ts, ts))
    for u, s_ref in enumerate(score_refs):
        _cmp_attn_subtile(s_ref[...], vc_ref, gates_ref, o_ref, sel_ref, pl.program_id(2) * tq + u * ts, u * ts,
                          ts, n_cmp)


def _cmp_attn_subtile(s, vc_ref, gates_ref, o_ref, sel_ref, t0, lane0, tq, n_cmp):
    g = pl.program_id(1)
    out_lanes = slice(lane0, lane0 + tq)

    tpos = t0 + lax.broadcasted_iota(jnp.int32, (n_cmp, tq), 1)
    cend = lax.broadcasted_iota(jnp.int32, (n_cmp, tq), 0) * CMP_STRIDE + (CMP_BLOCK - 1)
    visible = cend <= tpos
    bias = _lane_tile(jnp.where(visible, 0.0, MASK_VALUE), HEADS_PER_GROUP)
    okf = _lane_tile(jnp.where(visible, 1.0, 0.0), HEADS_PER_GROUP)
    s = s + bias
    m = jnp.max(s, axis=0, keepdims=True)
    e = jnp.exp2(s - m) * okf
    inv = 1.0 / jnp.maximum(jnp.sum(e, axis=0, keepdims=True), 1e-30)
    p = e * inv
    o = _dot(vc_ref[...], p.astype(BF16))

    for r in range(HEADS_PER_GROUP):
        gate = _gate_row(gates_ref, 0, g, r)[:, out_lanes]
        o_ref[r * HEAD_DIM:(r + 1) * HEAD_DIM, out_lanes] = gate * o[:, r * tq:(r + 1) * tq]

    psum = p[:, 0:tq]
    for r in range(1, HEADS_PER_GROUP):
        psum = psum + p[:, r * tq:(r + 1) * tq]
    n_sel_rows = LANES // 2
    jrow = lax.broadcasted_iota(jnp.int32, (n_sel_rows, n_cmp), 0)
    ccol = lax.broadcasted_iota(jnp.int32, (n_sel_rows, n_cmp), 1)
    c_lo, c_hi = ccol * CMP_STRIDE, ccol * CMP_STRIDE + (CMP_BLOCK - 1)
    j_lo, j_hi = jrow * SEL_BLOCK, jrow * SEL_BLOCK + (SEL_BLOCK - 1)
    ov = jnp.maximum(jnp.minimum(c_hi, j_hi) - jnp.maximum(c_lo, j_lo) + 1, 0).astype(F32) / CMP_STRIDE
    ov = ov.astype(BF16)
    hi, mid, lo = _split3(psum)
    imp = _dot(ov, hi) + _dot(ov, mid) + _dot(ov, lo)

    blk = lax.broadcasted_iota(jnp.int32, (n_sel_rows, tq), 0)
    cur = (t0 + lax.broadcasted_iota(jnp.int32, (n_sel_rows, tq), 1)) // SEL_BLOCK
    forced = (blk == 0) | (blk == cur) | (blk == cur - 1)
    key = jnp.where(blk > cur, -1.0, imp)
    key = jnp.where(forced, 1e30, key)
    sel = jnp.zeros((n_sel_rows, tq), F32)
    for _ in range(SEL_TOP):
        mx = jnp.max(key, axis=0, keepdims=True)
        first = jnp.min(jnp.where(key == mx, blk, LANES), axis=0, keepdims=True)
        pick = blk == first
        sel = jnp.where(pick, 1.0, sel)
        key = jnp.where(pick, -2.0, key)
    sel = jnp.where(blk > cur, 0.0, sel)
    sel_ref[:, out_lanes] = jnp.concatenate([sel, jnp.zeros_like(sel)], axis=0).astype(BF16)


def _cmp_attn(feat_t, kc, vc_t, gates_t, batch, seq, tq, ts):
    t = feat_t.shape[1]
    nq = seq // tq
    n_cmp = kc.shape[2]
    assert seq // SEL_BLOCK <= LANES // 2
    return pl.pallas_call(
        functools.partial(_cmp_attn_body, tq=tq, n_cmp=n_cmp),
        grid=(batch, NSA_GROUPS, nq),
        in_specs=[
            pl.BlockSpec((GROUP_WIDTH, tq), lambda b, g, i: (g, b * nq + i)),
            pl.BlockSpec((None, None, n_cmp, HEAD_DIM), lambda b, g, i: (g, b, 0, 0)),
            pl.BlockSpec((None, None, HEAD_DIM, n_cmp), lambda b, g, i: (g, b, 0, 0)),
            pl.BlockSpec((LANES, tq), lambda b, g, i: (0, b * nq + i)),
        ],
        out_specs=[
            pl.BlockSpec((GROUP_WIDTH, tq), lambda b, g, i: (g, b * nq + i)),
            pl.BlockSpec((None, LANES, tq), lambda b, g, i: (g, 0, b * nq + i)),
        ],
        out_shape=[
            jax.ShapeDtypeStruct((D_MODEL, t), F32),
            jax.ShapeDtypeStruct((NSA_GROUPS, LANES, t), BF16),
        ],
        scratch_shapes=[pltpu.VMEM((n_cmp, HEADS_PER_GROUP * ts), F32) for _ in range(tq // ts)],
        compiler_params=_params(("parallel", "parallel", "parallel")),
        name="cmp_attn",
    )(feat_t, kc, vc_t, gates_t)


def _sel_attn_body(q_ref, k_ref, e_ref, v_ref, sel_ref, cos_ref, sin_ref, gates_ref, oin_ref, o_ref,
                   m_ref, l_ref, acc_ref, sa_ref, sb_ref, *, tq, tk):
    g = pl.program_id(1)
    qi = pl.program_id(2)
    t0 = qi * tq
    q = _rope_feature_major(_load_q_t(q_ref), _lane_tile(cos_ref[...], HEADS_PER_GROUP),
                            _lane_tile(sin_ref[...], HEADS_PER_GROUP))
    unselected = _lane_tile(sel_ref[...] - 1.0, HEADS_PER_GROUP)
    q_aug = jnp.concatenate([q, unselected.astype(BF16)], axis=0)

    m_ref[...] = jnp.full(m_ref.shape, MASK_VALUE, F32)
    l_ref[...] = jnp.zeros(l_ref.shape, F32)
    acc_ref[...] = jnp.zeros(acc_ref.shape, F32)

    def scores(ki):
        k0 = pl.multiple_of(ki * tk, tk)
        k_aug = jnp.concatenate([k_ref[pl.ds(k0, tk), :], e_ref[pl.ds(k0, tk), :]], axis=1)
        return _dot(k_aug, q_aug)

    def accumulate(s, ki, causal):
        k0 = pl.multiple_of(ki * tk, tk)
        if causal:
            kpos = k0 + lax.broadcasted_iota(jnp.int32, (tk, tq), 0)
            tpos = t0 + lax.broadcasted_iota(jnp.int32, (tk, tq), 1)
            s = s + _lane_tile(jnp.where(kpos <= tpos, 0.0, MASK_VALUE), HEADS_PER_GROUP)
        m_prev = m_ref[...]
        m_new = jnp.maximum(m_prev, jnp.max(s, axis=0, keepdims=True))
        alpha = jnp.exp2(m_prev - m_new)
        p = jnp.exp2(s - m_new)
        l_ref[...] = alpha * l_ref[...] + jnp.sum(p, axis=0, keepdims=True)
        acc_ref[...] = alpha * acc_ref[...] + _dot(v_ref[:, pl.ds(k0, tk)], p.astype(BF16))
        m_ref[...] = m_new

    n_before = t0 // tk
    sa_ref[...] = scores(0)

    def body(j, carry):
        sb_ref[...] = scores(2 * j + 1)
        accumulate(sa_ref[...], 2 * j, causal=False)
        sa_ref[...] = scores(2 * j + 2)
        accumulate(sb_ref[...], 2 * j + 1, causal=False)
        return carry

    lax.fori_loop(0, n_before // 2, body, 0)

    @pl.when(n_before % 2 == 1)
    def _():
        sb_ref[...] = scores(n_before)
        accumulate(sa_ref[...], n_before - 1, causal=False)
        accumulate(sb_ref[...], n_before, causal=True)

    @pl.when(n_before % 2 == 0)
    def _():
        accumulate(sa_ref[...], n_before, causal=True)

    o = acc_ref[...] * (1.0 / l_ref[...])
    for r in range(HEADS_PER_GROUP):
        rows = slice(r * HEAD_DIM, (r + 1) * HEAD_DIM)
        o_ref[rows, :] = oin_ref[rows, :] + _gate_row(gates_ref, 1, g, r) * o[:, r * tq:(r + 1) * tq]


def _sel_attn(feat_t, tok, selm, cos_t, sin_t, gates_t, oin, batch, seq, tq, tk):
    t = feat_t.shape[1]
    nq = seq // tq
    lanes = HEADS_PER_GROUP * tq
    v_row0 = NSA_HEADS
    assert tk % tq == 0 and seq % tk == 0
    key_block = jnp.arange(seq, dtype=jnp.int32)[:, None] // SEL_BLOCK
    block_of_key = jnp.where(key_block == jnp.arange(LANES, dtype=jnp.int32)[None, :], BLOCK_MASK_BIG, 0.0)
    return pl.pallas_call(
        functools.partial(_sel_attn_body, tq=tq, tk=tk),
        grid=(batch, NSA_GROUPS, nq),
        in_specs=[
            pl.BlockSpec((GROUP_WIDTH, tq), lambda b, g, i: (g, b * nq + i)),
            pl.BlockSpec((None, seq, HEAD_DIM), lambda b, g, i: (g, b, 0)),
            pl.BlockSpec((seq, LANES), lambda b, g, i: (0, 0)),
            pl.BlockSpec((HEAD_DIM, seq), lambda b, g, i: (v_row0 + g, b)),
            pl.BlockSpec((None, LANES, tq), lambda b, g, i: (g, 0, b * nq + i)),
            pl.BlockSpec((ROPE_HALF, tq), lambda b, g, i: (0, i)),
            pl.BlockSpec((ROPE_HALF, tq), lambda b, g, i: (0, i)),
            pl.BlockSpec((LANES, tq), lambda b, g, i: (0, b * nq + i)),
            pl.BlockSpec((GROUP_WIDTH, tq), lambda b, g, i: (g, b * nq + i)),
        ],
        out_specs=pl.BlockSpec((GROUP_WIDTH, tq), lambda b, g, i: (g, b * nq + i)),
        out_shape=jax.ShapeDtypeStruct((D_MODEL, t), F32),
        scratch_shapes=[
            pltpu.VMEM((1, lanes), F32),
            pltpu.VMEM((1, lanes), F32),
            pltpu.VMEM((HEAD_DIM, lanes), F32),
            pltpu.VMEM((tk, lanes), F32),
            pltpu.VMEM((tk, lanes), F32),
        ],
        compiler_params=_params(("parallel", "parallel", "parallel")),
        name="attn_sel",
    )(feat_t, tok, block_of_key.astype(BF16), feat_t, selm, cos_t, sin_t, gates_t, oin)


def _win_attn_body(q_ref, k_ref, v_ref, bias_ref, cos_ref, sin_ref, gates_ref, oin_ref, o_ref, *score_refs, tq):
    g = pl.program_id(1)
    n_sub = len(score_refs)
    ts = tq // n_sub
    band = WINDOW + ts
    starts = []
    for u, s_ref in enumerate(score_refs):
        t0 = pl.program_id(2) * tq + u * ts
        lanes = slice(u * ts, (u + 1) * ts)
        q = _rope_feature_major(_load_q_t(q_ref, u * ts, ts), _lane_tile(cos_ref[:, lanes], HEADS_PER_GROUP),
                                _lane_tile(sin_ref[:, lanes], HEADS_PER_GROUP))
        start = pl.multiple_of(jnp.maximum(t0 - WINDOW, 0), ts)
        s_ref[...] = _dot(k_ref[pl.ds(start, band), :], q)
        starts.append((t0, start, lanes))
    for s_ref, (t0, start, lanes) in zip(score_refs, starts):
        bias = bias_ref[jnp.minimum(t0 // ts, WINDOW // ts)]
        s = s_ref[...] + _lane_tile(bias, HEADS_PER_GROUP)
        m = jnp.max(s, axis=0, keepdims=True)
        p = jnp.exp2(s - m)
        inv = 1.0 / jnp.sum(p, axis=0, keepdims=True)
        o = _dot(v_ref[:, pl.ds(start, band)], p.astype(BF16)) * inv
        for r in range(HEADS_PER_GROUP):
            rows = slice(r * HEAD_DIM, (r + 1) * HEAD_DIM)
            total = oin_ref[rows, lanes] + _gate_row(gates_ref, 2, g, r)[:, lanes] * o[:, r * ts:(r + 1) * ts]
            o_ref[lanes, rows] = total.T.astype(BF16)


def _win_attn(feat_t, tok, cos_t, sin_t, gates_t, oin, batch, seq, tq, ts):
    t = feat_t.shape[1]
    nq = seq // tq
    v_row0 = NSA_HEADS + NSA_GROUPS
    assert seq >= WINDOW + ts and tq % ts == 0 and WINDOW % ts == 0
    band = WINDOW + ts
    rel = jnp.arange(band, dtype=jnp.int32)[None, :, None] - jnp.arange(ts, dtype=jnp.int32)[None, None, :]
    d = (jnp.arange(WINDOW // ts + 1, dtype=jnp.int32) * ts)[:, None, None]
    win_bias = jnp.where((rel <= d) & (rel > d - WINDOW), 0.0, MASK_VALUE).astype(F32)
    return pl.pallas_call(
        functools.partial(_win_attn_body, tq=tq),
        grid=(batch, NSA_GROUPS, nq),
        in_specs=[
            pl.BlockSpec((GROUP_WIDTH, tq), lambda b, g, i: (g, b * nq + i)),
            pl.BlockSpec((None, seq, HEAD_DIM), lambda b, g, i: (NSA_GROUPS + g, b, 0)),
            pl.BlockSpec((HEAD_DIM, seq), lambda b, g, i: (v_row0 + g, b)),
            pl.BlockSpec((WINDOW // ts + 1, band, ts), lambda b, g, i: (0, 0, 0)),
            pl.BlockSpec((ROPE_HALF, tq), lambda b, g, i: (0, i)),
            pl.BlockSpec((ROPE_HALF, tq), lambda b, g, i: (0, i)),
            pl.BlockSpec((LANES, tq), lambda b, g, i: (0, b * nq + i)),
            pl.BlockSpec((GROUP_WIDTH, tq), lambda b, g, i: (g, b * nq + i)),
        ],
        out_specs=pl.BlockSpec((tq, GROUP_WIDTH), lambda b, g, i: (b * nq + i, g)),
        out_shape=jax.ShapeDtypeStruct((t, D_MODEL), BF16),
        scratch_shapes=[pltpu.VMEM((WINDOW + ts, HEADS_PER_GROUP * ts), F32) for _ in range(tq // ts)],
        compiler_params=_params(("parallel", "parallel", "parallel")),
        name="attn_win",
    )(feat_t, tok, feat_t, win_bias, cos_t, sin_t, gates_t, oin)


def _nsa_mixer(h, norm_rows, layer, layer_j, batch, seq, w_in, w_out, phi_pe, phi_w1, phi_b1, phi_w2):
    tm = min(TOKEN_TILE, seq)
    w_in_t = jnp.swapaxes(w_in, 1, 2)
    w_gate_t = jnp.pad(w_in_t[layer_j, QKV_WIDTH:, :], ((0, LANES - N_GATES), (0, 0)))
    cos_tok, sin_tok, cos_t, sin_t = _rope_tables(seq)

    feat_t, gates_t, xn = _nsa_inproj_feat(h, norm_rows, layer, w_in_t, layer_j, w_gate_t, tm=tm)
    x_cmp, tok = _nsa_inproj_tok(xn, w_in_t, layer_j, cos_tok, sin_tok, seq, tm=tm)

    pe = phi_pe.reshape(phi_pe.shape[0], 2, 1, CMP_BLOCK * HEAD_DIM)
    b1 = phi_b1.reshape(phi_b1.shape[0], 2, 1, CMP_HIDDEN)
    kc = _compress(x_cmp, 0, pe, phi_w1, b1, phi_w2, layer_j, batch, feature_major=False)
    vc_t = _compress(x_cmp, 1, pe, phi_w1, b1, phi_w2[layer_j, 1].T, layer_j, batch, feature_major=True)

    tq = min(ATTN_STEP, seq)
    o1, selm = _cmp_attn(feat_t, kc, vc_t, gates_t, batch, seq, tq=tq, ts=ATTN_SUBTILE)
    o2 = _sel_attn(feat_t, tok, selm, cos_t, sin_t, gates_t, o1, batch, seq, tq=SEL_QUERIES, tk=SEL_KEYS)
    o3 = _win_attn(feat_t, tok, cos_t, sin_t, gates_t, o2, batch, seq, tq=tq, ts=ATTN_SUBTILE)
    return _outproj(o3, w_out, layer_j, h, norm_rows, layer, 3, tm=tm, tn=PROJ_COLS)


def _conv_inproj_body(h_ref, halo_ref, g_ref, wb_ref, wc_ref, wu_ref, cw_ref, o_ref, xn_ref, inv_ref, *,
                      tiles_per_seq):
    i = pl.program_id(0)
    j = pl.program_id(1)

    def channel_tile():
        x = xn_ref[...]
        bg = _dot(x, wb_ref[...].astype(BF16))
        z = _dot(x, wc_ref[...].astype(BF16)) * _dot(x, wu_ref[...].astype(BF16))
        cw = cw_ref[...]
        conv = cw[2:3] * z + cw[1:2] * pltpu.roll(z, 1, axis=0) + cw[0:1] * pltpu.roll(z, 2, axis=0)
        o_ref[...] = (bg * conv)[CONV_HALO:].astype(BF16)

    @pl.when(j == 0)
    def _():
        keep = jnp.where(i % tiles_per_seq == 0, 0.0, 1.0)
        xn_ref[0:CONV_HALO] = (_rms(halo_ref[...], g_ref[...]) * keep).astype(BF16)
        xn_ref[CONV_HALO:] = _rms_ref(h_ref, inv_ref, g_ref).astype(BF16)
        channel_tile()

    @pl.when(j > 0)
    def _():
        channel_tile()


def _conv_inproj(h, norm_rows, layer, w_in, conv_w, layer_j, seq, tm, tn):
    t = h.shape[0]
    nj = D_MODEL // tn
    halo_blocks = tm // CONV_HALO
    return pl.pallas_call(
        functools.partial(_conv_inproj_body, tiles_per_seq=seq // tm),
        grid=(t // tm, nj),
        in_specs=[
            pl.BlockSpec((tm, D_MODEL), lambda i, j: (i, 0)),
            pl.BlockSpec((CONV_HALO, D_MODEL), lambda i, j: (jnp.maximum(i * halo_blocks - 1, 0), 0)),
            _norm_row_spec(layer, 2),
            pl.BlockSpec((None, D_MODEL, tn), lambda i, j: (layer_j, 0, j)),
            pl.BlockSpec((None, D_MODEL, tn), lambda i, j: (layer_j, 0, nj + j)),
            pl.BlockSpec((None, D_MODEL, tn), lambda i, j: (layer_j, 0, 2 * nj + j)),
            pl.BlockSpec((None, CONV_WIDTH, tn), lambda i, j: (layer_j, 0, j)),
        ],
        out_specs=pl.BlockSpec((tm, tn), lambda i, j: (i, j)),
        out_shape=jax.ShapeDtypeStruct((t, D_MODEL), BF16),
        scratch_shapes=[pltpu.VMEM((tm + CONV_HALO, D_MODEL), BF16), _inv_scratch(tm)],
        compiler_params=_params(("parallel", "arbitrary")),
        name="conv_inproj",
    )(h, h, norm_rows, w_in, w_in, w_in, conv_w)


def _sgu_inproj_body(h_ref, g_ref, w_ref, lng_ref, lnb_ref, ws_ref, bs_ref, o_ref, xn_ref, z_ref, inv_ref, *,
                     tm, tn):
    j = pl.program_id(1)

    def column_tile(col):
        z_ref[:, pl.ds(col, tn)] = jax.nn.gelu(_dot(xn_ref[...], w_ref[...].astype(BF16)))

    @pl.when(j == 0)
    def _():
        xn_ref[...] = _rms_ref(h_ref, inv_ref, g_ref).astype(BF16)
        column_tile(0)

    @pl.when(j > 0)
    def _():
        column_tile(pl.multiple_of(j * tn, tn))

    @pl.when(j == pl.num_programs(1) - 1)
    def _():
        row = lax.broadcasted_iota(jnp.int32, (SGU_CHUNK, SGU_CHUNK), 0)
        colm = lax.broadcasted_iota(jnp.int32, (SGU_CHUNK, SGU_CHUNK), 1)
        ws = [jnp.where(colm <= row, ws_ref[grp], 0.0).astype(BF16) for grp in range(SGU_GROUPS)]
        bs = bs_ref[...]
        lng, lnb = lng_ref[...], lnb_ref[...]
        for c in range(tm // SGU_CHUNK):
            rws = slice(c * SGU_CHUNK, (c + 1) * SGU_CHUNK)
            v = z_ref[rws, D_MODEL:]
            mu = jnp.mean(v, axis=-1, keepdims=True)
            var = jnp.mean(jnp.square(v - mu), axis=-1, keepdims=True)
            vn = ((v - mu) * lax.rsqrt(var + EPS) * lng + lnb).astype(BF16)
            for grp in range(SGU_GROUPS):
                cols = slice(grp * SGU_GROUP_DIM, (grp + 1) * SGU_GROUP_DIM)
                sv = _dot(ws[grp], vn[:, cols]) + bs[:, grp:grp + 1]
                o_ref[rws, cols] = (z_ref[rws, cols] * sv).astype(BF16)


def _sgu_inproj(h, norm_rows, layer, w_in, ln_g, ln_b, w_s, b_s_t, layer_j, tm, tn):
    t = h.shape[0]
    return pl.pallas_call(
        functools.partial(_sgu_inproj_body, tm=tm, tn=tn),
        grid=(t // tm, 2 * D_MODEL // tn),
        in_specs=[
            pl.BlockSpec((tm, D_MODEL), lambda i, j: (i, 0)),
            _norm_row_spec(layer, 2),
            pl.BlockSpec((None, D_MODEL, tn), lambda i, j: (layer_j, 0, j)),
            pl.BlockSpec((None, 1, D_MODEL), lambda i, j: (layer_j, 0, 0)),
            pl.BlockSpec((None, 1, D_MODEL), lambda i, j: (layer_j, 0, 0)),
            pl.BlockSpec((None, SGU_GROUPS, SGU_CHUNK, SGU_CHUNK), lambda i, j: (layer_j, 0, 0, 0)),
            pl.BlockSpec((None, SGU_CHUNK, SGU_GROUPS), lambda i, j: (layer_j, 0, 0)),
        ],
        out_specs=pl.BlockSpec((tm, D_MODEL), lambda i, j: (i, 0)),
        out_shape=jax.ShapeDtypeStruct((t, D_MODEL), BF16),
        scratch_shapes=[pltpu.VMEM((tm, D_MODEL), BF16), pltpu.VMEM((tm, 2 * D_MODEL), F32), _inv_scratch(tm)],
        compiler_params=_params(("parallel", "arbitrary")),
        name="sgu_inproj",
    )(h, norm_rows, w_in, ln_g, ln_b, w_s, b_s_t)


def kernel(x, p, norm_g, ffn1_wg, ffn1_wu, ffn1_wd, ffn2_wg, ffn2_wu, ffn2_wd, ple_wg, ple_wp, nsa_w_in, nsa_w_out, nsa_phi_pe, nsa_phi_w1, nsa_phi_b1, nsa_phi_w2, conv_w_in, conv_w, conv_w_out, sgu_w_in, sgu_ln_g, sgu_ln_b, sgu_w_s, sgu_b_s, sgu_w_out):
    batch, seq, d = x.shape
    depth = p.shape[0]
    t = batch * seq
    assert d == D_MODEL and seq % SEL_KEYS == 0
    h = x.reshape(t, d)
    p2 = p.reshape(depth, t, PLE_DIM)
    norm_rows = norm_g.reshape(depth * N_NORMS, 1, d)
    sgu_ln_g3 = sgu_ln_g.reshape(-1, 1, d)
    sgu_ln_b3 = sgu_ln_b.reshape(-1, 1, d)
    sgu_b_s_t = jnp.swapaxes(sgu_b_s, 1, 2)
    tm = min(TOKEN_TILE, seq)

    for layer in range(depth):
        layer_j = layer // N_MIXERS
        h = _ffn(h, norm_rows, layer, 0, 1, ffn1_wg, ffn1_wu, ffn1_wd, tm=tm, tf=FFN_SLICE)
        kind = layer % N_MIXERS
        if kind == 0:
            h = _nsa_mixer(h, norm_rows, layer, layer_j, batch, seq, nsa_w_in, nsa_w_out, nsa_phi_pe,
                           nsa_phi_w1, nsa_phi_b1, nsa_phi_w2)
        elif kind == 1:
            a = _conv_inproj(h, norm_rows, layer, conv_w_in, conv_w, layer_j, seq, tm=tm, tn=PROJ_COLS)
            h = _outproj(a, conv_w_out, layer_j, h, norm_rows, layer, 3, tm=tm, tn=PROJ_COLS)
        else:
            a = _sgu_inproj(h, norm_rows, layer, sgu_w_in, sgu_ln_g3, sgu_ln_b3, sgu_w_s, sgu_b_s_t, layer_j,
                            tm=tm, tn=SGU_COLS)
            h = _outproj(a, sgu_w_out, layer_j, h, norm_rows, layer, 3, tm=tm, tn=PROJ_COLS)
        h = _ffn(h, norm_rows, layer, 4, 5, ffn2_wg, ffn2_wu, ffn2_wd, tm=tm, tf=FFN_SLICE)
        h = _ple(h, p2, norm_rows, layer, ple_wg, ple_wp, tm=tm, tn=PROJ_COLS)
    return h.reshape(batch, seq, d)
```

```python
import functools
import math

import jax
import jax.numpy as jnp
from jax import lax
from jax.experimental import pallas as pl
from jax.experimental.pallas import tpu as pltpu

F32 = jnp.float32
BF16 = jnp.bfloat16

EPS = 1e-6
LANES = 128
D_MODEL = 2048
D_FF = 5632
PLE_DIM = 256
N_NORMS = 8
N_MIXERS = 3
NSA_HEADS = 16
NSA_GROUPS = 4
HEADS_PER_GROUP = NSA_HEADS // NSA_GROUPS
HEAD_DIM = D_MODEL // NSA_HEADS
GROUP_WIDTH = HEADS_PER_GROUP * HEAD_DIM
KV_WIDTH = NSA_GROUPS * HEAD_DIM
ROPE_DIM = HEAD_DIM // 4
ROPE_HALF = ROPE_DIM // 2
ROPE_THETA = 500000.0
CMP_BLOCK = 32
CMP_STRIDE = 16
CMP_HIDDEN = 256
SEL_BLOCK = 64
SEL_TOP = 16
WINDOW = 512
QKV_WIDTH = NSA_HEADS * HEAD_DIM + 6 * KV_WIDTH
N_GATES = 3 * NSA_HEADS
SGU_CHUNK = 128
SGU_GROUPS = 8
SGU_GROUP_DIM = D_MODEL // SGU_GROUPS
CONV_WIDTH = 3
CONV_HALO = 16

MASK_VALUE = -1e30
BLOCK_MASK_BIG = 2.0 ** 100
SCORE_SCALE = HEAD_DIM ** -0.5
EXP2_SCALE = SCORE_SCALE * math.log2(math.e)
VMEM_LIMIT = 60 * 1024 * 1024

TOKEN_TILE = 1024
FFN_SLICE = 256
PROJ_COLS = 512
SGU_COLS = 256
ATTN_STEP = 2048
ATTN_SUBTILE = 256
SEL_QUERIES = 512
SEL_KEYS = 512


def _params(semantics):
    return pltpu.CompilerParams(dimension_semantics=semantics, vmem_limit_bytes=VMEM_LIMIT)


def _rms(x, g):
    return x * lax.rsqrt(jnp.mean(x * x, axis=-1, keepdims=True) + EPS) * g


def _rms_ref(src_ref, inv_ref, g_ref):
    x = src_ref[...]
    inv_ref[...] = lax.rsqrt(jnp.mean(x * x, axis=-1, keepdims=True) + EPS)
    return src_ref[...] * inv_ref[...] * g_ref[...]


def _inv_scratch(rows):
    return pltpu.VMEM((rows, 1), F32)


def _dot(a, b):
    return jnp.dot(a, b, preferred_element_type=F32)


def _dot_nt(a, b):
    return lax.dot_general(a, b, (((1,), (1,)), ((), ())), preferred_element_type=F32)


def _norm_row_spec(layer, k):
    idx = layer * N_NORMS + k
    return pl.BlockSpec((None, 1, D_MODEL), lambda i, j: (idx, 0, 0))


def _ffn_body(h_ref, gpre_ref, gpost_ref, wg_ref, wu_ref, wd_ref, o_ref, xn_ref, inv_ref):
    f = pl.program_id(1)

    def down_proj_slice():
        x = xn_ref[...]
        gate = _dot(x, wg_ref[...].astype(BF16))
        up = _dot(x, wu_ref[...].astype(BF16))
        act = (gate * jax.nn.sigmoid(gate) * up).astype(BF16)
        return _dot(act, wd_ref[...].astype(BF16))

    @pl.when(f == 0)
    def _():
        xn_ref[...] = _rms_ref(h_ref, inv_ref, gpre_ref).astype(BF16)
        o_ref[...] = down_proj_slice()

    @pl.when(f > 0)
    def _():
        o_ref[...] += down_proj_slice()

    @pl.when(f == pl.num_programs(1) - 1)
    def _():
        o_ref[...] = h_ref[...] + 0.5 * _rms_ref(o_ref, inv_ref, gpost_ref)


def _ffn(h, norm_rows, layer, k_pre, k_post, wg, wu, wd, tm, tf):
    t = h.shape[0]
    return pl.pallas_call(
        _ffn_body,
        grid=(t // tm, D_FF // tf),
        in_specs=[
            pl.BlockSpec((tm, D_MODEL), lambda i, f: (i, 0)),
            _norm_row_spec(layer, k_pre),
            _norm_row_spec(layer, k_post),
            pl.BlockSpec((None, D_MODEL, tf), lambda i, f: (layer, 0, f)),
            pl.BlockSpec((None, D_MODEL, tf), lambda i, f: (layer, 0, f)),
            pl.BlockSpec((None, tf, D_MODEL), lambda i, f: (layer, f, 0)),
        ],
        out_specs=pl.BlockSpec((tm, D_MODEL), lambda i, f: (i, 0)),
        out_shape=jax.ShapeDtypeStruct((t, D_MODEL), F32),
        scratch_shapes=[pltpu.VMEM((tm, D_MODEL), BF16), _inv_scratch(tm)],
        compiler_params=_params(("parallel", "arbitrary")),
        name="ffn",
    )(h, norm_rows, norm_rows, wg, wu, wd)


def _resident_cols(n_col_tiles):
    return lambda i, j: jnp.where(i == 0, j, n_col_tiles - 1)


def _outproj_body(a_ref, w_ref, h_ref, g_ref, o_ref, inv_ref, wb_ref, *, tn):
    j = pl.program_id(1)
    col = pl.multiple_of(j * tn, tn)

    @pl.when(pl.program_id(0) == 0)
    def _():
        wb_ref[:, pl.ds(col, tn)] = w_ref[...].astype(BF16)

    o_ref[:, pl.ds(col, tn)] = _dot(a_ref[...], wb_ref[:, pl.ds(col, tn)])

    @pl.when(j == pl.num_programs(1) - 1)
    def _():
        o_ref[...] = h_ref[...] + _rms_ref(o_ref, inv_ref, g_ref)


def _outproj(a, w, w_layer, h, norm_rows, layer, k_norm, tm, tn):
    t, kdim = a.shape
    w_col = _resident_cols(D_MODEL // tn)
    return pl.pallas_call(
        functools.partial(_outproj_body, tn=tn),
        grid=(t // tm, D_MODEL // tn),
        in_specs=[
            pl.BlockSpec((tm, kdim), lambda i, j: (i, 0)),
            pl.BlockSpec((None, kdim, tn), lambda i, j: (w_layer, 0, w_col(i, j))),
            pl.BlockSpec((tm, D_MODEL), lambda i, j: (i, 0)),
            _norm_row_spec(layer, k_norm),
        ],
        out_specs=pl.BlockSpec((tm, D_MODEL), lambda i, j: (i, 0)),
        out_shape=jax.ShapeDtypeStruct((t, D_MODEL), F32),
        scratch_shapes=[_inv_scratch(tm), pltpu.VMEM((kdim, D_MODEL), BF16)],
        compiler_params=_params(("arbitrary", "arbitrary")),
        name="outproj",
    )(a, w, h, norm_rows)


def _ple_body(h_ref, p_ref, gpre_ref, gpost_ref, wg_ref, wp_ref, o_ref, xn_ref, pb_ref, inv_ref, wgb_ref, wpb_ref,
              *, tn):
    j = pl.program_id(1)

    def column_tile(col):
        cols = pl.ds(col, tn)

        @pl.when(pl.program_id(0) == 0)
        def _():
            wgb_ref[:, cols] = wg_ref[...].astype(BF16)
            wpb_ref[:, cols] = wp_ref[...].astype(BF16)

        gate = jax.nn.sigmoid(_dot(xn_ref[...], wgb_ref[:, cols]))
        emb = _dot(pb_ref[...], wpb_ref[:, cols])
        o_ref[:, cols] = gate * emb

    @pl.when(j == 0)
    def _():
        xn_ref[...] = _rms_ref(h_ref, inv_ref, gpre_ref).astype(BF16)
        pb_ref[...] = p_ref[...].astype(BF16)
        column_tile(0)

    @pl.when(j > 0)
    def _():
        column_tile(pl.multiple_of(j * tn, tn))

    @pl.when(j == pl.num_programs(1) - 1)
    def _():
        o_ref[...] = h_ref[...] + _rms_ref(o_ref, inv_ref, gpost_ref)


def _ple(h, p, norm_rows, layer, wg, wp, tm, tn):
    t = h.shape[0]
    w_col = _resident_cols(D_MODEL // tn)
    return pl.pallas_call(
        functools.partial(_ple_body, tn=tn),
        grid=(t // tm, D_MODEL // tn),
        in_specs=[
            pl.BlockSpec((tm, D_MODEL), lambda i, j: (i, 0)),
            pl.BlockSpec((None, tm, PLE_DIM), lambda i, j: (layer, i, 0)),
            _norm_row_spec(layer, 6),
            _norm_row_spec(layer, 7),
            pl.BlockSpec((None, D_MODEL, tn), lambda i, j: (layer, 0, w_col(i, j))),
            pl.BlockSpec((None, PLE_DIM, tn), lambda i, j: (layer, 0, w_col(i, j))),
        ],
        out_specs=pl.BlockSpec((tm, D_MODEL), lambda i, j: (i, 0)),
        out_shape=jax.ShapeDtypeStruct((t, D_MODEL), F32),
        scratch_shapes=[pltpu.VMEM((tm, D_MODEL), BF16), pltpu.VMEM((tm, PLE_DIM), BF16), _inv_scratch(tm),
                        pltpu.VMEM((D_MODEL, D_MODEL), BF16), pltpu.VMEM((PLE_DIM, D_MODEL), BF16)],
        compiler_params=_params(("arbitrary", "arbitrary")),
        name="ple",
    )(h, p, norm_rows, norm_rows, wg, wp)


def _nsa_inproj_feat_body(h_ref, g_ref, w_ref, wg_ref, feat_ref, gates_ref, xn_ref, inv_ref, wb_ref, *,
                          n_q_tiles):
    j = pl.program_id(1)
    tn = w_ref.shape[0]

    def feature_tile():
        rows = pl.ds(pl.multiple_of(j * tn, tn), tn)

        @pl.when(pl.program_id(0) == 0)
        def _():
            wb_ref[rows, :] = w_ref[...].astype(BF16)

        row_scale = jnp.where(j < n_q_tiles, EXP2_SCALE, 1.0)
        feat_ref[...] = (_dot_nt(wb_ref[rows, :], xn_ref[...]) * row_scale).astype(BF16)

    @pl.when(j == 0)
    def _():
        xn = _rms_ref(h_ref, inv_ref, g_ref).astype(BF16)
        xn_ref[...] = xn
        gates_ref[...] = jax.nn.sigmoid(_dot_nt(wg_ref[...].astype(BF16), xn))
        feature_tile()

    @pl.when(j > 0)
    def _():
        feature_tile()


def _nsa_inproj_feat(h, norm_rows, layer, w_in_t, w_layer, w_gate_t, tm):
    t = h.shape[0]
    tn = KV_WIDTH
    n_q_tiles = NSA_HEADS * HEAD_DIM // tn
    n_feat = NSA_HEADS * HEAD_DIM + 2 * KV_WIDTH
    w_tile = _resident_cols(n_feat // tn)

    def w_block(i, j):
        jj = w_tile(i, j)
        return jj + 3 * (jj // n_q_tiles) + jj // (n_q_tiles + 1)

    return pl.pallas_call(
        functools.partial(_nsa_inproj_feat_body, n_q_tiles=n_q_tiles),
        grid=(t // tm, n_feat // tn),
        in_specs=[
            pl.BlockSpec((tm, D_MODEL), lambda i, j: (i, 0)),
            _norm_row_spec(layer, 2),
            pl.BlockSpec((None, tn, D_MODEL), lambda i, j: (w_layer, w_block(i, j), 0)),
            pl.BlockSpec((LANES, D_MODEL), lambda i, j: (0, 0)),
        ],
        out_specs=[
            pl.BlockSpec((tn, tm), lambda i, j: (j, i)),
            pl.BlockSpec((LANES, tm), lambda i, j: (0, i)),
            pl.BlockSpec((tm, D_MODEL), lambda i, j: (i, 0)),
        ],
        out_shape=[
            jax.ShapeDtypeStruct((n_feat, t), BF16),
            jax.ShapeDtypeStruct((LANES, t), F32),
            jax.ShapeDtypeStruct((t, D_MODEL), BF16),
        ],
        scratch_shapes=[_inv_scratch(tm), pltpu.VMEM((n_feat, D_MODEL), BF16)],
        compiler_params=_params(("arbitrary", "arbitrary")),
        name="nsa_inproj_feat",
    )(h, norm_rows, w_in_t, w_gate_t)


def _rope_token_major(x, cos, sin):
    lane = lax.broadcasted_iota(jnp.int32, x.shape, 1)
    partner = jnp.where(lane < ROPE_HALF, pltpu.roll(x, LANES - ROPE_HALF, axis=1),
                        pltpu.roll(x, ROPE_HALF, axis=1))
    return x * cos + partner * sin


def _nsa_inproj_tok_body(xn_ref, wc_ref, wk_ref, cos_ref, sin_ref, oc_ref, ok_ref, wcb_ref, wkb_ref):
    rows = pl.ds(pl.multiple_of(pl.program_id(1) * KV_WIDTH, KV_WIDTH), KV_WIDTH)

    @pl.when(pl.program_id(0) == 0)
    def _():
        wcb_ref[rows, :] = wc_ref[...].astype(BF16)
        wkb_ref[rows, :] = wk_ref[...].astype(BF16)

    xn = xn_ref[...]
    res_c = _dot_nt(xn, wcb_ref[rows, :])
    res_k = _dot_nt(xn, wkb_ref[rows, :])
    cos, sin = cos_ref[...], sin_ref[...]
    for c in range(NSA_GROUPS):
        cols = slice(c * LANES, (c + 1) * LANES)
        oc_ref[c] = res_c[:, cols]
        ok_ref[c] = _rope_token_major(res_k[:, cols], cos, sin).astype(BF16)


def _nsa_inproj_tok(xn, w_in_t, w_layer, cos_tok, sin_tok, seq, tm):
    t = xn.shape[0]
    per_seq = seq // tm
    q_blocks = NSA_HEADS * HEAD_DIM // KV_WIDTH
    slab_spec = pl.BlockSpec((NSA_GROUPS, tm, LANES), lambda i, j: (j, i, 0))
    w_step = _resident_cols(2)
    return pl.pallas_call(
        _nsa_inproj_tok_body,
        grid=(t // tm, 2),
        in_specs=[
            pl.BlockSpec((tm, D_MODEL), lambda i, j: (i, 0)),
            pl.BlockSpec((None, KV_WIDTH, D_MODEL), lambda i, j: (w_layer, q_blocks + w_step(i, j), 0)),
            pl.BlockSpec((None, KV_WIDTH, D_MODEL), lambda i, j: (w_layer, q_blocks + 2 + 2 * w_step(i, j), 0)),
            pl.BlockSpec((tm, LANES), lambda i, j: (i % per_seq, 0)),
            pl.BlockSpec((tm, LANES), lambda i, j: (i % per_seq, 0)),
        ],
        out_specs=[slab_spec, slab_spec],
        out_shape=[
            jax.ShapeDtypeStruct((2 * NSA_GROUPS, t, LANES), F32),
            jax.ShapeDtypeStruct((2 * NSA_GROUPS, t, LANES), BF16),
        ],
        scratch_shapes=[pltpu.VMEM((2 * KV_WIDTH, D_MODEL), BF16), pltpu.VMEM((2 * KV_WIDTH, D_MODEL), BF16)],
        compiler_params=_params(("arbitrary", "arbitrary")),
        name="nsa_inproj_tok",
    )(xn, w_in_t, w_in_t, cos_tok, sin_tok)


def _rope_tables(seq):
    inv_freq = jnp.power(jnp.float32(ROPE_THETA), -jnp.arange(0, ROPE_DIM, 2, dtype=F32) / ROPE_DIM)
    ang = jnp.arange(seq, dtype=F32)[:, None] * inv_freq[None, :]
    cos, sin = jnp.cos(ang), jnp.sin(ang)
    cos_tok = jnp.concatenate([cos, cos, jnp.ones((seq, LANES - ROPE_DIM), F32)], axis=1)
    sin_tok = jnp.concatenate([-sin, sin, jnp.zeros((seq, LANES - ROPE_DIM), F32)], axis=1)
    return cos_tok, sin_tok, cos.T, sin.T


def _compress_body(x_ref, pe_ref, w1_ref, b1_ref, w2_ref, o_ref, *, feature_major):
    n_chunks = x_ref.shape[0] // CMP_STRIDE
    w1 = w1_ref[...].astype(BF16)
    both = jnp.zeros((n_chunks, 2 * CMP_HIDDEN), F32)
    for l in range(CMP_STRIDE):
        x_l = x_ref[pl.ds(l, n_chunks, stride=CMP_STRIDE), :].astype(BF16)
        lo, hi = l * HEAD_DIM, (CMP_STRIDE + l) * HEAD_DIM
        w_l = jnp.concatenate([w1[lo:lo + HEAD_DIM], w1[hi:hi + HEAD_DIM]], axis=1)
        both = both + _dot(x_l, w_l)
    first = both[:, :CMP_HIDDEN]
    second = pltpu.roll(both[:, CMP_HIDDEN:], n_chunks - 1, axis=0)
    pe = jnp.broadcast_to(pe_ref[...], (8, CMP_BLOCK * HEAD_DIM)).astype(BF16)
    const = _dot(pe, w1)[0:1]
    hid = first + second + const + b1_ref[...]
    act = (hid * jax.nn.sigmoid(hid)).astype(BF16)
    if feature_major:
        o_ref[...] = _dot_nt(w2_ref[...].astype(BF16), act).astype(BF16)
    else:
        o_ref[...] = _dot(act, w2_ref[...].astype(BF16)).astype(BF16)


def _compress(x, which, pe, w1, b1, w2, layer_j, batch, feature_major):
    seq = x.shape[1] // batch
    n_chunks = seq // CMP_STRIDE
    if feature_major:
        w2_spec = pl.BlockSpec((HEAD_DIM, CMP_HIDDEN), lambda g, b: (0, 0))
        out_block, out_dims = (None, None, HEAD_DIM, n_chunks), (NSA_GROUPS, batch, HEAD_DIM, n_chunks)
    else:
        w2_spec = pl.BlockSpec((None, None, CMP_HIDDEN, HEAD_DIM), lambda g, b: (layer_j, which, 0, 0))
        out_block, out_dims = (None, None, n_chunks, HEAD_DIM), (NSA_GROUPS, batch, n_chunks, HEAD_DIM)
    return pl.pallas_call(
        functools.partial(_compress_body, feature_major=feature_major),
        grid=(NSA_GROUPS, batch),
        in_specs=[
            pl.BlockSpec((None, seq, HEAD_DIM), lambda g, b: (which * NSA_GROUPS + g, b, 0)),
            pl.BlockSpec((None, None, 1, CMP_BLOCK * HEAD_DIM), lambda g, b: (layer_j, which, 0, 0)),
            pl.BlockSpec((None, None, CMP_BLOCK * HEAD_DIM, CMP_HIDDEN), lambda g, b: (layer_j, which, 0, 0)),
            pl.BlockSpec((None, None, 1, CMP_HIDDEN), lambda g, b: (layer_j, which, 0, 0)),
            w2_spec,
        ],
        out_specs=pl.BlockSpec(out_block, lambda g, b: (g, b, 0, 0)),
        out_shape=jax.ShapeDtypeStruct(out_dims, BF16),
        compiler_params=_params(("parallel", "parallel")),
        name="compress",
    )(x, pe, w1, b1, w2)


def _lane_tile(x, n):
    return jnp.concatenate([x] * n, axis=1)


def _load_q_t(q_ref, lo=0, n=None):
    n = q_ref.shape[1] if n is None else n
    return jnp.concatenate([q_ref[r * HEAD_DIM:(r + 1) * HEAD_DIM, lo:lo + n] for r in range(HEADS_PER_GROUP)],
                           axis=1)


def _rope_feature_major(q, cos, sin):
    x1 = q[0:ROPE_HALF].astype(F32)
    x2 = q[ROPE_HALF:ROPE_DIM].astype(F32)
    r1 = (x1 * cos - x2 * sin).astype(BF16)
    r2 = (x2 * cos + x1 * sin).astype(BF16)
    return jnp.concatenate([r1, r2, q[ROPE_DIM:]], axis=0)


def _split3(x):
    hi = x.astype(BF16)
    r1 = x - hi.astype(F32)
    mid = r1.astype(BF16)
    lo = (r1 - mid.astype(F32)).astype(BF16)
    return hi, mid, lo


def _gate_row(gates_ref, branch, g, r):
    return gates_ref[pl.ds(branch * NSA_HEADS + g * HEADS_PER_GROUP + r, 1), :]


def _cmp_attn_body(q_ref, kc_ref, vc_ref, gates_ref, o_ref, sel_ref, *score_refs, tq, n_cmp):
    n_sub = len(score_refs)
    ts = tq // n_sub
    kc = kc_ref[...]
    for u, s_ref in enumerate(score_refs):
        s_ref[...] = _dot(kc, _load_q_t(q_ref, u * ts, ts))
    for u, s_ref in enumerate(score_refs):
        _cmp_attn_subtile(s_ref[...], vc_ref, gates_ref, o_ref, sel_ref, pl.program_id(2) * tq + u * ts, u * ts,
                          ts, n_cmp)


def _cmp_attn_subtile(s, vc_ref, gates_ref, o_ref, sel_ref, t0, lane0, tq, n_cmp):
    g = pl.program_id(1)
    out_lanes = slice(lane0, lane0 + tq)

    tpos = t0 + lax.broadcasted_iota(jnp.int32, (n_cmp, tq), 1)
    cend = lax.broadcasted_iota(jnp.int32, (n_cmp, tq), 0) * CMP_STRIDE + (CMP_BLOCK - 1)
    visible = cend <= tpos
    bias = _lane_tile(jnp.where(visible, 0.0, MASK_VALUE), HEADS_PER_GROUP)
    okf = _lane_tile(jnp.where(visible, 1.0, 0.0), HEADS_PER_GROUP)
    s = s + bias
    m = jnp.max(s, axis=0, keepdims=True)
    e = jnp.exp2(s - m) * okf
    inv = 1.0 / jnp.maximum(jnp.sum(e, axis=0, keepdims=True), 1e-30)
    p = e * inv
    o = _dot(vc_ref[...], p.astype(BF16))

    for r in range(HEADS_PER_GROUP):
        gate = _gate_row(gates_ref, 0, g, r)[:, out_lanes]
        o_ref[r * HEAD_DIM:(r + 1) * HEAD_DIM, out_lanes] = gate * o[:, r * tq:(r + 1) * tq]

    psum = p[:, 0:tq]
    for r in range(1, HEADS_PER_GROUP):
        psum = psum + p[:, r * tq:(r + 1) * tq]
    n_sel_rows = LANES // 2
    jrow = lax.broadcasted_iota(jnp.int32, (n_sel_rows, n_cmp), 0)
    ccol = lax.broadcasted_iota(jnp.int32, (n_sel_rows, n_cmp), 1)
    c_lo, c_hi = ccol * CMP_STRIDE, ccol * CMP_STRIDE + (CMP_BLOCK - 1)
    j_lo, j_hi = jrow * SEL_BLOCK, jrow * SEL_BLOCK + (SEL_BLOCK - 1)
    ov = jnp.maximum(jnp.minimum(c_hi, j_hi) - jnp.maximum(c_lo, j_lo) + 1, 0).astype(F32) / CMP_STRIDE
    ov = ov.astype(BF16)
    hi, mid, lo = _split3(psum)
    imp = _dot(ov, hi) + _dot(ov, mid) + _dot(ov, lo)

    blk = lax.broadcasted_iota(jnp.int32, (n_sel_rows, tq), 0)
    cur = (t0 + lax.broadcasted_iota(jnp.int32, (n_sel_rows, tq), 1)) // SEL_BLOCK
    forced = (blk == 0) | (blk == cur) | (blk == cur - 1)
    key = jnp.where(forced | (blk > cur), -1.0, imp)
    sel = jnp.where(forced, 1.0, 0.0)
    for _ in range(SEL_TOP - 3):
        mx = jnp.max(key, axis=0, keepdims=True)
        first = jnp.min(jnp.where(key == mx, blk, LANES), axis=0, keepdims=True)
        pick = blk == first
        sel = jnp.where(pick, 1.0, sel)
        key = jnp.where(pick, -2.0, key)
    sel = jnp.where(blk > cur, 0.0, sel)
    sel_ref[:, out_lanes] = jnp.concatenate([sel, jnp.zeros_like(sel)], axis=0).astype(BF16)


def _cmp_attn(feat_t, kc, vc_t, gates_t, batch, seq, tq, ts):
    t = feat_t.shape[1]
    nq = seq // tq
    n_cmp = kc.shape[2]
    assert seq // SEL_BLOCK <= LANES // 2
    return pl.pallas_call(
        functools.partial(_cmp_attn_body, tq=tq, n_cmp=n_cmp),
        grid=(batch, NSA_GROUPS, nq),
        in_specs=[
            pl.BlockSpec((GROUP_WIDTH, tq), lambda b, g, i: (g, b * nq + i)),
            pl.BlockSpec((None, None, n_cmp, HEAD_DIM), lambda b, g, i: (g, b, 0, 0)),
            pl.BlockSpec((None, None, HEAD_DIM, n_cmp), lambda b, g, i: (g, b, 0, 0)),
            pl.BlockSpec((LANES, tq), lambda b, g, i: (0, b * nq + i)),
        ],
        out_specs=[
            pl.BlockSpec((GROUP_WIDTH, tq), lambda b, g, i: (g, b * nq + i)),
            pl.BlockSpec((None, LANES, tq), lambda b, g, i: (g, 0, b * nq + i)),
        ],
        out_shape=[
            jax.ShapeDtypeStruct((D_MODEL, t), F32),
            jax.ShapeDtypeStruct((NSA_GROUPS, LANES, t), BF16),
        ],
        scratch_shapes=[pltpu.VMEM((n_cmp, HEADS_PER_GROUP * ts), F32) for _ in range(tq // ts)],
        compiler_params=_params(("parallel", "parallel", "parallel")),
        name="cmp_attn",
    )(feat_t, kc, vc_t, gates_t)


def _sel_attn_body(q_ref, k_ref, e_ref, v_ref, sel_ref, cos_ref, sin_ref, gates_ref, oin_ref, o_ref,
                   m_ref, l_ref, acc_ref, sa_ref, sb_ref, *, tq, tk):
    g = pl.program_id(1)
    qi = pl.program_id(2)
    t0 = qi * tq
    q = _rope_feature_major(_load_q_t(q_ref), _lane_tile(cos_ref[...], HEADS_PER_GROUP),
                            _lane_tile(sin_ref[...], HEADS_PER_GROUP))
    unselected = _lane_tile(sel_ref[...] - 1.0, HEADS_PER_GROUP)
    q_aug = jnp.concatenate([q, unselected.astype(BF16)], axis=0)

    m_ref[...] = jnp.full(m_ref.shape, MASK_VALUE, F32)
    l_ref[...] = jnp.zeros(l_ref.shape, F32)
    acc_ref[...] = jnp.zeros(acc_ref.shape, F32)

    def scores(ki):
        k0 = pl.multiple_of(ki * tk, tk)
        k_aug = jnp.concatenate([k_ref[pl.ds(k0, tk), :], e_ref[pl.ds(k0, tk), :]], axis=1)
        return _dot(k_aug, q_aug)

    def accumulate(s, ki, causal):
        k0 = pl.multiple_of(ki * tk, tk)
        if causal:
            kpos = k0 + lax.broadcasted_iota(jnp.int32, (tk, tq), 0)
            tpos = t0 + lax.broadcasted_iota(jnp.int32, (tk, tq), 1)
            s = s + _lane_tile(jnp.where(kpos <= tpos, 0.0, MASK_VALUE), HEADS_PER_GROUP)
        m_prev = m_ref[...]
        m_new = jnp.maximum(m_prev, jnp.max(s, axis=0, keepdims=True))
        alpha = jnp.exp2(m_prev - m_new)
        p = jnp.exp2(s - m_new)
        l_ref[...] = alpha * l_ref[...] + jnp.sum(p, axis=0, keepdims=True)
        acc_ref[...] = alpha * acc_ref[...] + _dot(v_ref[:, pl.ds(k0, tk)], p.astype(BF16))
        m_ref[...] = m_new

    n_before = t0 // tk
    sa_ref[...] = scores(0)

    def body(j, carry):
        sb_ref[...] = scores(2 * j + 1)
        accumulate(sa_ref[...], 2 * j, causal=False)
        sa_ref[...] = scores(2 * j + 2)
        accumulate(sb_ref[...], 2 * j + 1, causal=False)
        return carry

    lax.fori_loop(0, n_before // 2, body, 0)

    @pl.when(n_before % 2 == 1)
    def _():
        sb_ref[...] = scores(n_before)
        accumulate(sa_ref[...], n_before - 1, causal=False)
        accumulate(sb_ref[...], n_before, causal=True)

    @pl.when(n_before % 2 == 0)
    def _():
        accumulate(sa_ref[...], n_before, causal=True)

    o = acc_ref[...] * (1.0 / l_ref[...])
    for r in range(HEADS_PER_GROUP):
        rows = slice(r * HEAD_DIM, (r + 1) * HEAD_DIM)
        o_ref[rows, :] = oin_ref[rows, :] + _gate_row(gates_ref, 1, g, r) * o[:, r * tq:(r + 1) * tq]


def _sel_attn(feat_t, tok, selm, cos_t, sin_t, gates_t, oin, batch, seq, tq, tk):
    t = feat_t.shape[1]
    nq = seq // tq
    lanes = HEADS_PER_GROUP * tq
    v_row0 = NSA_HEADS
    assert tk % tq == 0 and seq % tk == 0
    key_block = jnp.arange(seq, dtype=jnp.int32)[:, None] // SEL_BLOCK
    block_of_key = jnp.where(key_block == jnp.arange(LANES, dtype=jnp.int32)[None, :], BLOCK_MASK_BIG, 0.0)
    return pl.pallas_call(
        functools.partial(_sel_attn_body, tq=tq, tk=tk),
        grid=(batch, NSA_GROUPS, nq),
        in_specs=[
            pl.BlockSpec((GROUP_WIDTH, tq), lambda b, g, i: (g, b * nq + i)),
            pl.BlockSpec((None, seq, HEAD_DIM), lambda b, g, i: (g, b, 0)),
            pl.BlockSpec((seq, LANES), lambda b, g, i: (0, 0)),
            pl.BlockSpec((HEAD_DIM, seq), lambda b, g, i: (v_row0 + g, b)),
            pl.BlockSpec((None, LANES, tq), lambda b, g, i: (g, 0, b * nq + i)),
            pl.BlockSpec((ROPE_HALF, tq), lambda b, g, i: (0, i)),
            pl.BlockSpec((ROPE_HALF, tq), lambda b, g, i: (0, i)),
            pl.BlockSpec((LANES, tq), lambda b, g, i: (0, b * nq + i)),
            pl.BlockSpec((GROUP_WIDTH, tq), lambda b, g, i: (g, b * nq + i)),
        ],
        out_specs=pl.BlockSpec((GROUP_WIDTH, tq), lambda b, g, i: (g, b * nq + i)),
        out_shape=jax.ShapeDtypeStruct((D_MODEL, t), F32),
        scratch_shapes=[
            pltpu.VMEM((1, lanes), F32),
            pltpu.VMEM((1, lanes), F32),
            pltpu.VMEM((HEAD_DIM, lanes), F32),
            pltpu.VMEM((tk, lanes), F32),
            pltpu.VMEM((tk, lanes), F32),
        ],
        compiler_params=_params(("parallel", "parallel", "parallel")),
        name="attn_sel",
    )(feat_t, tok, block_of_key.astype(BF16), feat_t, selm, cos_t, sin_t, gates_t, oin)


def _win_attn_body(q_ref, k_ref, v_ref, bias_ref, cos_ref, sin_ref, gates_ref, oin_ref, o_ref, *score_refs, tq):
    g = pl.program_id(1)
    n_sub = len(score_refs)
    ts = tq // n_sub
    band = WINDOW + ts
    starts = []
    for u, s_ref in enumerate(score_refs):
        t0 = pl.program_id(2) * tq + u * ts
        lanes = slice(u * ts, (u + 1) * ts)
        q = _rope_feature_major(_load_q_t(q_ref, u * ts, ts), _lane_tile(cos_ref[:, lanes], HEADS_PER_GROUP),
                                _lane_tile(sin_ref[:, lanes], HEADS_PER_GROUP))
        start = pl.multiple_of(jnp.maximum(t0 - WINDOW, 0), ts)
        s_ref[...] = _dot(k_ref[pl.ds(start, band), :], q)
        starts.append((t0, start, lanes))
    for s_ref, (t0, start, lanes) in zip(score_refs, starts):
        bias = bias_ref[jnp.minimum(t0 // ts, WINDOW // ts)]
        s = s_ref[...] + _lane_tile(bias, HEADS_PER_GROUP)
        m = jnp.max(s, axis=0, keepdims=True)
        p = jnp.exp2(s - m)
        inv = 1.0 / jnp.sum(p, axis=0, keepdims=True)
        o = _dot(v_ref[:, pl.ds(start, band)], p.astype(BF16)) * inv
        for r in range(HEADS_PER_GROUP):
            rows = slice(r * HEAD_DIM, (r + 1) * HEAD_DIM)
            total = oin_ref[rows, lanes] + _gate_row(gates_ref, 2, g, r)[:, lanes] * o[:, r * ts:(r + 1) * ts]
            o_ref[lanes, rows] = total.T.astype(BF16)


def _win_attn(feat_t, tok, cos_t, sin_t, gates_t, oin, batch, seq, tq, ts):
    t = feat_t.shape[1]
    nq = seq // tq
    v_row0 = NSA_HEADS + NSA_GROUPS
    assert seq >= WINDOW + ts and tq % ts == 0 and WINDOW % ts == 0
    band = WINDOW + ts
    rel = jnp.arange(band, dtype=jnp.int32)[None, :, None] - jnp.arange(ts, dtype=jnp.int32)[None, None, :]
    d = (jnp.arange(WINDOW // ts + 1, dtype=jnp.int32) * ts)[:, None, None]
    win_bias = jnp.where((rel <= d) & (rel > d - WINDOW), 0.0, MASK_VALUE).astype(F32)
    return pl.pallas_call(
        functools.partial(_win_attn_body, tq=tq),
        grid=(batch, NSA_GROUPS, nq),
        in_specs=[
            pl.BlockSpec((GROUP_WIDTH, tq), lambda b, g, i: (g, b * nq + i)),
            pl.BlockSpec((None, seq, HEAD_DIM), lambda b, g, i: (NSA_GROUPS + g, b, 0)),
            pl.BlockSpec((HEAD_DIM, seq), lambda b, g, i: (v_row0 + g, b)),
            pl.BlockSpec((WINDOW // ts + 1, band, ts), lambda b, g, i: (0, 0, 0)),
            pl.BlockSpec((ROPE_HALF, tq), lambda b, g, i: (0, i)),
            pl.BlockSpec((ROPE_HALF, tq), lambda b, g, i: (0, i)),
            pl.BlockSpec((LANES, tq), lambda b, g, i: (0, b * nq + i)),
            pl.BlockSpec((GROUP_WIDTH, tq), lambda b, g, i: (g, b * nq + i)),
        ],
        out_specs=pl.BlockSpec((tq, GROUP_WIDTH), lambda b, g, i: (b * nq + i, g)),
        out_shape=jax.ShapeDtypeStruct((t, D_MODEL), BF16),
        scratch_shapes=[pltpu.VMEM((WINDOW + ts, HEADS_PER_GROUP * ts), F32) for _ in range(tq // ts)],
        compiler_params=_params(("parallel", "parallel", "parallel")),
        name="attn_win",
    )(feat_t, tok, feat_t, win_bias, cos_t, sin_t, gates_t, oin)


def _nsa_mixer(h, norm_rows, layer, layer_j, batch, seq, w_in, w_out, phi_pe, phi_w1, phi_b1, phi_w2):
    tm = min(TOKEN_TILE, seq)
    w_in_t = jnp.swapaxes(w_in, 1, 2)
    w_gate_t = jnp.pad(w_in_t[layer_j, QKV_WIDTH:, :], ((0, LANES - N_GATES), (0, 0)))
    cos_tok, sin_tok, cos_t, sin_t = _rope_tables(seq)

    feat_t, gates_t, xn = _nsa_inproj_feat(h, norm_rows, layer, w_in_t, layer_j, w_gate_t, tm=tm)
    x_cmp, tok = _nsa_inproj_tok(xn, w_in_t, layer_j, cos_tok, sin_tok, seq, tm=tm)

    pe = phi_pe.reshape(phi_pe.shape[0], 2, 1, CMP_BLOCK * HEAD_DIM)
    b1 = phi_b1.reshape(phi_b1.shape[0], 2, 1, CMP_HIDDEN)
    kc = _compress(x_cmp, 0, pe, phi_w1, b1, phi_w2, layer_j, batch, feature_major=False)
    vc_t = _compress(x_cmp, 1, pe, phi_w1, b1, phi_w2[layer_j, 1].T, layer_j, batch, feature_major=True)

    tq = min(ATTN_STEP, seq)
    o1, selm = _cmp_attn(feat_t, kc, vc_t, gates_t, batch, seq, tq=tq, ts=ATTN_SUBTILE)
    o2 = _sel_attn(feat_t, tok, selm, cos_t, sin_t, gates_t, o1, batch, seq, tq=SEL_QUERIES, tk=SEL_KEYS)
    o3 = _win_attn(feat_t, tok, cos_t, sin_t, gates_t, o2, batch, seq, tq=tq, ts=ATTN_SUBTILE)
    return _outproj(o3, w_out, layer_j, h, norm_rows, layer, 3, tm=tm, tn=PROJ_COLS)


def _conv_inproj_body(h_ref, halo_ref, g_ref, wb_ref, wc_ref, wu_ref, cw_ref, o_ref, xn_ref, inv_ref, *,
                      tiles_per_seq):
    i = pl.program_id(0)
    j = pl.program_id(1)

    def channel_tile():
        x = xn_ref[...]
        bg = _dot(x, wb_ref[...].astype(BF16))
        z = _dot(x, wc_ref[...].astype(BF16)) * _dot(x, wu_ref[...].astype(BF16))
        cw = cw_ref[...]
        conv = cw[2:3] * z + cw[1:2] * pltpu.roll(z, 1, axis=0) + cw[0:1] * pltpu.roll(z, 2, axis=0)
        o_ref[...] = (bg * conv)[CONV_HALO:].astype(BF16)

    @pl.when(j == 0)
    def _():
        keep = jnp.where(i % tiles_per_seq == 0, 0.0, 1.0)
        xn_ref[0:CONV_HALO] = (_rms(halo_ref[...], g_ref[...]) * keep).astype(BF16)
        xn_ref[CONV_HALO:] = _rms_ref(h_ref, inv_ref, g_ref).astype(BF16)
        channel_tile()

    @pl.when(j > 0)
    def _():
        channel_tile()


def _conv_inproj(h, norm_rows, layer, w_in, conv_w, layer_j, seq, tm, tn):
    t = h.shape[0]
    nj = D_MODEL // tn
    halo_blocks = tm // CONV_HALO
    return pl.pallas_call(
        functools.partial(_conv_inproj_body, tiles_per_seq=seq // tm),
        grid=(t // tm, nj),
        in_specs=[
            pl.BlockSpec((tm, D_MODEL), lambda i, j: (i, 0)),
            pl.BlockSpec((CONV_HALO, D_MODEL), lambda i, j: (jnp.maximum(i * halo_blocks - 1, 0), 0)),
            _norm_row_spec(layer, 2),
            pl.BlockSpec((None, D_MODEL, tn), lambda i, j: (layer_j, 0, j)),
            pl.BlockSpec((None, D_MODEL, tn), lambda i, j: (layer_j, 0, nj + j)),
            pl.BlockSpec((None, D_MODEL, tn), lambda i, j: (layer_j, 0, 2 * nj + j)),
            pl.BlockSpec((None, CONV_WIDTH, tn), lambda i, j: (layer_j, 0, j)),
        ],
        out_specs=pl.BlockSpec((tm, tn), lambda i, j: (i, j)),
        out_shape=jax.ShapeDtypeStruct((t, D_MODEL), BF16),
        scratch_shapes=[pltpu.VMEM((tm + CONV_HALO, D_MODEL), BF16), _inv_scratch(tm)],
        compiler_params=_params(("parallel", "arbitrary")),
        name="conv_inproj",
    )(h, h, norm_rows, w_in, w_in, w_in, conv_w)


def _sgu_inproj_body(h_ref, g_ref, w_ref, lng_ref, lnb_ref, ws_ref, bs_ref, o_ref, xn_ref, z_ref, inv_ref, *,
                     tm, tn):
    j = pl.program_id(1)

    def column_tile(col):
        z_ref[:, pl.ds(col, tn)] = jax.nn.gelu(_dot(xn_ref[...], w_ref[...].astype(BF16)))

    @pl.when(j == 0)
    def _():
        xn_ref[...] = _rms_ref(h_ref, inv_ref, g_ref).astype(BF16)
        column_tile(0)

    @pl.when(j > 0)
    def _():
        column_tile(pl.multiple_of(j * tn, tn))

    @pl.when(j == pl.num_programs(1) - 1)
    def _():
        row = lax.broadcasted_iota(jnp.int32, (SGU_CHUNK, SGU_CHUNK), 0)
        colm = lax.broadcasted_iota(jnp.int32, (SGU_CHUNK, SGU_CHUNK), 1)
        ws = [jnp.where(colm <= row, ws_ref[grp], 0.0).astype(BF16) for grp in range(SGU_GROUPS)]
        bs = bs_ref[...]
        lng, lnb = lng_ref[...], lnb_ref[...]
        for c in range(tm // SGU_CHUNK):
            rws = slice(c * SGU_CHUNK, (c + 1) * SGU_CHUNK)
            v = z_ref[rws, D_MODEL:]
            mu = jnp.mean(v, axis=-1, keepdims=True)
            var = jnp.mean(jnp.square(v - mu), axis=-1, keepdims=True)
            vn = ((v - mu) * lax.rsqrt(var + EPS) * lng + lnb).astype(BF16)
            for grp in range(SGU_GROUPS):
                cols = slice(grp * SGU_GROUP_DIM, (grp + 1) * SGU_GROUP_DIM)
                sv = _dot(ws[grp], vn[:, cols]) + bs[:, grp:grp + 1]
                o_ref[rws, cols] = (z_ref[rws, cols] * sv).astype(BF16)


def _sgu_inproj(h, norm_rows, layer, w_in, ln_g, ln_b, w_s, b_s_t, layer_j, tm, tn):
    t = h.shape[0]
    return pl.pallas_call(
        functools.partial(_sgu_inproj_body, tm=tm, tn=tn),
        grid=(t // tm, 2 * D_MODEL // tn),
        in_specs=[
            pl.BlockSpec((tm, D_MODEL), lambda i, j: (i, 0)),
            _norm_row_spec(layer, 2),
            pl.BlockSpec((None, D_MODEL, tn), lambda i, j: (layer_j, 0, j)),
            pl.BlockSpec((None, 1, D_MODEL), lambda i, j: (layer_j, 0, 0)),
            pl.BlockSpec((None, 1, D_MODEL), lambda i, j: (layer_j, 0, 0)),
            pl.BlockSpec((None, SGU_GROUPS, SGU_CHUNK, SGU_CHUNK), lambda i, j: (layer_j, 0, 0, 0)),
            pl.BlockSpec((None, SGU_CHUNK, SGU_GROUPS), lambda i, j: (layer_j, 0, 0)),
        ],
        out_specs=pl.BlockSpec((tm, D_MODEL), lambda i, j: (i, 0)),
        out_shape=jax.ShapeDtypeStruct((t, D_MODEL), BF16),
        scratch_shapes=[pltpu.VMEM((tm, D_MODEL), BF16), pltpu.VMEM((tm, 2 * D_MODEL), F32), _inv_scratch(tm)],
        compiler_params=_params(("parallel", "arbitrary")),
        name="sgu_inproj",
    )(h, norm_rows, w_in, ln_g, ln_b, w_s, b_s_t)


def kernel(x, p, norm_g, ffn1_wg, ffn1_wu, ffn1_wd, ffn2_wg, ffn2_wu, ffn2_wd, ple_wg, ple_wp, nsa_w_in, nsa_w_out, nsa_phi_pe, nsa_phi_w1, nsa_phi_b1, nsa_phi_w2, conv_w_in, conv_w, conv_w_out, sgu_w_in, sgu_ln_g, sgu_ln_b, sgu_w_s, sgu_b_s, sgu_w_out):
    batch, seq, d = x.shape
    depth = p.shape[0]
    t = batch * seq
    assert d == D_MODEL and seq % SEL_KEYS == 0
    h = x.reshape(t, d)
    p2 = p.reshape(depth, t, PLE_DIM)
    norm_rows = norm_g.reshape(depth * N_NORMS, 1, d)
    sgu_ln_g3 = sgu_ln_g.reshape(-1, 1, d)
    sgu_ln_b3 = sgu_ln_b.reshape(-1, 1, d)
    sgu_b_s_t = jnp.swapaxes(sgu_b_s, 1, 2)
    tm = min(TOKEN_TILE, seq)

    for layer in range(depth):
        layer_j = layer // N_MIXERS
        h = _ffn(h, norm_rows, layer, 0, 1, ffn1_wg, ffn1_wu, ffn1_wd, tm=tm, tf=FFN_SLICE)
        kind = layer % N_MIXERS
        if kind == 0:
            h = _nsa_mixer(h, norm_rows, layer, layer_j, batch, seq, nsa_w_in, nsa_w_out, nsa_phi_pe,
                           nsa_phi_w1, nsa_phi_b1, nsa_phi_w2)
        elif kind == 1:
            a = _conv_inproj(h, norm_rows, layer, conv_w_in, conv_w, layer_j, seq, tm=tm, tn=PROJ_COLS)
            h = _outproj(a, conv_w_out, layer_j, h, norm_rows, layer, 3, tm=tm, tn=PROJ_COLS)
        else:
            a = _sgu_inproj(h, norm_rows, layer, sgu_w_in, sgu_ln_g3, sgu_ln_b3, sgu_w_s, sgu_b_s_t, layer_j,
                            tm=tm, tn=SGU_COLS)
            h = _outproj(a, sgu_w_out, layer_j, h, norm_rows, layer, 3, tm=tm, tn=PROJ_COLS)
        h = _ffn(h, norm_rows, layer, 4, 5, ffn2_wg, ffn2_wu, ffn2_wd, tm=tm, tf=FFN_SLICE)
        h = _ple(h, p2, norm_rows, layer, ple_wg, ple_wp, tm=tm, tn=PROJ_COLS)
    return h.reshape(batch, seq, d)
```
